```python
import math
import jax, jax.numpy as jnp
from jax import lax
import numpy as np

D_MODEL = 2048
BATCH = 1
SEQ = 8192
DEPTH = 2
DEC_BATCH = 32
DEC_SEQ = 4
PAST_LEN = 8192
PAGE_SIZE = 128

N_META = 16
N_AB = (DEPTH + 1) // 2
N_C = DEPTH // 2
A_WIDTH = D_MODEL // 2
A_HEAD_DIM = 128
A_HEADS = A_WIDTH // (2 * A_HEAD_DIM)
B_WIDTH = D_MODEL // 2
B_BLOCKS = 4
B_BLOCK = B_WIDTH // B_BLOCKS
CONV_W = 4
LRU_C = 8.0
POOL_WINDOWS = (2, 4, 8, 16)
POOL_GROUPS = 4
POOL_GW = D_MODEL // POOL_GROUPS
POOL_BUF = max(POOL_WINDOWS) - 1
D_FF = 4 * D_MODEL
IN_COLS = 3 * A_WIDTH + 2 * B_WIDTH
SPLITS = (A_WIDTH, 2 * A_WIDTH, 3 * A_WIDTH, 3 * A_WIDTH + B_WIDTH)
Q_BLOCK = 128
DN_ALPHA = (2 * DEPTH) ** 0.25
DN_BETA = (8 * DEPTH) ** -0.25
LN_EPS = 1e-5
RMS_EPS = 1e-5

kernel_name = 'hybrid_diffattn_rglru_pool_step'


def layer_norm(x, g, b):
    xf = x.astype(jnp.float32)
    mu = jnp.mean(xf, axis=-1, keepdims=True)
    var = jnp.mean(jnp.square(xf - mu), axis=-1, keepdims=True)
    y = (xf - mu) * lax.rsqrt(var + LN_EPS)
    return (y * g.astype(jnp.float32) + b.astype(jnp.float32)).astype(x.dtype)


def rms_norm(x, g):
    xf = x.astype(jnp.float32)
    y = xf * lax.rsqrt(jnp.mean(xf * xf, axis=-1, keepdims=True) + RMS_EPS)
    return (y * g.astype(jnp.float32)).astype(x.dtype)


def alibi_slopes():
    return 2.0 ** (-8.0 * jnp.arange(1, A_HEADS + 1, dtype=jnp.float32) / A_HEADS)


def diff_attn_core(q, k_parts, v_parts, kpos_parts, qpos, lam):
    slopes = alibi_slopes()[:, None, None, None]
    scale = A_HEAD_DIM ** -0.5
    logits = []
    for k, kpos in zip(k_parts, kpos_parts):
        s = jnp.einsum('bqhcd,bkhcd->bhcqk', q, k, preferred_element_type=jnp.float32) * scale
        dist = (qpos[:, None] - kpos[None, :]).astype(jnp.float32)
        s = s - slopes * dist
        s = jnp.where(kpos[None, :] <= qpos[:, None], s, -jnp.inf)
        logits.append(s)
    p = jax.nn.softmax(jnp.concatenate(logits, axis=-1), axis=-1)
    attn = p[:, :, 0] - lam * p[:, :, 1]
    outs = []
    start = 0
    for v in v_parts:
        n = v.shape[1]
        outs.append(jnp.einsum('bhqk,bkhe->bqhe', attn[..., start:start + n].astype(v.dtype), v))
        start += n
    return sum(outs[1:], outs[0])


def diff_attn_prompt(q, k, v, lam):
    bsz, t_len = q.shape[:2]
    t_pad = -(-t_len // Q_BLOCK) * Q_BLOCK
    pad = t_pad - t_len
    qp = jnp.pad(q, ((0, 0), (0, pad), (0, 0), (0, 0), (0, 0)))
    kp = jnp.pad(k, ((0, 0), (0, pad), (0, 0), (0, 0), (0, 0)))
    vp = jnp.pad(v, ((0, 0), (0, pad), (0, 0), (0, 0)))
    kpos = jnp.arange(t_pad)
    nb = t_pad // Q_BLOCK
    qb = qp.reshape(bsz, nb, Q_BLOCK, A_HEADS, 2, A_HEAD_DIM).transpose(1, 0, 2, 3, 4, 5)
    starts = jnp.arange(nb) * Q_BLOCK

    def one_block(args):
        q_blk, s0 = args
        qpos = s0 + jnp.arange(Q_BLOCK)
        return diff_attn_core(q_blk, (kp,), (vp,), (kpos,), qpos, lam)

    out = lax.map(one_block, (qb, starts))
    out = out.transpose(1, 0, 2, 3, 4).reshape(bsz, t_pad, A_HEADS, 2 * A_HEAD_DIM)
    return out[:, :t_len]


def causal_conv(xb, buf, w, b):
    t_len = xb.shape[1]
    xp = jnp.concatenate([buf.astype(xb.dtype), xb], axis=1)
    y = b + sum(w[j] * xp[:, j:j + t_len] for j in range(CONV_W))
    return y, xp[:, -(CONV_W - 1):]


def rg_lru(xc, h0, start, ga_w, ga_b, gx_w, gx_b, lru_lambda):
    f32 = jnp.float32
    bsz, t_len, width = xc.shape
    xf = xc.astype(f32)
    xg = xf.reshape(bsz, t_len, B_BLOCKS, B_BLOCK)
    r = jax.nn.sigmoid(jnp.einsum('btni,nij->btnj', xg, ga_w.astype(f32)).reshape(bsz, t_len, width) + ga_b.astype(f32))
    i = jax.nn.sigmoid(jnp.einsum('btni,nij->btnj', xg, gx_w.astype(f32)).reshape(bsz, t_len, width) + gx_b.astype(f32))
    log_a = -LRU_C * r * jax.nn.softplus(-lru_lambda.astype(f32))
    a = jnp.exp(log_a)
    mult = jnp.sqrt(-jnp.expm1(2.0 * log_a))
    pos = start + jnp.arange(t_len)
    mult = jnp.where((pos == 0)[None, :, None], 1.0, mult)
    u = xf * i * mult

    def step(h, au):
        a_t, u_t = au
        h = a_t * h + u_t
        return h, h

    h_last, hs = lax.scan(step, h0.astype(f32), (a.transpose(1, 0, 2), u.transpose(1, 0, 2)))
    return hs.transpose(1, 0, 2), h_last


def ab_mixer(x, start, conv_buf, h0, past_kv, layer_idx, w_in, lam_q1, lam_k1, lam_q2, lam_k2,
             sub_g, conv_w, conv_b, ga_w, ga_b, gx_w, gx_b, lru_lambda, w_out):
    f32 = jnp.float32
    bsz, t_len, _ = x.shape
    z = x @ w_in
    q, k, v, xb, gb = jnp.split(z, SPLITS, axis=-1)
    q = q.reshape(bsz, t_len, A_HEADS, 2, A_HEAD_DIM)
    k = k.reshape(bsz, t_len, A_HEADS, 2, A_HEAD_DIM)
    v = v.reshape(bsz, t_len, A_HEADS, 2 * A_HEAD_DIM)
    lam_init = 0.8 - 0.6 * math.exp(-0.3 * layer_idx)
    lam = (jnp.exp(jnp.sum(lam_q1.astype(f32) * lam_k1.astype(f32)))
           - jnp.exp(jnp.sum(lam_q2.astype(f32) * lam_k2.astype(f32))) + lam_init)
    if past_kv is None:
        o = diff_attn_prompt(q, k, v, lam)
    else:
        past_k, past_v = past_kv
        qpos = start + jnp.arange(t_len)
        kpos_past = jnp.arange(past_k.shape[1])
        o = diff_attn_core(q, (past_k, k), (past_v, v), (kpos_past, qpos), qpos, lam)
    o = rms_norm(o, sub_g) * (1.0 - lam_init)
    xc, new_buf = causal_conv(xb, conv_buf, conv_w, conv_b)
    h, h_last = rg_lru(xc, h0, start, ga_w, ga_b, gx_w, gx_b, lru_lambda)
    yb = h.astype(x.dtype) * jax.nn.gelu(gb, approximate=True)
    mixed = jnp.concatenate([o.reshape(bsz, t_len, A_WIDTH), yb], axis=-1) @ w_out
    return mixed, k.reshape(bsz, t_len, A_HEADS, 2 * A_HEAD_DIM), v, new_buf, h_last


def pool_mixer(x, start, buf, w_grp, scale):
    f32 = jnp.float32
    bsz, t_len, _ = x.shape
    xf = x.astype(f32)
    xp = jnp.concatenate([buf.astype(f32), xf], axis=1)
    cs = jnp.pad(jnp.cumsum(xp, axis=1), ((0, 0), (1, 0), (0, 0)))
    pos = start + jnp.arange(t_len)
    outs = []
    for g, w in enumerate(POOL_WINDOWS):
        sl = slice(g * POOL_GW, (g + 1) * POOL_GW)
        hi = cs[:, POOL_BUF + 1:POOL_BUF + 1 + t_len, sl]
        lo = cs[:, POOL_BUF + 1 - w:POOL_BUF + 1 - w + t_len, sl]
        cnt = jnp.minimum(w, pos + 1).astype(f32)[None, :, None]
        outs.append((hi - lo) / cnt - xf[..., sl])
    m = jnp.stack(outs, axis=2)
    y = jnp.einsum('btgi,gij->btgj', m, w_grp.astype(f32)).reshape(bsz, t_len, D_MODEL)
    y = y * scale.astype(f32)
    return y.astype(x.dtype), xp[:, -POOL_BUF:].astype(x.dtype)


def sq_relu_mlp(x, w1, w2):
    return jnp.square(jax.nn.relu(x @ w1)) @ w2


def setup_inputs(seed: int = 0) -> dict:
    key = jax.random.key(seed)
    ks = jax.random.split(key, 40)
    f32 = jnp.float32
    n_pages = PAST_LEN // PAGE_SIZE
    n_used = DEC_BATCH * n_pages
    n_pool = n_used + n_used // 4

    def nrm(k, shape, s):
        return s * jax.random.normal(k, shape, f32)

    x_prompt = nrm(ks[0], (BATCH, SEQ, D_MODEL), 1.0)
    x_sample = nrm(ks[1], (DEC_BATCH, DEC_SEQ, D_MODEL), 1.0)
    cache_k = nrm(ks[2], (N_AB, n_pool, PAGE_SIZE, A_HEADS, 2 * A_HEAD_DIM), 1.0)
    cache_v = nrm(ks[3], (N_AB, n_pool, PAGE_SIZE, A_HEADS, 2 * A_HEAD_DIM), DN_BETA)
    state_conv = nrm(ks[4], (N_AB, DEC_BATCH, CONV_W - 1, B_WIDTH), 1.0)
    state_lru = nrm(ks[5], (N_AB, DEC_BATCH, B_WIDTH), 0.5)
    state_pool = nrm(ks[6], (N_C, DEC_BATCH, POOL_BUF, D_MODEL), 1.0)
    page_table = jax.random.permutation(ks[7], n_pool)[:n_used].reshape(DEC_BATCH, n_pages).astype(jnp.int32)
    meta_tokens = nrm(ks[8], (N_META, D_MODEL), 1.0)
    col_scale = jnp.concatenate([jnp.ones((2 * A_WIDTH,), f32), jnp.full((A_WIDTH,), DN_BETA, f32),
                                 jnp.ones((2 * B_WIDTH,), f32)])
    w_in = nrm(ks[9], (N_AB, D_MODEL, IN_COLS), D_MODEL ** -0.5) * col_scale
    lam_q1 = nrm(ks[10], (N_AB, A_HEAD_DIM), 0.1)
    lam_k1 = nrm(ks[11], (N_AB, A_HEAD_DIM), 0.1)
    lam_q2 = nrm(ks[12], (N_AB, A_HEAD_DIM), 0.1)
    lam_k2 = nrm(ks[13], (N_AB, A_HEAD_DIM), 0.1)
    sub_norm_g = 1.0 + nrm(ks[14], (N_AB, 2 * A_HEAD_DIM), 0.02)
    conv_w = nrm(ks[15], (N_AB, CONV_W, B_WIDTH), CONV_W ** -0.5)
    conv_b = nrm(ks[16], (N_AB, B_WIDTH), 0.01)
    gate_a_w = nrm(ks[17], (N_AB, B_BLOCKS, B_BLOCK, B_BLOCK), B_BLOCK ** -0.5)
    gate_a_b = nrm(ks[18], (N_AB, B_WIDTH), 0.01)
    gate_x_w = nrm(ks[19], (N_AB, B_BLOCKS, B_BLOCK, B_BLOCK), B_BLOCK ** -0.5)
    gate_x_b = nrm(ks[20], (N_AB, B_WIDTH), 0.01)
    a_c = jax.random.uniform(ks[21], (N_AB, B_WIDTH), f32, 0.9, 0.999)
    s = a_c ** (1.0 / LRU_C)
    lru_lambda = jnp.log(s) - jnp.log1p(-s)
    w_out_ab = nrm(ks[22], (N_AB, D_MODEL, D_MODEL), D_MODEL ** -0.5 * DN_BETA)
    pool_w = nrm(ks[23], (N_C, POOL_GROUPS, POOL_GW, POOL_GW), POOL_GW ** -0.5 * DN_BETA)
    pool_scale = 1.0 + nrm(ks[24], (N_C, D_MODEL), 0.02)
    mix_ln_g = 1.0 + nrm(ks[25], (DEPTH, D_MODEL), 0.02)
    mix_ln_b = nrm(ks[26], (DEPTH, D_MODEL), 0.02)
    w_ff1 = nrm(ks[27], (DEPTH, D_MODEL, D_FF), D_MODEL ** -0.5 * DN_BETA)
    w_ff2 = nrm(ks[28], (DEPTH, D_FF, D_MODEL), D_FF ** -0.5 * DN_BETA)
    ff_ln_g = 1.0 + nrm(ks[29], (DEPTH, D_MODEL), 0.02)
    ff_ln_b = nrm(ks[30], (DEPTH, D_MODEL), 0.02)
    return {'x_prompt': x_prompt, 'x_sample': x_sample, 'cache_k': cache_k, 'cache_v': cache_v,
            'state_conv': state_conv, 'state_lru': state_lru, 'state_pool': state_pool,
            'page_table': page_table, 'meta_tokens': meta_tokens, 'w_in': w_in,
            'lam_q1': lam_q1, 'lam_k1': lam_k1, 'lam_q2': lam_q2, 'lam_k2': lam_k2,
            'sub_norm_g': sub_norm_g, 'conv_w': conv_w, 'conv_b': conv_b,
            'gate_a_w': gate_a_w, 'gate_a_b': gate_a_b, 'gate_x_w': gate_x_w, 'gate_x_b': gate_x_b,
            'lru_lambda': lru_lambda, 'w_out_ab': w_out_ab, 'pool_w': pool_w, 'pool_scale': pool_scale,
            'mix_ln_g': mix_ln_g, 'mix_ln_b': mix_ln_b, 'w_ff1': w_ff1, 'w_ff2': w_ff2,
            'ff_ln_g': ff_ln_g, 'ff_ln_b': ff_ln_b}


def reference(x_prompt, x_sample, cache_k, cache_v, state_conv, state_lru, state_pool, page_table,
              meta_tokens, w_in, lam_q1, lam_k1, lam_q2, lam_k2, sub_norm_g, conv_w, conv_b,
              gate_a_w, gate_a_b, gate_x_w, gate_x_b, lru_lambda, w_out_ab, pool_w, pool_scale,
              mix_ln_g, mix_ln_b, w_ff1, w_ff2, ff_ln_g, ff_ln_b):
    dt = x_prompt.dtype
    n_prompt = x_prompt.shape[0]
    n_dec = x_sample.shape[0]
    meta = jnp.broadcast_to(meta_tokens.astype(dt)[None], (n_prompt, N_META, D_MODEL))
    xp = jnp.concatenate([meta, x_prompt], axis=1)
    xs = x_sample
    past_len = page_table.shape[1] * PAGE_SIZE
    kp_l, vp_l, cp_l, hp_l, pp_l = [], [], [], [], []
    ks_l, vs_l, cs_l, hs_l, ps_l = [], [], [], [], []
    for layer in range(DEPTH):
        j = layer // 2
        if layer % 2 == 0:
            params = (w_in[j], lam_q1[j], lam_k1[j], lam_q2[j], lam_k2[j], sub_norm_g[j],
                      conv_w[j], conv_b[j], gate_a_w[j], gate_a_b[j], gate_x_w[j], gate_x_b[j],
                      lru_lambda[j], w_out_ab[j])
            mp, k_new, v_new, cb_new, h_new = ab_mixer(
                xp, 0, jnp.zeros((n_prompt, CONV_W - 1, B_WIDTH), dt),
                jnp.zeros((n_prompt, B_WIDTH), jnp.float32), None, layer, *params)
            kp_l.append(k_new); vp_l.append(v_new); cp_l.append(cb_new); hp_l.append(h_new)
            past_k = cache_k[j, page_table].reshape(n_dec, past_len, A_HEADS, 2, A_HEAD_DIM)
            past_v = cache_v[j, page_table].reshape(n_dec, past_len, A_HEADS, 2 * A_HEAD_DIM)
            ms, k_new, v_new, cb_new, h_new = ab_mixer(
                xs, past_len, state_conv[j], state_lru[j], (past_k, past_v), layer, *params)
            ks_l.append(k_new); vs_l.append(v_new); cs_l.append(cb_new); hs_l.append(h_new)
        else:
            mp, pb_new = pool_mixer(xp, 0, jnp.zeros((n_prompt, POOL_BUF, D_MODEL), dt), pool_w[j], pool_scale[j])
            pp_l.append(pb_new)
            ms, pb_new = pool_mixer(xs, past_len, state_pool[j], pool_w[j], pool_scale[j])
            ps_l.append(pb_new)
        xp = layer_norm(DN_ALPHA * xp + mp, mix_ln_g[layer], mix_ln_b[layer])
        xs = layer_norm(DN_ALPHA * xs + ms, mix_ln_g[layer], mix_ln_b[layer])
        xp = layer_norm(DN_ALPHA * xp + sq_relu_mlp(xp, w_ff1[layer], w_ff2[layer]), ff_ln_g[layer], ff_ln_b[layer])
        xs = layer_norm(DN_ALPHA * xs + sq_relu_mlp(xs, w_ff1[layer], w_ff2[layer]), ff_ln_g[layer], ff_ln_b[layer])
    y_prompt = xp[:, N_META:]
    y_sample = xs
    new_k_prompt = jnp.stack(kp_l)
    new_v_prompt = jnp.stack(vp_l)
    new_conv_prompt = jnp.stack(cp_l)
    new_lru_prompt = jnp.stack(hp_l)
    new_pool_prompt = jnp.stack(pp_l)
    new_k_sample = jnp.stack(ks_l)
    new_v_sample = jnp.stack(vs_l)
    new_conv_sample = jnp.stack(cs_l)
    new_lru_sample = jnp.stack(hs_l)
    new_pool_sample = jnp.stack(ps_l)
    return (y_prompt, y_sample, new_k_prompt, new_v_prompt, new_conv_prompt, new_lru_prompt,
            new_pool_prompt, new_k_sample, new_v_sample, new_conv_sample, new_lru_sample,
            new_pool_sample)
```

```python
import functools
import math

import jax
import jax.numpy as jnp
from jax import lax
from jax.experimental import pallas as pl
from jax.experimental.pallas import tpu as pltpu

F32 = jnp.float32
BF16 = jnp.bfloat16

LN_EPS = 1e-5
RMS_EPS = 1e-5
LRU_C = 8.0
POOL_WINDOWS = (2, 4, 8, 16)
NEG = -1e30

ROW_ALIGN = 768
TM_PROJ = 768
TN_PROJ = 1024
T_ATT = 768
TM_LRU = 256
TM_OUT = 384
TM_MLP = 768
TF_MLP = 512
TM_POOL = 256
VMEM_MB = 56


def _params(sem, mb=VMEM_MB):
    return pltpu.CompilerParams(dimension_semantics=sem, vmem_limit_bytes=mb * 2**20)


def _layer_norm(y, g, b):
    mu = jnp.mean(y, axis=-1, keepdims=True)
    d = y - mu
    var = jnp.mean(d * d, axis=-1, keepdims=True)
    return d * lax.rsqrt(var + LN_EPS) * g + b


def _dot(a, b):
    return jnp.dot(a, b, preferred_element_type=F32)


def _dot_nt(a, b):
    return lax.dot_general(a, b, (((1,), (1,)), ((), ())), preferred_element_type=F32)


def _sigmoid(x):
    return 1.0 / (1.0 + jnp.exp(-x))


def _gelu_tanh(x):
    c = math.sqrt(2.0 / math.pi)
    return 0.5 * x * (1.0 + jnp.tanh(c * (x + 0.044715 * (x * x * x))))


def _lambda_value(lq1, lk1, lq2, lk2, lam_init):
    s1 = jnp.sum(lq1 * lk1, axis=-1, keepdims=True)
    s2 = jnp.sum(lq2 * lk2, axis=-1, keepdims=True)
    return jnp.exp(s1) - jnp.exp(s2) + lam_init


def _div_pow2(x, n):
    assert n & (n - 1) == 0
    return lax.shift_right_logical(x, n.bit_length() - 1)


def _mod_pow2(x, n):
    assert n & (n - 1) == 0
    return lax.bitwise_and(x, n - 1)


def _head_slope(h, n_heads):
    return 2.0 ** (-8.0 * (h + 1) / n_heads)


def _in_proj_kernel(x_ref, w_ref, z_ref):
    z_ref[...] = _dot(x_ref[...].astype(BF16), w_ref[...])


def _in_proj(x, w, tm, tn):
    r, d = x.shape
    n = w.shape[1]
    return pl.pallas_call(
        _in_proj_kernel,
        grid=(r // tm, n // tn),
        in_specs=[pl.BlockSpec((tm, d), lambda i, j: (i, 0)),
                  pl.BlockSpec((d, tn), lambda i, j: (0, j))],
        out_specs=pl.BlockSpec((tm, tn), lambda i, j: (i, j)),
        out_shape=jax.ShapeDtypeStruct((r, n), F32),
        compiler_params=_params(("parallel", "arbitrary")),
        name="in_proj",
    )(x, w)


def _attn_prompt_kernel(qi_ref, kj_ref, q_ref, k_ref, v_ref, lq1_ref, lk1_ref, lq2_ref, lk2_ref,
                        g_ref, o_ref, m_sc, l_sc, acc_sc, *, t, n_heads, hd, lam_init):
    p = pl.program_id(0)
    i = qi_ref[p]
    j = kj_ref[p]
    scale = hd ** -0.5

    @pl.when(j == 0)
    def _():
        m_sc[...] = jnp.full_like(m_sc, NEG)
        l_sc[...] = jnp.zeros_like(l_sc)
        acc_sc[...] = jnp.zeros_like(acc_sc)

    qpos = i * t + lax.broadcasted_iota(jnp.int32, (t, t), 0)
    kpos = j * t + lax.broadcasted_iota(jnp.int32, (t, t), 1)
    dist = (qpos - kpos).astype(F32)
    causal = kpos <= qpos

    for h in range(n_heads):
        slope = _head_slope(h, n_heads)
        c0 = 2 * hd * h
        qh = (q_ref[:, c0:c0 + 2 * hd] * scale).astype(BF16)
        kh = k_ref[:, c0:c0 + 2 * hd].astype(BF16)
        vh = v_ref[:, c0:c0 + 2 * hd].astype(BF16)
        for c in range(2):
            idx = 2 * h + c
            s = _dot_nt(qh[:, c * hd:(c + 1) * hd], kh[:, c * hd:(c + 1) * hd])
            s = jnp.where(causal, s - slope * dist, NEG)
            m_old = m_sc[idx]
            m_new = jnp.maximum(m_old, jnp.max(s, axis=1, keepdims=True))
            alpha = jnp.exp(m_old - m_new)
            pr = jnp.exp(s - m_new)
            l_sc[idx] = alpha * l_sc[idx] + jnp.sum(pr, axis=1, keepdims=True)
            acc_sc[idx] = alpha * acc_sc[idx] + _dot(pr.astype(BF16), vh)
            m_sc[idx] = m_new

    @pl.when(j == i)
    def _():
        lam = _lambda_value(lq1_ref[...], lk1_ref[...], lq2_ref[...], lk2_ref[...], lam_init)
        g = g_ref[...]
        for h in range(n_heads):
            o1 = acc_sc[2 * h] / l_sc[2 * h]
            o2 = acc_sc[2 * h + 1] / l_sc[2 * h + 1]
            oh = o1 - lam * o2
            oh = oh * lax.rsqrt(jnp.mean(oh * oh, axis=-1, keepdims=True) + RMS_EPS)
            oh = oh * g * (1.0 - lam_init)
            o_ref[:, 2 * hd * h:2 * hd * (h + 1)] = oh.astype(o_ref.dtype)


def _attn_prompt(z, lam_params, sub_g, *, t, n_heads, hd, lam_init):
    r = z.shape[0]
    aw = 2 * hd * n_heads
    nq = r // t
    qi = jnp.asarray([i for i in range(nq) for _ in range(i + 1)], jnp.int32)
    kj = jnp.asarray([j for i in range(nq) for j in range(i + 1)], jnp.int32)
    vec = lambda n: pl.BlockSpec((1, n), lambda p, qi, kj: (0, 0))
    grid_spec = pltpu.PrefetchScalarGridSpec(
        num_scalar_prefetch=2,
        grid=(int(qi.shape[0]),),
        in_specs=[pl.BlockSpec((t, aw), lambda p, qi, kj: (qi[p], 0)),
                  pl.BlockSpec((t, aw), lambda p, qi, kj: (kj[p], 1)),
                  pl.BlockSpec((t, aw), lambda p, qi, kj: (kj[p], 2)),
                  vec(hd), vec(hd), vec(hd), vec(hd), vec(2 * hd)],
        out_specs=pl.BlockSpec((t, aw), lambda p, qi, kj: (qi[p], 0)),
        scratch_shapes=[pltpu.VMEM((2 * n_heads, t, 1), F32),
                        pltpu.VMEM((2 * n_heads, t, 1), F32),
                        pltpu.VMEM((2 * n_heads, t, 2 * hd), F32)],
    )
    kern = functools.partial(_attn_prompt_kernel, t=t, n_heads=n_heads, hd=hd, lam_init=lam_init)
    return pl.pallas_call(
        kern, grid_spec=grid_spec,
        out_shape=jax.ShapeDtypeStruct((r, aw), BF16),
        compiler_params=_params(("arbitrary",)),
        name="attn_prompt",
    )(qi, kj, z, z, z, *lam_params, sub_g)


def _attn_sample_kernel(pt_ref, q_ref, kn_ref, vn_ref, kc_ref, vc_ref, lq1_ref, lk1_ref, lq2_ref,
                        lk2_ref, g_ref, o_ref, qm_sc, m_sc, l_sc, acc_sc, *, n_heads, hd, s_len,
                        past_len, page, lam_init):
    del pt_ref
    p = pl.program_id(1)
    n_pages = pl.num_programs(1)
    aw = 2 * hd * n_heads
    nr = 2 * n_heads * s_len
    grp = 2 * s_len
    scale = hd ** -0.5

    def row_info(shape):
        row = lax.broadcasted_iota(jnp.int32, shape, 0)
        head = _div_pow2(row, grp)
        tok = _mod_pow2(row, s_len)
        slope = jnp.zeros(shape, F32)
        for h in range(n_heads):
            slope = jnp.where(head == h, _head_slope(h, n_heads), slope)
        return tok, slope

    @pl.when(p == 0)
    def _():
        row = lax.broadcasted_iota(jnp.int32, (nr, aw), 0)
        lane = lax.broadcasted_iota(jnp.int32, (nr, aw), 1)
        qsel = jnp.zeros((nr, aw), F32)
        for tt in range(s_len):
            qsel = jnp.where(_mod_pow2(row, s_len) == tt, q_ref[0, tt:tt + 1, :], qsel)
        qm = jnp.where(_div_pow2(lane, hd) == _div_pow2(row, s_len), qsel * scale, 0.0)
        qm_sc[...] = qm.astype(BF16)
        m_sc[...] = jnp.full_like(m_sc, NEG)
        l_sc[...] = jnp.zeros_like(l_sc)
        acc_sc[...] = jnp.zeros_like(acc_sc)

    tok, slope = row_info((nr, page))
    s = _dot_nt(qm_sc[...], kc_ref[0].astype(BF16))
    kpos = p * page + lax.broadcasted_iota(jnp.int32, (nr, page), 1)
    s = s - slope * (past_len + tok - kpos).astype(F32)
    m_old = m_sc[...]
    m_new = jnp.maximum(m_old, jnp.max(s, axis=1, keepdims=True))
    alpha = jnp.exp(m_old - m_new)
    pr = jnp.exp(s - m_new)
    l_sc[...] = alpha * l_sc[...] + jnp.sum(pr, axis=1, keepdims=True)
    m_sc[...] = m_new
    prb = pr.astype(BF16)
    vp = vc_ref[0].astype(BF16)
    for h in range(n_heads):
        rows = slice(grp * h, grp * (h + 1))
        pv = _dot(prb[rows, :], vp[:, 2 * hd * h:2 * hd * (h + 1)])
        acc_sc[rows, :] = alpha[rows, :] * acc_sc[rows, :] + pv

    @pl.when(p == n_pages - 1)
    def _():
        tok1, slope1 = row_info((nr, 1))
        qm = qm_sc[...].astype(F32)
        s_new = []
        for t2 in range(s_len):
            st = jnp.sum(qm * kn_ref[0, t2:t2 + 1, :], axis=1, keepdims=True)
            st = st - slope1 * (tok1 - t2).astype(F32)
            s_new.append(jnp.where(t2 <= tok1, st, NEG))
        m_fin = m_sc[...]
        for st in s_new:
            m_fin = jnp.maximum(m_fin, st)
        a_fin = jnp.exp(m_sc[...] - m_fin)
        l_fin = a_fin * l_sc[...]
        acc = a_fin * acc_sc[...]
        for t2 in range(s_len):
            pt = jnp.exp(s_new[t2] - m_fin)
            l_fin = l_fin + pt
            vsel = jnp.concatenate(
                [jnp.broadcast_to(vn_ref[0, t2:t2 + 1, 2 * hd * h:2 * hd * (h + 1)], (grp, 2 * hd))
                 for h in range(n_heads)], axis=0)
            acc = acc + pt * vsel
        o = acc / l_fin
        lam = _lambda_value(lq1_ref[...], lk1_ref[...], lq2_ref[...], lk2_ref[...], lam_init)
        g = g_ref[...]
        for h in range(n_heads):
            o1 = o[grp * h:grp * h + s_len, :]
            o2 = o[grp * h + s_len:grp * (h + 1), :]
            oh = o1 - lam * o2
            oh = oh * lax.rsqrt(jnp.mean(oh * oh, axis=-1, keepdims=True) + RMS_EPS)
            o_ref[0, :, 2 * hd * h:2 * hd * (h + 1)] = oh * g * (1.0 - lam_init)


def _attn_sample(page_table, q, k_new, v_new, cache_k, cache_v, lam_params, sub_g, *,
                 n_heads, hd, lam_init):
    bsz, s_len, aw = q.shape
    n_pages = page_table.shape[1]
    page = cache_k.shape[1]
    nr = 2 * n_heads * s_len
    tok = pl.BlockSpec((1, s_len, aw), lambda b, p, pt: (b, 0, 0))
    pg = pl.BlockSpec((1, page, aw), lambda b, p, pt: (pt[b, p], 0, 0))
    vec = lambda n: pl.BlockSpec((1, n), lambda b, p, pt: (0, 0))
    grid_spec = pltpu.PrefetchScalarGridSpec(
        num_scalar_prefetch=1,
        grid=(bsz, n_pages),
        in_specs=[tok, tok, tok, pg, pg, vec(hd), vec(hd), vec(hd), vec(hd), vec(2 * hd)],
        out_specs=pl.BlockSpec((1, s_len, aw), lambda b, p, pt: (b, 0, 0)),
        scratch_shapes=[pltpu.VMEM((nr, aw), BF16),
                        pltpu.VMEM((nr, 1), F32),
                        pltpu.VMEM((nr, 1), F32),
                        pltpu.VMEM((nr, 2 * hd), F32)],
    )
    kern = functools.partial(_attn_sample_kernel, n_heads=n_heads, hd=hd, s_len=s_len,
                             past_len=n_pages * page, page=page, lam_init=lam_init)
    return pl.pallas_call(
        kern, grid_spec=grid_spec,
        out_shape=jax.ShapeDtypeStruct((bsz, s_len, aw), F32),
        compiler_params=_params(("arbitrary", "arbitrary")),
        name="attn_sample",
    )(page_table, q, k_new, v_new, cache_k, cache_v, *lam_params, sub_g)


def _lru_gates(xc, gaw_ref, gab_ref, gxw_ref, gxb_ref, lam_ref):
    n_blk, bs, _ = gaw_ref.shape
    xcb = xc.astype(BF16)
    ra = jnp.concatenate([_dot(xcb[:, n * bs:(n + 1) * bs], gaw_ref[n]) for n in range(n_blk)], axis=1)
    rx = jnp.concatenate([_dot(xcb[:, n * bs:(n + 1) * bs], gxw_ref[n]) for n in range(n_blk)], axis=1)
    r = _sigmoid(ra + gab_ref[...])
    ig = _sigmoid(rx + gxb_ref[...])
    neg_lam = -lam_ref[...]
    softplus = jnp.maximum(neg_lam, 0.0) + jnp.log1p(jnp.exp(-jnp.abs(neg_lam)))
    log_a = -LRU_C * r * softplus
    a = jnp.exp(log_a)
    mult = jnp.sqrt(-jnp.tanh(log_a) * (a * a + 1.0))
    return a, ig, mult


def _lru_prompt_kernel(xb_ref, gb_ref, cw_ref, cb_ref, gaw_ref, gab_ref, gxw_ref, gxb_ref, lam_ref,
                       yb_ref, hl_ref, xbuf, hbuf, hcar, *, tm, conv_w, last_tile, last_row):
    i = pl.program_id(0)
    w = xb_ref.shape[1]
    n_grp = tm // 8

    @pl.when(i == 0)
    def _():
        xbuf[0:8, :] = jnp.zeros((8, w), F32)
        hcar[...] = jnp.zeros_like(hcar)

    xbuf[8:8 + tm, :] = xb_ref[...]
    xc = cb_ref[...] + cw_ref[conv_w - 1:conv_w, :] * xb_ref[...]
    for jj in range(conv_w - 1):
        start = 8 - (conv_w - 1) + jj
        xc = xc + cw_ref[jj:jj + 1, :] * xbuf[start:start + tm, :]
    xbuf[0:8, :] = xbuf[tm:tm + 8, :]

    a, ig, mult = _lru_gates(xc, gaw_ref, gab_ref, gxw_ref, gxb_ref, lam_ref)
    grow = i * tm + lax.broadcasted_iota(jnp.int32, (tm, 1), 0)
    mult = jnp.where(grow == 0, 1.0, mult)
    u = xc * ig * mult

    a3 = a.reshape(n_grp, 8, w)
    u3 = u.reshape(n_grp, 8, w)
    sub = lax.broadcasted_iota(jnp.int32, (n_grp, 8, w), 1)
    for sh in (1, 2, 4):
        a_prev = pltpu.roll(a3, sh, axis=1)
        u_prev = pltpu.roll(u3, sh, axis=1)
        ok = sub >= sh
        u3 = jnp.where(ok, a3 * u_prev + u3, u3)
        a3 = jnp.where(ok, a3 * a_prev, a3)
    h_prev = hcar[...]
    for gi in range(n_grp):
        hg = a3[gi] * h_prev + u3[gi]
        hbuf[8 * gi:8 * (gi + 1), :] = hg
        h_prev = hg[7:8, :]
    hcar[...] = h_prev

    yb_ref[...] = (hbuf[...] * _gelu_tanh(gb_ref[...])).astype(yb_ref.dtype)

    @pl.when(i == last_tile)
    def _():
        hl_ref[...] = hbuf[last_row:last_row + 1, :]


def _lru_prompt(z, conv_w, conv_b, gaw, gab, gxw, gxb, lru_lambda, *, t_len, tm):
    r = z.shape[0]
    w = conv_b.shape[1]
    cw = conv_w.shape[0]
    n_blk, bs, _ = gaw.shape
    full = lambda shape: pl.BlockSpec(shape, lambda i: (0,) * len(shape))
    kern = functools.partial(_lru_prompt_kernel, tm=tm, conv_w=cw,
                             last_tile=(t_len - 1) // tm, last_row=(t_len - 1) % tm)
    return pl.pallas_call(
        kern,
        grid=(r // tm,),
        in_specs=[pl.BlockSpec((tm, w), lambda i: (i, 3)),
                  pl.BlockSpec((tm, w), lambda i: (i, 4)),
                  full((cw, w)), full((1, w)),
                  full((n_blk, bs, bs)), full((1, w)),
                  full((n_blk, bs, bs)), full((1, w)), full((1, w))],
        out_specs=[pl.BlockSpec((tm, w), lambda i: (i, 0)), full((1, w))],
        out_shape=[jax.ShapeDtypeStruct((r, w), BF16), jax.ShapeDtypeStruct((1, w), F32)],
        scratch_shapes=[pltpu.VMEM((tm + 8, w), F32), pltpu.VMEM((tm, w), F32),
                        pltpu.VMEM((1, w), F32)],
        compiler_params=_params(("arbitrary",)),
        name="lru_prompt",
    )(z, z, conv_w, conv_b, gaw, gab, gxw, gxb, lru_lambda)


def _lru_sample_kernel(xb_ref, gb_ref, cs_ref, h0_ref, cw_ref, cb_ref, gaw_ref, gab_ref, gxw_ref,
                       gxb_ref, lam_ref, yb_ref, hl_ref, *, conv_w):
    s_len, bsz, w = xb_ref.shape
    xp = [cs_ref[jj] for jj in range(conv_w - 1)] + [xb_ref[tt] for tt in range(s_len)]
    xcs = []
    for tt in range(s_len):
        xc = cb_ref[...] + cw_ref[0:1, :] * xp[tt]
        for jj in range(1, conv_w):
            xc = xc + cw_ref[jj:jj + 1, :] * xp[tt + jj]
        xcs.append(xc)
    xc = jnp.concatenate(xcs, axis=0)
    a, ig, mult = _lru_gates(xc, gaw_ref, gab_ref, gxw_ref, gxb_ref, lam_ref)
    u = xc * ig * mult
    h = h0_ref[...]
    for tt in range(s_len):
        rows = slice(tt * bsz, (tt + 1) * bsz)
        h = a[rows, :] * h + u[rows, :]
        yb_ref[tt] = (h * _gelu_tanh(gb_ref[tt])).astype(yb_ref.dtype)
    hl_ref[...] = h


def _lru_sample(xb, gb, conv_state, h0, conv_w, conv_b, gaw, gab, gxw, gxb, lru_lambda):
    s_len, bsz, w = xb.shape
    kern = functools.partial(_lru_sample_kernel, conv_w=conv_w.shape[0])
    return pl.pallas_call(
        kern,
        out_shape=[jax.ShapeDtypeStruct((s_len, bsz, w), BF16), jax.ShapeDtypeStruct((bsz, w), F32)],
        compiler_params=pltpu.CompilerParams(vmem_limit_bytes=VMEM_MB * 2**20),
        name="lru_sample",
    )(xb, gb, conv_state, h0, conv_w, conv_b, gaw, gab, gxw, gxb, lru_lambda)


def _out_proj_kernel(o_ref, yb_ref, x_ref, w_ref, g_ref, b_ref, out_ref, *, alpha):
    aw = o_ref.shape[1]
    mixed = _dot(o_ref[...], w_ref[0:aw, :]) + _dot(yb_ref[...], w_ref[aw:, :])
    out_ref[...] = _layer_norm(alpha * x_ref[...] + mixed, g_ref[...], b_ref[...])


def _out_proj(o, yb, x, w, g, b, *, alpha, tm):
    r, d = x.shape
    aw, bw = o.shape[1], yb.shape[1]
    full = lambda shape: pl.BlockSpec(shape, lambda i: (0,) * len(shape))
    return pl.pallas_call(
        functools.partial(_out_proj_kernel, alpha=alpha),
        grid=(r // tm,),
        in_specs=[pl.BlockSpec((tm, aw), lambda i: (i, 0)),
                  pl.BlockSpec((tm, bw), lambda i: (i, 0)),
                  pl.BlockSpec((tm, d), lambda i: (i, 0)),
                  full((aw + bw, d)), full((1, d)), full((1, d))],
        out_specs=pl.BlockSpec((tm, d), lambda i: (i, 0)),
        out_shape=jax.ShapeDtypeStruct((r, d), F32),
        compiler_params=_params(("parallel",)),
        name="out_proj",
    )(o, yb, x, w, g, b)


def _mlp_kernel(x_ref, w1_ref, w2_ref, g_ref, b_ref, out_ref, xb_sc, acc_sc, *, alpha):
    f = pl.program_id(1)

    @pl.when(f == 0)
    def _():
        xb_sc[...] = x_ref[...].astype(BF16)
        acc_sc[...] = jnp.zeros_like(acc_sc)

    h = jnp.maximum(_dot(xb_sc[...], w1_ref[...]), 0.0)
    acc_sc[...] += _dot((h * h).astype(BF16), w2_ref[...])

    @pl.when(f == pl.num_programs(1) - 1)
    def _():
        out_ref[...] = _layer_norm(alpha * x_ref[...] + acc_sc[...], g_ref[...], b_ref[...])


def _mlp(x, w1, w2, g, b, *, alpha, tm, tf):
    r, d = x.shape
    ff = w1.shape[1]
    vec = pl.BlockSpec((1, d), lambda i, f: (0, 0))
    return pl.pallas_call(
        functools.partial(_mlp_kernel, alpha=alpha),
        grid=(r // tm, ff // tf),
        in_specs=[pl.BlockSpec((tm, d), lambda i, f: (i, 0)),
                  pl.BlockSpec((d, tf), lambda i, f: (0, f)),
                  pl.BlockSpec((tf, d), lambda i, f: (f, 0)),
                  vec, vec],
        out_specs=pl.BlockSpec((tm, d), lambda i, f: (i, 0)),
        out_shape=jax.ShapeDtypeStruct((r, d), F32),
        scratch_shapes=[pltpu.VMEM((tm, d), BF16), pltpu.VMEM((tm, d), F32)],
        compiler_params=_params(("parallel", "arbitrary")),
        name="mlp",
    )(x, w1, w2, g, b)


def _pool_project(ms, x, w_ref, sc_ref, g_ref, b_ref, alpha):
    y = jnp.concatenate([_dot(ms[gi].astype(BF16), w_ref[gi]) for gi in range(len(ms))], axis=1)
    return _layer_norm(alpha * x + y * sc_ref[...], g_ref[...], b_ref[...])


def _pool_prompt_kernel(x_ref, w_ref, sc_ref, g_ref, b_ref, out_ref, ebuf, *, tm, alpha, pad):
    i = pl.program_id(0)
    d = x_ref.shape[1]
    gw = d // len(POOL_WINDOWS)

    @pl.when(i == 0)
    def _():
        ebuf[0:pad, :] = jnp.zeros((pad, d), F32)

    x = x_ref[...]
    ebuf[pad:pad + tm, :] = x
    pos = i * tm + lax.broadcasted_iota(jnp.int32, (tm, 1), 0)
    ms = []
    for gi, win in enumerate(POOL_WINDOWS):
        cols = slice(gi * gw, (gi + 1) * gw)
        ws = x[:, cols]
        for kk in range(1, win):
            ws = ws + ebuf[pad - kk:pad - kk + tm, cols]
        cnt = jnp.minimum(win, pos + 1).astype(F32)
        ms.append(ws / cnt - x[:, cols])
    ebuf[0:pad, :] = ebuf[tm:tm + pad, :]
    out_ref[...] = _pool_project(ms, x, w_ref, sc_ref, g_ref, b_ref, alpha)


def _pool_prompt(x, w, sc, g, b, *, alpha, tm):
    r, d = x.shape
    n_g, gw, _ = w.shape
    pad = 16
    assert max(POOL_WINDOWS) - 1 <= pad
    full = lambda shape: pl.BlockSpec(shape, lambda i: (0,) * len(shape))
    return pl.pallas_call(
        functools.partial(_pool_prompt_kernel, tm=tm, alpha=alpha, pad=pad),
        grid=(r // tm,),
        in_specs=[pl.BlockSpec((tm, d), lambda i: (i, 0)),
                  full((n_g, gw, gw)), full((1, d)), full((1, d)), full((1, d))],
        out_specs=pl.BlockSpec((tm, d), lambda i: (i, 0)),
        out_shape=jax.ShapeDtypeStruct((r, d), F32),
        scratch_shapes=[pltpu.VMEM((tm + pad, d), F32)],
        compiler_params=_params(("arbitrary",)),
        name="pool_prompt",
    )(x, w, sc, g, b)


def _pool_sample_kernel(x_ref, st_ref, w_ref, sc_ref, g_ref, b_ref, out_ref, *, alpha):
    s_len, bsz, d = x_ref.shape
    n_buf = st_ref.shape[0]
    gw = d // len(POOL_WINDOWS)
    ext = [st_ref[jj] for jj in range(n_buf)] + [x_ref[tt] for tt in range(s_len)]
    x = jnp.concatenate(ext[n_buf:], axis=0)
    ms = []
    for gi, win in enumerate(POOL_WINDOWS):
        cols = slice(gi * gw, (gi + 1) * gw)
        rows = []
        for tt in range(s_len):
            ws = ext[n_buf + tt][:, cols]
            for kk in range(1, win):
                ws = ws + ext[n_buf + tt - kk][:, cols]
            rows.append(ws / float(win) - ext[n_buf + tt][:, cols])
        ms.append(jnp.concatenate(rows, axis=0))
    y = _pool_project(ms, x, w_ref, sc_ref, g_ref, b_ref, alpha)
    for tt in range(s_len):
        out_ref[tt] = y[tt * bsz:(tt + 1) * bsz, :]


def _pool_sample(x, state, w, sc, g, b, *, alpha):
    return pl.pallas_call(
        functools.partial(_pool_sample_kernel, alpha=alpha),
        out_shape=jax.ShapeDtypeStruct(x.shape, F32),
        compiler_params=pltpu.CompilerParams(vmem_limit_bytes=VMEM_MB * 2**20),
        name="pool_sample",
    )(x, state, w, sc, g, b)


def kernel(x_prompt, x_sample, cache_k, cache_v, state_conv, state_lru, state_pool, page_table, meta_tokens, w_in, lam_q1, lam_k1, lam_q2, lam_k2, sub_norm_g, conv_w, conv_b, gate_a_w, gate_a_b, gate_x_w, gate_x_b, lru_lambda, w_out_ab, pool_w, pool_scale, mix_ln_g, mix_ln_b, w_ff1, w_ff2, ff_ln_g, ff_ln_b):
    n_prompt, seq, d = x_prompt.shape
    bsz, s_len, _ = x_sample.shape
    depth = w_ff1.shape[0]
    assert n_prompt == 1 and depth == 2
    n_meta = meta_tokens.shape[0]
    n_heads, hd2 = cache_k.shape[-2:]
    hd = hd2 // 2
    aw = n_heads * hd2
    bw = state_lru.shape[-1]
    assert aw == bw and w_in.shape[-1] == 3 * aw + 2 * bw
    page = cache_k.shape[2]
    n_buf = state_pool.shape[2]
    alpha = (2.0 * depth) ** 0.25
    lam_init = 0.8 - 0.6 * math.exp(-0.3 * 0)

    t_len = n_meta + seq
    n_s = bsz * s_len
    rows = -(-(t_len + n_s) // ROW_ALIGN) * ROW_ALIGN
    xs_tm = x_sample.transpose(1, 0, 2).reshape(n_s, d)
    x0 = jnp.concatenate([meta_tokens, x_prompt[0], xs_tm,
                          jnp.zeros((rows - t_len - n_s, d), F32)], axis=0)

    row2 = lambda v: v.reshape(1, -1)
    lam_params = (row2(lam_q1[0]), row2(lam_k1[0]), row2(lam_q2[0]), row2(lam_k2[0]))
    sub_g = row2(sub_norm_g[0])

    z = _in_proj(x0, w_in[0].astype(BF16), TM_PROJ, TN_PROJ)
    zs = z[t_len:t_len + n_s]

    o = _attn_prompt(z, lam_params, sub_g, t=T_ATT, n_heads=n_heads, hd=hd, lam_init=lam_init)
    to_bt = lambda v: v.reshape(s_len, bsz, -1).transpose(1, 0, 2)
    o_s = _attn_sample(page_table, to_bt(zs[:, :aw]), to_bt(zs[:, aw:2 * aw]),
                       to_bt(zs[:, 2 * aw:3 * aw]),
                       cache_k[0].reshape(-1, page, aw), cache_v[0].reshape(-1, page, aw),
                       lam_params, sub_g, n_heads=n_heads, hd=hd, lam_init=lam_init)
    o = lax.dynamic_update_slice(o, o_s.transpose(1, 0, 2).reshape(n_s, aw).astype(BF16), (t_len, 0))

    gaw, gxw = gate_a_w[0].astype(BF16), gate_x_w[0].astype(BF16)
    lru_args = (conv_w[0], row2(conv_b[0]), gaw, row2(gate_a_b[0]), gxw, row2(gate_x_b[0]),
                row2(lru_lambda[0]))
    yb, h_last_p = _lru_prompt(z, *lru_args, t_len=t_len, tm=TM_LRU)
    xb_s = zs[:, 3 * aw:3 * aw + bw].reshape(s_len, bsz, bw)
    gb_s = zs[:, 3 * aw + bw:].reshape(s_len, bsz, bw)
    yb_s, h_last_s = _lru_sample(xb_s, gb_s, state_conv[0].transpose(1, 0, 2), state_lru[0], *lru_args)
    yb = lax.dynamic_update_slice(yb, yb_s.reshape(n_s, bw), (t_len, 0))

    x1 = _out_proj(o, yb, x0, w_out_ab[0].astype(BF16), row2(mix_ln_g[0]), row2(mix_ln_b[0]),
                   alpha=alpha, tm=TM_OUT)
    x2 = _mlp(x1, w_ff1[0].astype(BF16), w_ff2[0].astype(BF16), row2(ff_ln_g[0]), row2(ff_ln_b[0]),
              alpha=alpha, tm=TM_MLP, tf=TF_MLP)

    pool_args = (pool_w[0].astype(BF16), row2(pool_scale[0]), row2(mix_ln_g[1]), row2(mix_ln_b[1]))
    x3 = _pool_prompt(x2, *pool_args, alpha=alpha, tm=TM_POOL)
    x2_s = x2[t_len:t_len + n_s].reshape(s_len, bsz, d)
    x3_s = _pool_sample(x2_s, state_pool[0].transpose(1, 0, 2), *pool_args, alpha=alpha)
    x3 = lax.dynamic_update_slice(x3, x3_s.reshape(n_s, d), (t_len, 0))
    x4 = _mlp(x3, w_ff1[1].astype(BF16), w_ff2[1].astype(BF16), row2(ff_ln_g[1]), row2(ff_ln_b[1]),
              alpha=alpha, tm=TM_MLP, tf=TF_MLP)

    kv_p = lambda c: z[:t_len, c * aw:(c + 1) * aw].reshape(1, 1, t_len, n_heads, hd2)
    kv_s = lambda c: to_bt(zs[:, c * aw:(c + 1) * aw]).reshape(1, bsz, s_len, n_heads, hd2)
    cw = conv_w.shape[1]
    y_prompt = x4[n_meta:t_len][None]
    y_sample = to_bt(x4[t_len:t_len + n_s])
    new_conv_prompt = z[t_len - (cw - 1):t_len, 3 * aw:3 * aw + bw][None, None]
    new_lru_prompt = h_last_p[None]
    new_pool_prompt = x2[t_len - n_buf:t_len][None, None]
    new_conv_sample = xb_s[s_len - (cw - 1):].transpose(1, 0, 2)[None]
    new_lru_sample = h_last_s[None]
    new_pool_sample = jnp.concatenate([state_pool[0], x2_s.transpose(1, 0, 2)], axis=1)[:, -n_buf:][None]
    return (y_prompt, y_sample, kv_p(1), kv_p(2), new_conv_prompt, new_lru_prompt, new_pool_prompt,
            kv_s(1), kv_s(2), new_conv_sample, new_lru_sample, new_pool_sample)
```

```python
import functools
import math

import jax
import jax.numpy as jnp
from jax import lax
from jax.experimental import pallas as pl
from jax.experimental.pallas import tpu as pltpu

F32 = jnp.float32
BF16 = jnp.bfloat16

LN_EPS = 1e-5
RMS_EPS = 1e-5
LRU_C = 8.0
POOL_WINDOWS = (2, 4, 8, 16)
NEG = -1e30
LOG2E = 1.4426950408889634
LANES = 128
N_AUG = 3

ROW_ALIGN = 768
T_ATT = 768
TN_PROJ = 1024
TM_LRU = 256
TM_OUT = 384
TM_MLP = 768
TF_MLP = 512
TM_POOL = 256
ATT_HEAD_UNROLL = 2
PAGES_PER_STEP = 8
VMEM_MB = 56


def _params(sem, mb=VMEM_MB):
    return pltpu.CompilerParams(dimension_semantics=sem, vmem_limit_bytes=mb * 2**20)


def _layer_norm(y, g, b):
    mu = jnp.mean(y, axis=-1, keepdims=True)
    d = y - mu
    var = jnp.mean(d * d, axis=-1, keepdims=True)
    return d * lax.rsqrt(var + LN_EPS) * g + b


def _dot(a, b):
    return jnp.dot(a, b, preferred_element_type=F32)


def _dot_nt(a, b):
    return lax.dot_general(a, b, (((1,), (1,)), ((), ())), preferred_element_type=F32)


def _sigmoid(x):
    return 1.0 / (1.0 + jnp.exp(-x))


def _gelu_tanh(x):
    c = math.sqrt(2.0 / math.pi)
    return 0.5 * x * (1.0 + jnp.tanh(c * (x + 0.044715 * (x * x * x))))


def _lambda_value(lq1, lk1, lq2, lk2, lam_init):
    s1 = jnp.sum(lq1 * lk1, axis=-1, keepdims=True)
    s2 = jnp.sum(lq2 * lk2, axis=-1, keepdims=True)
    return jnp.exp(s1) - jnp.exp(s2) + lam_init


def _div_pow2(x, n):
    assert n & (n - 1) == 0
    return lax.shift_right_logical(x, n.bit_length() - 1)


def _mod_pow2(x, n):
    assert n & (n - 1) == 0
    return lax.bitwise_and(x, n - 1)


def _head_slope(h, n_heads):
    return 2.0 ** (-8.0 * (h + 1) / n_heads)


def _in_proj_kernel(x_ref, w_ref, z_ref, qa_ref, ka_ref, vb_ref, *, n_heads, hd):
    j = pl.program_id(1)
    z = _dot(x_ref[...].astype(BF16), w_ref[...])
    z_ref[...] = z
    tm = z.shape[0]
    lane = lax.broadcasted_iota(jnp.int32, (tm, hd), 1)

    @pl.when(j == 0)
    def _():
        ones = jnp.where(lane < N_AUG, 1.0, 0.0).astype(BF16)
        q_scale = hd ** -0.5 * LOG2E
        for h in range(n_heads):
            for c in range(2):
                col = (2 * h + c) * hd
                qa_ref[h, :, 2 * hd * c:2 * hd * c + hd] = (z[:, col:col + hd] * q_scale).astype(BF16)
                qa_ref[h, :, 2 * hd * c + hd:2 * hd * (c + 1)] = ones

    @pl.when(j == 1)
    def _():
        row = lax.broadcasted_iota(jnp.int32, (tm, hd), 0).astype(F32)
        for h in range(n_heads):
            a = row * (_head_slope(h, n_heads) * LOG2E)
            hi = a.astype(BF16).astype(F32)
            mid = (a - hi).astype(BF16).astype(F32)
            lo = a - hi - mid
            aug = jnp.where(lane == 0, hi, jnp.where(lane == 1, mid, jnp.where(lane == 2, lo, 0.0)))
            aug = aug.astype(BF16)
            for c in range(2):
                col = (2 * h + c) * hd
                ka_ref[h, :, 2 * hd * c:2 * hd * c + hd] = z[:, col:col + hd].astype(BF16)
                ka_ref[h, :, 2 * hd * c + hd:2 * hd * (c + 1)] = aug

    @pl.when(j == 2)
    def _():
        for h in range(n_heads):
            vb_ref[h] = z[:, 2 * hd * h:2 * hd * (h + 1)].astype(BF16)


def _in_proj(x, w, *, tm, tn, n_heads, hd):
    r, d = x.shape
    n = w.shape[1]
    assert hd == LANES and tn == 2 * hd * n_heads
    head_blk = lambda width: pl.BlockSpec((n_heads, tm, width), lambda i, j: (0, i, 0))
    return pl.pallas_call(
        functools.partial(_in_proj_kernel, n_heads=n_heads, hd=hd),
        grid=(r // tm, n // tn),
        in_specs=[pl.BlockSpec((tm, d), lambda i, j: (i, 0)),
                  pl.BlockSpec((d, tn), lambda i, j: (0, j))],
        out_specs=[pl.BlockSpec((tm, tn), lambda i, j: (i, j)),
                   head_blk(4 * hd), head_blk(4 * hd), head_blk(2 * hd)],
        out_shape=[jax.ShapeDtypeStruct((r, n), F32),
                   jax.ShapeDtypeStruct((n_heads, r, 4 * hd), BF16),
                   jax.ShapeDtypeStruct((n_heads, r, 4 * hd), BF16),
                   jax.ShapeDtypeStruct((n_heads, r, 2 * hd), BF16)],
        compiler_params=_params(("arbitrary", "arbitrary")),
        name="in_proj",
    )(x, w)


def _attn_prompt_kernel(qi_ref, kj_ref, qa_ref, ka_ref, vb_ref, sl_ref, lq1_ref, lk1_ref, lq2_ref,
                        lk2_ref, g_ref, o_ref, m_sc, l_sc, acc_sc, *, t, n_heads, hd, lam_init):
    p = pl.program_id(0)
    i = qi_ref[p]
    j = kj_ref[p]

    @pl.when(j == 0)
    def _():
        m_sc[...] = jnp.full_like(m_sc, NEG)
        l_sc[...] = jnp.zeros_like(l_sc)
        acc_sc[...] = jnp.zeros_like(acc_sc)

    def tile(masked):
        if masked:
            keep = (lax.broadcasted_iota(jnp.int32, (t, t), 1)
                    <= lax.broadcasted_iota(jnp.int32, (t, t), 0))
        tile_dist = ((j - i) * t).astype(F32)

        def head(h, carry):
            shift = sl_ref[h][:, 0:1] * tile_dist
            for c in range(2):
                s = _dot_nt(qa_ref[h, :, 2 * hd * c:2 * hd * (c + 1)],
                            ka_ref[h, :, 2 * hd * c:2 * hd * (c + 1)])
                if masked:
                    s = jnp.where(keep, s, NEG)
                idx = 2 * h + c
                m_old = m_sc[idx]
                m_new = jnp.maximum(m_old, jnp.max(s, axis=1, keepdims=True) + shift)
                alpha = jnp.exp2(m_old - m_new)
                pr = jnp.exp2(s - (m_new - shift))
                l_sc[idx] = alpha * l_sc[idx] + jnp.sum(pr, axis=1, keepdims=True)
                acc_sc[idx] = alpha * acc_sc[idx] + _dot(pr.astype(BF16), vb_ref[h])
                m_sc[idx] = m_new
            return carry

        lax.fori_loop(0, n_heads, head, 0, unroll=ATT_HEAD_UNROLL)

    @pl.when(j < i)
    def _():
        tile(False)

    @pl.when(j == i)
    def _():
        tile(True)
        lam = _lambda_value(lq1_ref[...], lk1_ref[...], lq2_ref[...], lk2_ref[...], lam_init)
        g = g_ref[...]
        for h in range(n_heads):
            o1 = acc_sc[2 * h] / l_sc[2 * h]
            o2 = acc_sc[2 * h + 1] / l_sc[2 * h + 1]
            oh = o1 - lam * o2
            oh = oh * lax.rsqrt(jnp.mean(oh * oh, axis=-1, keepdims=True) + RMS_EPS)
            o_ref[h] = (oh * g * (1.0 - lam_init)).astype(o_ref.dtype)


def _attn_prompt(qa, ka, vb, lam_params, sub_g, *, t, hd, lam_init):
    n_heads, r, _ = qa.shape
    nq = r // t
    qi = jnp.asarray([i for i in range(nq) for _ in range(i + 1)], jnp.int32)
    kj = jnp.asarray([j for i in range(nq) for j in range(i + 1)], jnp.int32)
    slopes = jnp.asarray([[[_head_slope(h, n_heads) * LOG2E] * LANES] for h in range(n_heads)], F32)
    vec = lambda n: pl.BlockSpec((1, n), lambda p, qi, kj: (0, 0))
    grid_spec = pltpu.PrefetchScalarGridSpec(
        num_scalar_prefetch=2,
        grid=(int(qi.shape[0]),),
        in_specs=[pl.BlockSpec((n_heads, t, 4 * hd), lambda p, qi, kj: (0, qi[p], 0)),
                  pl.BlockSpec((n_heads, t, 4 * hd), lambda p, qi, kj: (0, kj[p], 0)),
                  pl.BlockSpec((n_heads, t, 2 * hd), lambda p, qi, kj: (0, kj[p], 0)),
                  pl.BlockSpec((n_heads, 1, LANES), lambda p, qi, kj: (0, 0, 0)),
                  vec(hd), vec(hd), vec(hd), vec(hd), vec(2 * hd)],
        out_specs=pl.BlockSpec((n_heads, t, 2 * hd), lambda p, qi, kj: (0, qi[p], 0)),
        scratch_shapes=[pltpu.VMEM((2 * n_heads, t, 1), F32),
                        pltpu.VMEM((2 * n_heads, t, 1), F32),
                        pltpu.VMEM((2 * n_heads, t, 2 * hd), F32)],
    )
    kern = functools.partial(_attn_prompt_kernel, t=t, n_heads=n_heads, hd=hd, lam_init=lam_init)
    return pl.pallas_call(
        kern, grid_spec=grid_spec,
        out_shape=jax.ShapeDtypeStruct((n_heads, r, 2 * hd), BF16),
        compiler_params=_params(("arbitrary",)),
        name="attn_prompt",
    )(qi, kj, qa, ka, vb, slopes, *lam_params, sub_g)


def _attn_sample_kernel(pt_ref, q_ref, kn_ref, vn_ref, *rest, n_heads, hd, s_len, past_len, page,
                        gp, lam_init):
    del pt_ref
    page_refs = rest[:2 * gp]
    (lq1_ref, lk1_ref, lq2_ref, lk2_ref, g_ref, o_ref,
     qb_sc, bias_sc, m_sc, l_sc, acc_sc) = rest[2 * gp:]
    p = pl.program_id(1)
    nr = 2 * n_heads * s_len
    kr = 2 * n_heads
    ncol = page * kr
    half = nr // 2
    scale = hd ** -0.5

    def row_info(shape):
        row = lax.broadcasted_iota(jnp.int32, shape, 0)
        tok = _mod_pow2(row, s_len)
        head = _mod_pow2(_div_pow2(row, s_len), n_heads)
        cmap = _div_pow2(row, s_len * n_heads)
        slope = jnp.zeros(shape, F32)
        for h in range(n_heads):
            slope = jnp.where(head == h, _head_slope(h, n_heads), slope)
        return tok, head, cmap, slope

    @pl.when(p == 0)
    def _():
        qb_sc[...] = (q_ref[0] * scale).astype(BF16)
        tok, head, cmap, slope = row_info((nr, ncol))
        col = lax.broadcasted_iota(jnp.int32, (nr, ncol), 1)
        key = _div_pow2(col, kr)
        own = _mod_pow2(col, kr) == cmap * n_heads + head
        bias_sc[...] = jnp.where(own, -slope * (past_len + tok - key).astype(F32), NEG)
        m_sc[...] = jnp.full_like(m_sc, NEG)
        l_sc[...] = jnp.zeros_like(l_sc)
        acc_sc[...] = jnp.zeros_like(acc_sc)

    tok1, _, _, slope1 = row_info((nr, 1))
    m = m_sc[...]
    l = l_sc[...]
    acc = acc_sc[...]
    scores, shifts = [], []
    m_new = m
    for gi in range(gp):
        kp = page_refs[gi][0].astype(BF16)
        s = _dot_nt(qb_sc[...], kp) + bias_sc[...]
        shift = slope1 * ((p * gp + gi) * page).astype(F32)
        m_new = jnp.maximum(m_new, jnp.max(s, axis=1, keepdims=True) + shift)
        scores.append(s)
        shifts.append(shift)
    alpha = jnp.exp(m - m_new)
    l = alpha * l
    pv = jnp.zeros((2 * nr, hd), F32)
    for gi in range(gp):
        pr = jnp.exp(scores[gi] - (m_new - shifts[gi]))
        l = l + jnp.sum(pr, axis=1, keepdims=True)
        top, bot = pr[:half], pr[half:]
        chunks = range(ncol // LANES)
        top_r = jnp.concatenate(
            [pltpu.roll(top[:, LANES * cc:LANES * (cc + 1)], n_heads, 1) for cc in chunks], axis=1)
        bot_r = jnp.concatenate(
            [pltpu.roll(bot[:, LANES * cc:LANES * (cc + 1)], LANES - n_heads, 1) for cc in chunks], axis=1)
        lhs = jnp.concatenate([top, bot_r, top_r, bot], axis=0).astype(BF16)
        vp = page_refs[gp + gi][0].astype(BF16)
        pv = pv + _dot(lhs, vp)
    acc = jnp.concatenate([alpha, alpha], axis=0) * acc + pv
    m = m_new
    m_sc[...] = m
    l_sc[...] = l
    acc_sc[...] = acc

    @pl.when(p == pl.num_programs(1) - 1)
    def _():
        qf = q_ref[0] * scale
        s_new = []
        for t2 in range(s_len):
            st = jnp.sum(qf * kn_ref[0, t2], axis=1, keepdims=True)
            st = st - slope1 * (tok1 - t2).astype(F32)
            s_new.append(jnp.where(t2 <= tok1, st, NEG))
        m_fin = m
        for st in s_new:
            m_fin = jnp.maximum(m_fin, st)
        a_fin = jnp.exp(m - m_fin)
        l_fin = a_fin * l
        acc_f = jnp.concatenate([a_fin, a_fin], axis=0) * acc
        for t2 in range(s_len):
            pt = jnp.exp(s_new[t2] - m_fin)
            l_fin = l_fin + pt
            acc_f = acc_f + jnp.concatenate([pt, pt], axis=0) * vn_ref[0, t2]
        lam = _lambda_value(lq1_ref[...], lk1_ref[...], lq2_ref[...], lk2_ref[...], lam_init)
        halves = []
        for c2 in range(2):
            a = acc_f[c2 * nr:(c2 + 1) * nr] / l_fin
            halves.append(a[:half] - lam * a[half:])
        ms = (jnp.sum(halves[0] * halves[0], axis=1, keepdims=True)
              + jnp.sum(halves[1] * halves[1], axis=1, keepdims=True)) / (2.0 * hd)
        rs = lax.rsqrt(ms + RMS_EPS) * (1.0 - lam_init)
        for c2 in range(2):
            o_ref[0, :, c2 * hd:(c2 + 1) * hd] = halves[c2] * rs * g_ref[:, c2 * hd:(c2 + 1) * hd]


def _attn_sample(page_table, q, kn, vn, ck, cv, lam_params, sub_g, *, n_heads, hd, s_len, page,
                 gp, lam_init):
    bsz, nr, _ = q.shape
    n_pages = page_table.shape[1]
    assert n_pages % gp == 0 and hd == LANES and nr == 2 * n_heads * s_len
    ncol = ck.shape[1]
    pg = lambda gi: pl.BlockSpec((1, ncol, hd), lambda b, p, pt: (pt[b, p * gp + gi], 0, 0))
    vec = lambda n: pl.BlockSpec((1, n), lambda b, p, pt: (0, 0))
    grid_spec = pltpu.PrefetchScalarGridSpec(
        num_scalar_prefetch=1,
        grid=(bsz, n_pages // gp),
        in_specs=[pl.BlockSpec((1, nr, hd), lambda b, p, pt: (b, 0, 0)),
                  pl.BlockSpec((1, s_len, nr, hd), lambda b, p, pt: (b, 0, 0, 0)),
                  pl.BlockSpec((1, s_len, 2 * nr, hd), lambda b, p, pt: (b, 0, 0, 0))]
                 + [pg(gi) for gi in range(gp)] + [pg(gi) for gi in range(gp)]
                 + [vec(hd), vec(hd), vec(hd), vec(hd), vec(2 * hd)],
        out_specs=pl.BlockSpec((1, nr // 2, 2 * hd), lambda b, p, pt: (b, 0, 0)),
        scratch_shapes=[pltpu.VMEM((nr, hd), BF16),
                        pltpu.VMEM((nr, ncol), F32),
                        pltpu.VMEM((nr, 1), F32),
                        pltpu.VMEM((nr, 1), F32),
                        pltpu.VMEM((2 * nr, hd), F32)],
    )
    kern = functools.partial(_attn_sample_kernel, n_heads=n_heads, hd=hd, s_len=s_len,
                             past_len=n_pages * page, page=page, gp=gp, lam_init=lam_init)
    return pl.pallas_call(
        kern, grid_spec=grid_spec,
        out_shape=jax.ShapeDtypeStruct((bsz, nr // 2, 2 * hd), F32),
        compiler_params=_params(("arbitrary", "arbitrary")),
        name="attn_sample",
    )(page_table, q, kn, vn, *([ck] * gp), *([cv] * gp), *lam_params, sub_g)


def _lru_gates(xc, gaw_ref, gab_ref, gxw_ref, gxb_ref, lam_ref):
    n_blk, bs, _ = gaw_ref.shape
    xcb = xc.astype(BF16)
    ra = jnp.concatenate([_dot(xcb[:, n * bs:(n + 1) * bs], gaw_ref[n]) for n in range(n_blk)], axis=1)
    rx = jnp.concatenate([_dot(xcb[:, n * bs:(n + 1) * bs], gxw_ref[n]) for n in range(n_blk)], axis=1)
    r = _sigmoid(ra + gab_ref[...])
    ig = _sigmoid(rx + gxb_ref[...])
    neg_lam = -lam_ref[...]
    softplus = jnp.maximum(neg_lam, 0.0) + jnp.log1p(jnp.exp(-jnp.abs(neg_lam)))
    log_a = -LRU_C * r * softplus
    a = jnp.exp(log_a)
    mult = jnp.sqrt(-jnp.tanh(log_a) * (a * a + 1.0))
    return a, ig, mult


def _lru_prompt_kernel(xb_ref, gb_ref, cw_ref, cb_ref, gaw_ref, gab_ref, gxw_ref, gxb_ref, lam_ref,
                       yb_ref, hl_ref, xbuf, hbuf, hcar, *, tm, conv_w, last_tile, last_row):
    i = pl.program_id(0)
    w = xb_ref.shape[1]
    n_grp = tm // 8

    @pl.when(i == 0)
    def _():
        xbuf[0:8, :] = jnp.zeros((8, w), F32)
        hcar[...] = jnp.zeros_like(hcar)

    xbuf[8:8 + tm, :] = xb_ref[...]
    xc = cb_ref[...] + cw_ref[conv_w - 1:conv_w, :] * xb_ref[...]
    for jj in range(conv_w - 1):
        start = 8 - (conv_w - 1) + jj
        xc = xc + cw_ref[jj:jj + 1, :] * xbuf[start:start + tm, :]
    xbuf[0:8, :] = xbuf[tm:tm + 8, :]

    a, ig, mult = _lru_gates(xc, gaw_ref, gab_ref, gxw_ref, gxb_ref, lam_ref)
    grow = i * tm + lax.broadcasted_iota(jnp.int32, (tm, 1), 0)
    mult = jnp.where(grow == 0, 1.0, mult)
    u = xc * ig * mult

    a3 = a.reshape(n_grp, 8, w)
    u3 = u.reshape(n_grp, 8, w)
    sub = lax.broadcasted_iota(jnp.int32, (n_grp, 8, w), 1)
    for sh in (1, 2, 4):
        a_prev = pltpu.roll(a3, sh, axis=1)
        u_prev = pltpu.roll(u3, sh, axis=1)
        ok = sub >= sh
        u3 = jnp.where(ok, a3 * u_prev + u3, u3)
        a3 = jnp.where(ok, a3 * a_prev, a3)
    h_prev = hcar[...]
    for gi in range(n_grp):
        hg = a3[gi] * h_prev + u3[gi]
        hbuf[8 * gi:8 * (gi + 1), :] = hg
        h_prev = hg[7:8, :]
    hcar[...] = h_prev

    yb_ref[...] = (hbuf[...] * _gelu_tanh(gb_ref[...])).astype(yb_ref.dtype)

    @pl.when(i == last_tile)
    def _():
        hl_ref[...] = hbuf[last_row:last_row + 1, :]


def _lru_prompt(z, conv_w, conv_b, gaw, gab, gxw, gxb, lru_lambda, *, t_len, tm):
    r = z.shape[0]
    w = conv_b.shape[1]
    cw = conv_w.shape[0]
    n_blk, bs, _ = gaw.shape
    full = lambda shape: pl.BlockSpec(shape, lambda i: (0,) * len(shape))
    kern = functools.partial(_lru_prompt_kernel, tm=tm, conv_w=cw,
                             last_tile=(t_len - 1) // tm, last_row=(t_len - 1) % tm)
    return pl.pallas_call(
        kern,
        grid=(r // tm,),
        in_specs=[pl.BlockSpec((tm, w), lambda i: (i, 3)),
                  pl.BlockSpec((tm, w), lambda i: (i, 4)),
                  full((cw, w)), full((1, w)),
                  full((n_blk, bs, bs)), full((1, w)),
                  full((n_blk, bs, bs)), full((1, w)), full((1, w))],
        out_specs=[pl.BlockSpec((tm, w), lambda i: (i, 0)), full((1, w))],
        out_shape=[jax.ShapeDtypeStruct((r, w), BF16), jax.ShapeDtypeStruct((1, w), F32)],
        scratch_shapes=[pltpu.VMEM((tm + 8, w), F32), pltpu.VMEM((tm, w), F32),
                        pltpu.VMEM((1, w), F32)],
        compiler_params=_params(("arbitrary",)),
        name="lru_prompt",
    )(z, z, conv_w, conv_b, gaw, gab, gxw, gxb, lru_lambda)


def _lru_sample_kernel(xb_ref, gb_ref, cs_ref, h0_ref, cw_ref, cb_ref, gaw_ref, gab_ref, gxw_ref,
                       gxb_ref, lam_ref, yb_ref, hl_ref, *, conv_w):
    s_len, bsz, w = xb_ref.shape
    xp = [cs_ref[jj] for jj in range(conv_w - 1)] + [xb_ref[tt] for tt in range(s_len)]
    xcs = []
    for tt in range(s_len):
        xc = cb_ref[...] + cw_ref[0:1, :] * xp[tt]
        for jj in range(1, conv_w):
            xc = xc + cw_ref[jj:jj + 1, :] * xp[tt + jj]
        xcs.append(xc)
    xc = jnp.concatenate(xcs, axis=0)
    a, ig, mult = _lru_gates(xc, gaw_ref, gab_ref, gxw_ref, gxb_ref, lam_ref)
    u = xc * ig * mult
    h = h0_ref[...]
    for tt in range(s_len):
        rows = slice(tt * bsz, (tt + 1) * bsz)
        h = a[rows, :] * h + u[rows, :]
        yb_ref[tt] = (h * _gelu_tanh(gb_ref[tt])).astype(yb_ref.dtype)
    hl_ref[...] = h


def _lru_sample(xb, gb, conv_state, h0, conv_w, conv_b, gaw, gab, gxw, gxb, lru_lambda):
    s_len, bsz, w = xb.shape
    kern = functools.partial(_lru_sample_kernel, conv_w=conv_w.shape[0])
    return pl.pallas_call(
        kern,
        out_shape=[jax.ShapeDtypeStruct((s_len, bsz, w), BF16), jax.ShapeDtypeStruct((bsz, w), F32)],
        compiler_params=pltpu.CompilerParams(vmem_limit_bytes=VMEM_MB * 2**20),
        name="lru_sample",
    )(xb, gb, conv_state, h0, conv_w, conv_b, gaw, gab, gxw, gxb, lru_lambda)


def _out_proj_kernel(o_ref, yb_ref, x_ref, w_ref, g_ref, b_ref, out_ref, *, alpha):
    n_heads, _, hw = o_ref.shape
    aw = n_heads * hw
    mixed = _dot(yb_ref[...], w_ref[aw:, :])
    for h in range(n_heads):
        mixed = mixed + _dot(o_ref[h], w_ref[hw * h:hw * (h + 1), :])
    out_ref[...] = _layer_norm(alpha * x_ref[...] + mixed, g_ref[...], b_ref[...])


def _out_proj(o, yb, x, w, g, b, *, alpha, tm):
    r, d = x.shape
    n_heads, _, hw = o.shape
    bw = yb.shape[1]
    full = lambda shape: pl.BlockSpec(shape, lambda i: (0,) * len(shape))
    return pl.pallas_call(
        functools.partial(_out_proj_kernel, alpha=alpha),
        grid=(r // tm,),
        in_specs=[pl.BlockSpec((n_heads, tm, hw), lambda i: (0, i, 0)),
                  pl.BlockSpec((tm, bw), lambda i: (i, 0)),
                  pl.BlockSpec((tm, d), lambda i: (i, 0)),
                  full((n_heads * hw + bw, d)), full((1, d)), full((1, d))],
        out_specs=pl.BlockSpec((tm, d), lambda i: (i, 0)),
        out_shape=jax.ShapeDtypeStruct((r, d), F32),
        compiler_params=_params(("parallel",)),
        name="out_proj",
    )(o, yb, x, w, g, b)


def _mlp_kernel(x_ref, w1_ref, w2_ref, g_ref, b_ref, out_ref, xb_sc, acc_sc, *, alpha):
    f = pl.program_id(1)

    @pl.when(f == 0)
    def _():
        xb_sc[...] = x_ref[...].astype(BF16)
        acc_sc[...] = jnp.zeros_like(acc_sc)

    h = jnp.maximum(_dot(xb_sc[...], w1_ref[...]), 0.0)
    acc_sc[...] += _dot((h * h).astype(BF16), w2_ref[...])

    @pl.when(f == pl.num_programs(1) - 1)
    def _():
        out_ref[...] = _layer_norm(alpha * x_ref[...] + acc_sc[...], g_ref[...], b_ref[...])


def _mlp(x, w1, w2, g, b, layer, *, alpha, tm, tf):
    r, d = x.shape
    ff = w1.shape[2]
    vec = pl.BlockSpec((None, 1, d), lambda i, f: (layer, 0, 0))
    return pl.pallas_call(
        functools.partial(_mlp_kernel, alpha=alpha),
        grid=(r // tm, ff // tf),
        in_specs=[pl.BlockSpec((tm, d), lambda i, f: (i, 0)),
                  pl.BlockSpec((None, d, tf), lambda i, f: (layer, 0, f)),
                  pl.BlockSpec((None, tf, d), lambda i, f: (layer, f, 0)),
                  vec, vec],
        out_specs=pl.BlockSpec((tm, d), lambda i, f: (i, 0)),
        out_shape=jax.ShapeDtypeStruct((r, d), F32),
        scratch_shapes=[pltpu.VMEM((tm, d), BF16), pltpu.VMEM((tm, d), F32)],
        compiler_params=_params(("parallel", "arbitrary")),
        name="mlp",
    )(x, w1, w2, g, b)


def _pool_project(ms, x, w_ref, sc_ref, g_ref, b_ref, alpha):
    y = jnp.concatenate([_dot(ms[gi].astype(BF16), w_ref[gi]) for gi in range(len(ms))], axis=1)
    return _layer_norm(alpha * x + y * sc_ref[...], g_ref[...], b_ref[...])


def _pool_prompt_kernel(x_ref, w_ref, sc_ref, g_ref, b_ref, out_ref, ebuf, *, tm, alpha, pad):
    i = pl.program_id(0)
    d = x_ref.shape[1]
    gw = d // len(POOL_WINDOWS)

    @pl.when(i == 0)
    def _():
        ebuf[0:pad, :] = jnp.zeros((pad, d), F32)

    x = x_ref[...]
    ebuf[pad:pad + tm, :] = x
    pos = i * tm + lax.broadcasted_iota(jnp.int32, (tm, 1), 0)
    ms = []
    for gi, win in enumerate(POOL_WINDOWS):
        cols = slice(gi * gw, (gi + 1) * gw)
        ws = x[:, cols]
        for kk in range(1, win):
            ws = ws + ebuf[pad - kk:pad - kk + tm, cols]
        cnt = jnp.minimum(win, pos + 1).astype(F32)
        ms.append(ws / cnt - x[:, cols])
    ebuf[0:pad, :] = ebuf[tm:tm + pad, :]
    out_ref[...] = _pool_project(ms, x, w_ref, sc_ref, g_ref, b_ref, alpha)


def _pool_prompt(x, w, sc, g, b, *, alpha, tm):
    r, d = x.shape
    n_g, gw, _ = w.shape
    pad = 16
    assert max(POOL_WINDOWS) - 1 <= pad
    full = lambda shape: pl.BlockSpec(shape, lambda i: (0,) * len(shape))
    return pl.pallas_call(
        functools.partial(_pool_prompt_kernel, tm=tm, alpha=alpha, pad=pad),
        grid=(r // tm,),
        in_specs=[pl.BlockSpec((tm, d), lambda i: (i, 0)),
                  full((n_g, gw, gw)), full((1, d)), full((1, d)), full((1, d))],
        out_specs=pl.BlockSpec((tm, d), lambda i: (i, 0)),
        out_shape=jax.ShapeDtypeStruct((r, d), F32),
        scratch_shapes=[pltpu.VMEM((tm + pad, d), F32)],
        compiler_params=_params(("arbitrary",)),
        name="pool_prompt",
    )(x, w, sc, g, b)


def _pool_sample_kernel(x_ref, st_ref, w_ref, sc_ref, g_ref, b_ref, out_ref, *, alpha):
    s_len, bsz, d = x_ref.shape
    n_buf = st_ref.shape[0]
    gw = d // len(POOL_WINDOWS)
    ext = [st_ref[jj] for jj in range(n_buf)] + [x_ref[tt] for tt in range(s_len)]
    x = jnp.concatenate(ext[n_buf:], axis=0)
    ms = []
    for gi, win in enumerate(POOL_WINDOWS):
        cols = slice(gi * gw, (gi + 1) * gw)
        rows = []
        for tt in range(s_len):
            ws = ext[n_buf + tt][:, cols]
            for kk in range(1, win):
                ws = ws + ext[n_buf + tt - kk][:, cols]
            rows.append(ws / float(win) - ext[n_buf + tt][:, cols])
        ms.append(jnp.concatenate(rows, axis=0))
    y = _pool_project(ms, x, w_ref, sc_ref, g_ref, b_ref, alpha)
    for tt in range(s_len):
        out_ref[tt] = y[tt * bsz:(tt + 1) * bsz, :]


def _pool_sample(x, state, w, sc, g, b, *, alpha):
    return pl.pallas_call(
        functools.partial(_pool_sample_kernel, alpha=alpha),
        out_shape=jax.ShapeDtypeStruct(x.shape, F32),
        compiler_params=pltpu.CompilerParams(vmem_limit_bytes=VMEM_MB * 2**20),
        name="pool_sample",
    )(x, state, w, sc, g, b)


def kernel(x_prompt, x_sample, cache_k, cache_v, state_conv, state_lru, state_pool, page_table, meta_tokens, w_in, lam_q1, lam_k1, lam_q2, lam_k2, sub_norm_g, conv_w, conv_b, gate_a_w, gate_a_b, gate_x_w, gate_x_b, lru_lambda, w_out_ab, pool_w, pool_scale, mix_ln_g, mix_ln_b, w_ff1, w_ff2, ff_ln_g, ff_ln_b):
    n_prompt, seq, d = x_prompt.shape
    bsz, s_len, _ = x_sample.shape
    depth = w_ff1.shape[0]
    assert n_prompt == 1 and depth == 2
    n_meta = meta_tokens.shape[0]
    n_heads, hd2 = cache_k.shape[-2:]
    hd = hd2 // 2
    aw = n_heads * hd2
    bw = state_lru.shape[-1]
    assert aw == bw and w_in.shape[-1] == 3 * aw + 2 * bw
    page = cache_k.shape[2]
    n_buf = state_pool.shape[2]
    cw = conv_w.shape[1]
    assert s_len >= cw - 1
    alpha = (2.0 * depth) ** 0.25
    lam_init = 0.8 - 0.6 * math.exp(-0.3 * 0)

    t_len = n_meta + seq
    n_s = bsz * s_len
    rows = -(-(t_len + n_s) // ROW_ALIGN) * ROW_ALIGN
    xs_tm = x_sample.transpose(1, 0, 2).reshape(n_s, d)
    x0 = jnp.concatenate([meta_tokens, x_prompt[0], xs_tm,
                          jnp.zeros((rows - t_len - n_s, d), F32)], axis=0)

    row2 = lambda v: v.reshape(1, -1)
    lam_params = (row2(lam_q1[0]), row2(lam_k1[0]), row2(lam_q2[0]), row2(lam_k2[0]))
    sub_g = row2(sub_norm_g[0])

    z, qa, ka, vb = _in_proj(x0, w_in[0].astype(BF16), tm=T_ATT, tn=TN_PROJ, n_heads=n_heads, hd=hd)
    zs = z[t_len:t_len + n_s]

    o = _attn_prompt(qa, ka, vb, lam_params, sub_g, t=T_ATT, hd=hd, lam_init=lam_init)

    split = lambda c: zs[:, c * aw:(c + 1) * aw].reshape(s_len, bsz, n_heads, 2, hd)
    nr = 2 * n_heads * s_len
    q_s = split(0).transpose(1, 3, 2, 0, 4).reshape(bsz, nr, hd)
    kn = split(1).transpose(1, 0, 3, 2, 4)[:, :, :, :, None, :]
    kn = jnp.broadcast_to(kn, (bsz, s_len, 2, n_heads, s_len, hd)).reshape(bsz, s_len, nr, hd)
    vn = split(2).transpose(1, 0, 3, 2, 4)[:, :, :, None, :, None, :]
    vn = jnp.broadcast_to(vn, (bsz, s_len, 2, 2, n_heads, s_len, hd)).reshape(bsz, s_len, 2 * nr, hd)
    page_rows = lambda c: (c[0].reshape(-1, page, n_heads, 2, hd).transpose(0, 1, 3, 2, 4)
                           .reshape(-1, page * 2 * n_heads, hd))
    o_s = _attn_sample(page_table, q_s, kn, vn, page_rows(cache_k), page_rows(cache_v), lam_params,
                       sub_g, n_heads=n_heads, hd=hd, s_len=s_len, page=page,
                       gp=min(PAGES_PER_STEP, page_table.shape[1]), lam_init=lam_init)
    o_s = o_s.reshape(bsz, n_heads, s_len, hd2).transpose(1, 2, 0, 3).reshape(n_heads, n_s, hd2)
    o = lax.dynamic_update_slice(o, o_s.astype(BF16), (0, t_len, 0))

    gaw, gxw = gate_a_w[0].astype(BF16), gate_x_w[0].astype(BF16)
    lru_args = (conv_w[0], row2(conv_b[0]), gaw, row2(gate_a_b[0]), gxw, row2(gate_x_b[0]),
                row2(lru_lambda[0]))
    yb, h_last_p = _lru_prompt(z, *lru_args, t_len=t_len, tm=TM_LRU)
    xb_s = zs[:, 3 * aw:3 * aw + bw].reshape(s_len, bsz, bw)
    gb_s = zs[:, 3 * aw + bw:].reshape(s_len, bsz, bw)
    yb_s, h_last_s = _lru_sample(xb_s, gb_s, state_conv[0].transpose(1, 0, 2), state_lru[0], *lru_args)
    yb = lax.dynamic_update_slice(yb, yb_s.reshape(n_s, bw), (t_len, 0))

    x1 = _out_proj(o, yb, x0, w_out_ab[0].astype(BF16), row2(mix_ln_g[0]), row2(mix_ln_b[0]),
                   alpha=alpha, tm=TM_OUT)
    w1b, w2b = w_ff1.astype(BF16), w_ff2.astype(BF16)
    ff_g, ff_b = ff_ln_g[:, None, :], ff_ln_b[:, None, :]
    mlp = functools.partial(_mlp, alpha=alpha, tm=TM_MLP, tf=TF_MLP)
    x2 = mlp(x1, w1b, w2b, ff_g, ff_b, 0)

    pool_args = (pool_w[0].astype(BF16), row2(pool_scale[0]), row2(mix_ln_g[1]), row2(mix_ln_b[1]))
    x3 = _pool_prompt(x2, *pool_args, alpha=alpha, tm=TM_POOL)
    x2_s = x2[t_len:t_len + n_s].reshape(s_len, bsz, d)
    x3_s = _pool_sample(x2_s, state_pool[0].transpose(1, 0, 2), *pool_args, alpha=alpha)
    x3 = lax.dynamic_update_slice(x3, x3_s.reshape(n_s, d), (t_len, 0))
    x4 = mlp(x3, w1b, w2b, ff_g, ff_b, 1)

    to_bt = lambda v: v.reshape(s_len, bsz, -1).transpose(1, 0, 2)
    kv_p = lambda c: z[:t_len, c * aw:(c + 1) * aw].reshape(1, 1, t_len, n_heads, hd2)
    kv_s = lambda c: to_bt(zs[:, c * aw:(c + 1) * aw]).reshape(1, bsz, s_len, n_heads, hd2)
    y_prompt = x4[n_meta:t_len][None]
    y_sample = to_bt(x4[t_len:t_len + n_s])
    new_conv_prompt = z[t_len - (cw - 1):t_len, 3 * aw:3 * aw + bw][None, None]
    new_lru_prompt = h_last_p[None]
    new_pool_prompt = x2[t_len - n_buf:t_len][None, None]
    new_conv_sample = xb_s[s_len - (cw - 1):].transpose(1, 0, 2)[None]
    new_lru_sample = h_last_s[None]
    new_pool_sample = jnp.concatenate([state_pool[0], x2_s.transpose(1, 0, 2)], axis=1)[:, -n_buf:][None]
    return (y_prompt, y_sample, kv_p(1), kv_p(2), new_conv_prompt, new_lru_prompt, new_pool_prompt,
            kv_s(1), kv_s(2), new_conv_sample, new_lru_sample, new_pool_sample)
```

```python
import functools
import math

import jax
import jax.numpy as jnp
from jax import lax
from jax.experimental import pallas as pl
from jax.experimental.pallas import tpu as pltpu

F32 = jnp.float32
BF16 = jnp.bfloat16

LN_EPS = 1e-5
RMS_EPS = 1e-5
LRU_C = 8.0
POOL_WINDOWS = (2, 4, 8, 16)
NEG = -1e30
LOG2E = 1.4426950408889634
LANES = 128
N_AUG = 3

ROW_ALIGN = 768
T_ATT = 768
TN_PROJ = 1024
TM_LRU = 256
TM_OUT = 384
TM_MLP = 768
TF_MLP = 512
TM_POOL = 256
ATT_HEAD_UNROLL = 2
PAGES_PER_STEP = 16
VMEM_MB = 56


def _params(sem, mb=VMEM_MB):
    return pltpu.CompilerParams(dimension_semantics=sem, vmem_limit_bytes=mb * 2**20)


def _layer_norm(y, g, b):
    mu = jnp.mean(y, axis=-1, keepdims=True)
    d = y - mu
    var = jnp.mean(d * d, axis=-1, keepdims=True)
    return d * lax.rsqrt(var + LN_EPS) * g + b


def _dot(a, b):
    return jnp.dot(a, b, preferred_element_type=F32)


def _dot_nt(a, b):
    return lax.dot_general(a, b, (((1,), (1,)), ((), ())), preferred_element_type=F32)


def _sigmoid(x):
    return 1.0 / (1.0 + jnp.exp(-x))


def _gelu_tanh(x):
    c = math.sqrt(2.0 / math.pi)
    return 0.5 * x * (1.0 + jnp.tanh(c * (x + 0.044715 * (x * x * x))))


def _lambda_value(lq1, lk1, lq2, lk2, lam_init):
    s1 = jnp.sum(lq1 * lk1, axis=-1, keepdims=True)
    s2 = jnp.sum(lq2 * lk2, axis=-1, keepdims=True)
    return jnp.exp(s1) - jnp.exp(s2) + lam_init


def _div_pow2(x, n):
    assert n & (n - 1) == 0
    return lax.shift_right_logical(x, n.bit_length() - 1)


def _mod_pow2(x, n):
    assert n & (n - 1) == 0
    return lax.bitwise_and(x, n - 1)


def _head_slope(h, n_heads):
    return 2.0 ** (-8.0 * (h + 1) / n_heads)


def _in_proj_kernel(x_ref, w_ref, z_ref, qk_ref, vb_ref, kr_ref, vr_ref, *, n_heads, hd):
    j = pl.program_id(1)
    z = _dot(x_ref[...].astype(BF16), w_ref[...])
    z_ref[...] = z
    tm = z.shape[0]
    lane = lax.broadcasted_iota(jnp.int32, (tm, hd), 1)

    def cache_rows(dst_ref):
        for c in range(2):
            for h in range(n_heads):
                col = (2 * h + c) * hd
                dst_ref[pl.ds(c * n_heads + h, tm, stride=2 * n_heads), :] = z[:, col:col + hd]

    @pl.when(j == 0)
    def _():
        ones = jnp.where(lane < N_AUG, 1.0, 0.0).astype(BF16)
        q_scale = hd ** -0.5 * LOG2E
        for h in range(n_heads):
            for c in range(2):
                col = (2 * h + c) * hd
                qk_ref[0, h, :, 2 * hd * c:2 * hd * c + hd] = (z[:, col:col + hd] * q_scale).astype(BF16)
                qk_ref[0, h, :, 2 * hd * c + hd:2 * hd * (c + 1)] = ones

    @pl.when(j == 1)
    def _():
        row = lax.broadcasted_iota(jnp.int32, (tm, hd), 0).astype(F32)
        for h in range(n_heads):
            a = row * (_head_slope(h, n_heads) * LOG2E)
            hi = a.astype(BF16).astype(F32)
            mid = (a - hi).astype(BF16).astype(F32)
            lo = a - hi - mid
            aug = jnp.where(lane == 0, hi, jnp.where(lane == 1, mid, jnp.where(lane == 2, lo, 0.0)))
            aug = aug.astype(BF16)
            for c in range(2):
                col = (2 * h + c) * hd
                qk_ref[0, h, :, 2 * hd * c:2 * hd * c + hd] = z[:, col:col + hd].astype(BF16)
                qk_ref[0, h, :, 2 * hd * c + hd:2 * hd * (c + 1)] = aug
        cache_rows(kr_ref)

    @pl.when(j == 2)
    def _():
        for h in range(n_heads):
            vb_ref[h] = z[:, 2 * hd * h:2 * hd * (h + 1)].astype(BF16)
        cache_rows(vr_ref)


def _in_proj(x, w, *, t_len, tm, tn, n_heads, hd):
    r, d = x.shape
    n = w.shape[1]
    assert hd == LANES and tn == 2 * hd * n_heads
    kr = 2 * n_heads
    cache_blk = pl.BlockSpec((tm * kr, hd), lambda i, j: (i, 0))
    cache_shape = jax.ShapeDtypeStruct((t_len * kr, hd), F32)
    return pl.pallas_call(
        functools.partial(_in_proj_kernel, n_heads=n_heads, hd=hd),
        grid=(r // tm, n // tn),
        in_specs=[pl.BlockSpec((tm, d), lambda i, j: (i, 0)),
                  pl.BlockSpec((d, tn), lambda i, j: (0, j))],
        out_specs=[pl.BlockSpec((tm, tn), lambda i, j: (i, j)),
                   pl.BlockSpec((1, n_heads, tm, 4 * hd), lambda i, j: (jnp.minimum(j, 1), 0, i, 0)),
                   pl.BlockSpec((n_heads, tm, 2 * hd), lambda i, j: (0, i, 0)),
                   cache_blk, cache_blk],
        out_shape=[jax.ShapeDtypeStruct((r, n), F32),
                   jax.ShapeDtypeStruct((2, n_heads, r, 4 * hd), BF16),
                   jax.ShapeDtypeStruct((n_heads, r, 2 * hd), BF16),
                   cache_shape, cache_shape],
        compiler_params=_params(("arbitrary", "arbitrary")),
        name="in_proj",
    )(x, w)


def _attn_prompt_kernel(qi_ref, kj_ref, qa_ref, ka_ref, vb_ref, sl_ref, lq1_ref, lk1_ref, lq2_ref,
                        lk2_ref, g_ref, o_ref, m_sc, l_sc, acc_sc, *, t, n_heads, hd, lam_init):
    p = pl.program_id(0)
    i = qi_ref[p]
    j = kj_ref[p]

    @pl.when(j == 0)
    def _():
        m_sc[...] = jnp.full_like(m_sc, NEG)
        l_sc[...] = jnp.zeros_like(l_sc)
        acc_sc[...] = jnp.zeros_like(acc_sc)

    def tile(masked):
        if masked:
            keep = (lax.broadcasted_iota(jnp.int32, (t, t), 1)
                    <= lax.broadcasted_iota(jnp.int32, (t, t), 0))
        tile_dist = ((j - i) * t).astype(F32)

        def head(h, carry):
            shift = sl_ref[h][:, 0:1] * tile_dist
            for c in range(2):
                s = _dot_nt(qa_ref[h, :, 2 * hd * c:2 * hd * (c + 1)],
                            ka_ref[h, :, 2 * hd * c:2 * hd * (c + 1)])
                if masked:
                    s = jnp.where(keep, s, NEG)
                idx = 2 * h + c
                m_old = m_sc[idx]
                m_new = jnp.maximum(m_old, jnp.max(s, axis=1, keepdims=True) + shift)
                alpha = jnp.exp2(m_old - m_new)
                pr = jnp.exp2(s - (m_new - shift))
                l_sc[idx] = alpha * l_sc[idx] + jnp.sum(pr, axis=1, keepdims=True)
                acc_sc[idx] = alpha * acc_sc[idx] + _dot(pr.astype(BF16), vb_ref[h])
                m_sc[idx] = m_new
            return carry

        lax.fori_loop(0, n_heads, head, 0, unroll=ATT_HEAD_UNROLL)

    @pl.when(j < i)
    def _():
        tile(False)

    @pl.when(j == i)
    def _():
        tile(True)
        lam = _lambda_value(lq1_ref[...], lk1_ref[...], lq2_ref[...], lk2_ref[...], lam_init)
        g = g_ref[...]
        for h in range(n_heads):
            o1 = acc_sc[2 * h] / l_sc[2 * h]
            o2 = acc_sc[2 * h + 1] / l_sc[2 * h + 1]
            oh = o1 - lam * o2
            oh = oh * lax.rsqrt(jnp.mean(oh * oh, axis=-1, keepdims=True) + RMS_EPS)
            o_ref[h] = (oh * g * (1.0 - lam_init)).astype(o_ref.dtype)


def _attn_prompt(qk, vb, lam_params, sub_g, *, t, hd, lam_init):
    _, n_heads, r, _ = qk.shape
    nq = r // t
    qi = jnp.asarray([i for i in range(nq) for _ in range(i + 1)], jnp.int32)
    kj = jnp.asarray([j for i in range(nq) for j in range(i + 1)], jnp.int32)
    slopes = jnp.asarray([[[_head_slope(h, n_heads) * LOG2E] * LANES] for h in range(n_heads)], F32)
    vec = lambda n: pl.BlockSpec((1, n), lambda p, qi, kj: (0, 0))
    grid_spec = pltpu.PrefetchScalarGridSpec(
        num_scalar_prefetch=2,
        grid=(int(qi.shape[0]),),
        in_specs=[pl.BlockSpec((None, n_heads, t, 4 * hd), lambda p, qi, kj: (0, 0, qi[p], 0)),
                  pl.BlockSpec((None, n_heads, t, 4 * hd), lambda p, qi, kj: (1, 0, kj[p], 0)),
                  pl.BlockSpec((n_heads, t, 2 * hd), lambda p, qi, kj: (0, kj[p], 0)),
                  pl.BlockSpec((n_heads, 1, LANES), lambda p, qi, kj: (0, 0, 0)),
                  vec(hd), vec(hd), vec(hd), vec(hd), vec(2 * hd)],
        out_specs=pl.BlockSpec((n_heads, t, 2 * hd), lambda p, qi, kj: (0, qi[p], 0)),
        scratch_shapes=[pltpu.VMEM((2 * n_heads, t, 1), F32),
                        pltpu.VMEM((2 * n_heads, t, 1), F32),
                        pltpu.VMEM((2 * n_heads, t, 2 * hd), F32)],
    )
    kern = functools.partial(_attn_prompt_kernel, t=t, n_heads=n_heads, hd=hd, lam_init=lam_init)
    return pl.pallas_call(
        kern, grid_spec=grid_spec,
        out_shape=jax.ShapeDtypeStruct((n_heads, r, 2 * hd), BF16),
        compiler_params=_params(("arbitrary",)),
        name="attn_prompt",
    )(qi, kj, qk, qk, vb, slopes, *lam_params, sub_g)


def _attn_sample_kernel(pt_ref, q_ref, kn_ref, vn_ref, *rest, n_heads, hd, s_len, past_len, page,
                        gp, lam_init):
    del pt_ref
    page_refs = rest[:2 * gp]
    (lq1_ref, lk1_ref, lq2_ref, lk2_ref, g_ref, o_ref,
     qb_sc, bias_sc, m_sc, l_sc, acc_sc) = rest[2 * gp:]
    p = pl.program_id(1)
    nr = 2 * n_heads * s_len
    kr = 2 * n_heads
    ncol = page * kr
    half = nr // 2
    scale = hd ** -0.5

    def row_info(shape):
        row = lax.broadcasted_iota(jnp.int32, shape, 0)
        tok = _mod_pow2(row, s_len)
        head = _mod_pow2(_div_pow2(row, s_len), n_heads)
        cmap = _div_pow2(row, s_len * n_heads)
        slope = jnp.zeros(shape, F32)
        for h in range(n_heads):
            slope = jnp.where(head == h, _head_slope(h, n_heads), slope)
        return tok, head, cmap, slope

    @pl.when(p == 0)
    def _():
        qb_sc[...] = (q_ref[0] * scale).astype(BF16)
        tok, head, cmap, slope = row_info((nr, ncol))
        col = lax.broadcasted_iota(jnp.int32, (nr, ncol), 1)
        key = _div_pow2(col, kr)
        own = _mod_pow2(col, kr) == cmap * n_heads + head
        bias_sc[...] = jnp.where(own, -slope * (past_len + tok - key).astype(F32), NEG)
        m_sc[...] = jnp.full_like(m_sc, NEG)
        l_sc[...] = jnp.zeros_like(l_sc)
        acc_sc[...] = jnp.zeros_like(acc_sc)

    tok1, _, _, slope1 = row_info((nr, 1))
    m = m_sc[...]
    l = l_sc[...]
    acc = acc_sc[...]
    scores, shifts = [], []
    m_new = m
    for gi in range(gp):
        kp = page_refs[gi][0].astype(BF16)
        s = _dot_nt(qb_sc[...], kp) + bias_sc[...]
        shift = slope1 * ((p * gp + gi) * page).astype(F32)
        m_new = jnp.maximum(m_new, jnp.max(s, axis=1, keepdims=True) + shift)
        scores.append(s)
        shifts.append(shift)
    alpha = jnp.exp(m - m_new)
    l = alpha * l
    pv = jnp.zeros((2 * nr, hd), F32)
    for gi in range(gp):
        pr = jnp.exp(scores[gi] - (m_new - shifts[gi]))
        l = l + jnp.sum(pr, axis=1, keepdims=True)
        top, bot = pr[:half], pr[half:]
        chunks = range(ncol // LANES)
        top_r = jnp.concatenate(
            [pltpu.roll(top[:, LANES * cc:LANES * (cc + 1)], n_heads, 1) for cc in chunks], axis=1)
        bot_r = jnp.concatenate(
            [pltpu.roll(bot[:, LANES * cc:LANES * (cc + 1)], LANES - n_heads, 1) for cc in chunks], axis=1)
        lhs = jnp.concatenate([top, bot_r, top_r, bot], axis=0).astype(BF16)
        vp = page_refs[gp + gi][0].astype(BF16)
        pv = pv + _dot(lhs, vp)
    acc = jnp.concatenate([alpha, alpha], axis=0) * acc + pv
    m = m_new
    m_sc[...] = m
    l_sc[...] = l
    acc_sc[...] = acc

    @pl.when(p == pl.num_programs(1) - 1)
    def _():
        qf = q_ref[0] * scale
        s_new = []
        for t2 in range(s_len):
            st = jnp.sum(qf * kn_ref[0, t2], axis=1, keepdims=True)
            st = st - slope1 * (tok1 - t2).astype(F32)
            s_new.append(jnp.where(t2 <= tok1, st, NEG))
        m_fin = m
        for st in s_new:
            m_fin = jnp.maximum(m_fin, st)
        a_fin = jnp.exp(m - m_fin)
        l_fin = a_fin * l
        acc_f = jnp.concatenate([a_fin, a_fin], axis=0) * acc
        for t2 in range(s_len):
            pt = jnp.exp(s_new[t2] - m_fin)
            l_fin = l_fin + pt
            acc_f = acc_f + jnp.concatenate([pt, pt], axis=0) * vn_ref[0, t2]
        lam = _lambda_value(lq1_ref[...], lk1_ref[...], lq2_ref[...], lk2_ref[...], lam_init)
        halves = []
        for c2 in range(2):
            a = acc_f[c2 * nr:(c2 + 1) * nr] / l_fin
            halves.append(a[:half] - lam * a[half:])
        ms = (jnp.sum(halves[0] * halves[0], axis=1, keepdims=True)
              + jnp.sum(halves[1] * halves[1], axis=1, keepdims=True)) / (2.0 * hd)
        rs = lax.rsqrt(ms + RMS_EPS) * (1.0 - lam_init)
        for c2 in range(2):
            o_ref[0, :, c2 * hd:(c2 + 1) * hd] = halves[c2] * rs * g_ref[:, c2 * hd:(c2 + 1) * hd]


def _attn_sample(page_table, q, kn, vn, ck, cv, lam_params, sub_g, *, n_heads, hd, s_len, page,
                 gp, lam_init):
    bsz, nr, _ = q.shape
    n_pages = page_table.shape[1]
    assert n_pages % gp == 0 and hd == LANES and nr == 2 * n_heads * s_len
    ncol = ck.shape[1]
    pg = lambda gi: pl.BlockSpec((1, ncol, hd), lambda b, p, pt: (pt[b, p * gp + gi], 0, 0))
    vec = lambda n: pl.BlockSpec((1, n), lambda b, p, pt: (0, 0))
    grid_spec = pltpu.PrefetchScalarGridSpec(
        num_scalar_prefetch=1,
        grid=(bsz, n_pages // gp),
        in_specs=[pl.BlockSpec((1, nr, hd), lambda b, p, pt: (b, 0, 0)),
                  pl.BlockSpec((1, s_len, nr, hd), lambda b, p, pt: (b, 0, 0, 0)),
                  pl.BlockSpec((1, s_len, 2 * nr, hd), lambda b, p, pt: (b, 0, 0, 0))]
                 + [pg(gi) for gi in range(gp)] + [pg(gi) for gi in range(gp)]
                 + [vec(hd), vec(hd), vec(hd), vec(hd), vec(2 * hd)],
        out_specs=pl.BlockSpec((1, nr // 2, 2 * hd), lambda b, p, pt: (b, 0, 0)),
        scratch_shapes=[pltpu.VMEM((nr, hd), BF16),
                        pltpu.VMEM((nr, ncol), F32),
                        pltpu.VMEM((nr, 1), F32),
                        pltpu.VMEM((nr, 1), F32),
                        pltpu.VMEM((2 * nr, hd), F32)],
    )
    kern = functools.partial(_attn_sample_kernel, n_heads=n_heads, hd=hd, s_len=s_len,
                             past_len=n_pages * page, page=page, gp=gp, lam_init=lam_init)
    return pl.pallas_call(
        kern, grid_spec=grid_spec,
        out_shape=jax.ShapeDtypeStruct((bsz, nr // 2, 2 * hd), F32),
        compiler_params=_params(("arbitrary", "arbitrary")),
        name="attn_sample",
    )(page_table, q, kn, vn, *([ck] * gp), *([cv] * gp), *lam_params, sub_g)


def _lru_gates(xc, gaw_ref, gab_ref, gxw_ref, gxb_ref, lam_ref):
    n_blk, bs, _ = gaw_ref.shape
    xcb = xc.astype(BF16)
    ra = jnp.concatenate([_dot(xcb[:, n * bs:(n + 1) * bs], gaw_ref[n]) for n in range(n_blk)], axis=1)
    rx = jnp.concatenate([_dot(xcb[:, n * bs:(n + 1) * bs], gxw_ref[n]) for n in range(n_blk)], axis=1)
    r = _sigmoid(ra + gab_ref[...])
    ig = _sigmoid(rx + gxb_ref[...])
    neg_lam = -lam_ref[...]
    softplus = jnp.maximum(neg_lam, 0.0) + jnp.log1p(jnp.exp(-jnp.abs(neg_lam)))
    log_a = -LRU_C * r * softplus
    a = jnp.exp(log_a)
    mult = jnp.sqrt(-jnp.tanh(log_a) * (a * a + 1.0))
    return a, ig, mult


def _lru_prompt_kernel(xb_ref, gb_ref, cw_ref, cb_ref, gaw_ref, gab_ref, gxw_ref, gxb_ref, lam_ref,
                       yb_ref, hl_ref, xbuf, hbuf, hcar, *, tm, conv_w, last_tile, last_row):
    i = pl.program_id(0)
    w = xb_ref.shape[1]
    n_grp = tm // 8

    @pl.when(i == 0)
    def _():
        xbuf[0:8, :] = jnp.zeros((8, w), F32)
        hcar[...] = jnp.zeros_like(hcar)

    xbuf[8:8 + tm, :] = xb_ref[...]
    xc = cb_ref[...] + cw_ref[conv_w - 1:conv_w, :] * xb_ref[...]
    for jj in range(conv_w - 1):
        start = 8 - (conv_w - 1) + jj
        xc = xc + cw_ref[jj:jj + 1, :] * xbuf[start:start + tm, :]
    xbuf[0:8, :] = xbuf[tm:tm + 8, :]

    a, ig, mult = _lru_gates(xc, gaw_ref, gab_ref, gxw_ref, gxb_ref, lam_ref)
    grow = i * tm + lax.broadcasted_iota(jnp.int32, (tm, 1), 0)
    mult = jnp.where(grow == 0, 1.0, mult)
    u = xc * ig * mult

    a3 = a.reshape(n_grp, 8, w)
    u3 = u.reshape(n_grp, 8, w)
    sub = lax.broadcasted_iota(jnp.int32, (n_grp, 8, w), 1)
    for sh in (1, 2, 4):
        a_prev = pltpu.roll(a3, sh, axis=1)
        u_prev = pltpu.roll(u3, sh, axis=1)
        ok = sub >= sh
        u3 = jnp.where(ok, a3 * u_prev + u3, u3)
        a3 = jnp.where(ok, a3 * a_prev, a3)
    h_prev = hcar[...]
    for gi in range(n_grp):
        hg = a3[gi] * h_prev + u3[gi]
        hbuf[8 * gi:8 * (gi + 1), :] = hg
        h_prev = hg[7:8, :]
    hcar[...] = h_prev

    yb_ref[...] = (hbuf[...] * _gelu_tanh(gb_ref[...])).astype(yb_ref.dtype)

    @pl.when(i == last_tile)
    def _():
        hl_ref[...] = hbuf[last_row:last_row + 1, :]


def _lru_prompt(z, conv_w, conv_b, gaw, gab, gxw, gxb, lru_lambda, *, t_len, tm):
    r = z.shape[0]
    w = conv_b.shape[1]
    cw = conv_w.shape[0]
    n_blk, bs, _ = gaw.shape
    full = lambda shape: pl.BlockSpec(shape, lambda i: (0,) * len(shape))
    kern = functools.partial(_lru_prompt_kernel, tm=tm, conv_w=cw,
                             last_tile=(t_len - 1) // tm, last_row=(t_len - 1) % tm)
    return pl.pallas_call(
        kern,
        grid=(r // tm,),
        in_specs=[pl.BlockSpec((tm, w), lambda i: (i, 3)),
                  pl.BlockSpec((tm, w), lambda i: (i, 4)),
                  full((cw, w)), full((1, w)),
                  full((n_blk, bs, bs)), full((1, w)),
                  full((n_blk, bs, bs)), full((1, w)), full((1, w))],
        out_specs=[pl.BlockSpec((tm, w), lambda i: (i, 0)), full((1, w))],
        out_shape=[jax.ShapeDtypeStruct((r, w), BF16), jax.ShapeDtypeStruct((1, w), F32)],
        scratch_shapes=[pltpu.VMEM((tm + 8, w), F32), pltpu.VMEM((tm, w), F32),
                        pltpu.VMEM((1, w), F32)],
        compiler_params=_params(("arbitrary",)),
        name="lru_prompt",
    )(z, z, conv_w, conv_b, gaw, gab, gxw, gxb, lru_lambda)


def _lru_sample_kernel(xb_ref, gb_ref, cs_ref, h0_ref, cw_ref, cb_ref, gaw_ref, gab_ref, gxw_ref,
                       gxb_ref, lam_ref, yb_ref, hl_ref, *, conv_w):
    s_len, bsz, w = xb_ref.shape
    xp = [cs_ref[jj] for jj in range(conv_w - 1)] + [xb_ref[tt] for tt in range(s_len)]
    xcs = []
    for tt in range(s_len):
        xc = cb_ref[...] + cw_ref[0:1, :] * xp[tt]
        for jj in range(1, conv_w):
            xc = xc + cw_ref[jj:jj + 1, :] * xp[tt + jj]
        xcs.append(xc)
    xc = jnp.concatenate(xcs, axis=0)
    a, ig, mult = _lru_gates(xc, gaw_ref, gab_ref, gxw_ref, gxb_ref, lam_ref)
    u = xc * ig * mult
    h = h0_ref[...]
    for tt in range(s_len):
        rows = slice(tt * bsz, (tt + 1) * bsz)
        h = a[rows, :] * h + u[rows, :]
        yb_ref[tt] = (h * _gelu_tanh(gb_ref[tt])).astype(yb_ref.dtype)
    hl_ref[...] = h


def _lru_sample(xb, gb, conv_state, h0, conv_w, conv_b, gaw, gab, gxw, gxb, lru_lambda):
    s_len, bsz, w = xb.shape
    kern = functools.partial(_lru_sample_kernel, conv_w=conv_w.shape[0])
    return pl.pallas_call(
        kern,
        out_shape=[jax.ShapeDtypeStruct((s_len, bsz, w), BF16), jax.ShapeDtypeStruct((bsz, w), F32)],
        compiler_params=pltpu.CompilerParams(vmem_limit_bytes=VMEM_MB * 2**20),
        name="lru_sample",
    )(xb, gb, conv_state, h0, conv_w, conv_b, gaw, gab, gxw, gxb, lru_lambda)


def _out_proj_kernel(o_ref, yb_ref, x_ref, w_ref, g_ref, b_ref, out_ref, *, alpha):
    n_heads, _, hw = o_ref.shape
    aw = n_heads * hw
    mixed = _dot(yb_ref[...], w_ref[aw:, :])
    for h in range(n_heads):
        mixed = mixed + _dot(o_ref[h], w_ref[hw * h:hw * (h + 1), :])
    out_ref[...] = _layer_norm(alpha * x_ref[...] + mixed, g_ref[...], b_ref[...])


def _out_proj(o, yb, x, w, g, b, *, alpha, tm):
    r, d = x.shape
    n_heads, _, hw = o.shape
    bw = yb.shape[1]
    full = lambda shape: pl.BlockSpec(shape, lambda i: (0,) * len(shape))
    return pl.pallas_call(
        functools.partial(_out_proj_kernel, alpha=alpha),
        grid=(r // tm,),
        in_specs=[pl.BlockSpec((n_heads, tm, hw), lambda i: (0, i, 0)),
                  pl.BlockSpec((tm, bw), lambda i: (i, 0)),
                  pl.BlockSpec((tm, d), lambda i: (i, 0)),
                  full((n_heads * hw + bw, d)), full((1, d)), full((1, d))],
        out_specs=pl.BlockSpec((tm, d), lambda i: (i, 0)),
        out_shape=jax.ShapeDtypeStruct((r, d), F32),
        compiler_params=_params(("parallel",)),
        name="out_proj",
    )(o, yb, x, w, g, b)


def _mlp_kernel(x_ref, w1_ref, w2_ref, g_ref, b_ref, out_ref, xb_sc, acc_sc, *, alpha):
    f = pl.program_id(1)

    @pl.when(f == 0)
    def _():
        xb_sc[...] = x_ref[...].astype(BF16)
        acc_sc[...] = jnp.zeros_like(acc_sc)

    h = jnp.maximum(_dot(xb_sc[...], w1_ref[...]), 0.0)
    acc_sc[...] += _dot((h * h).astype(BF16), w2_ref[...])

    @pl.when(f == pl.num_programs(1) - 1)
    def _():
        out_ref[...] = _layer_norm(alpha * x_ref[...] + acc_sc[...], g_ref[...], b_ref[...])


def _mlp(x, w1, w2, g, b, layer, *, alpha, tm, tf):
    r, d = x.shape
    ff = w1.shape[2]
    vec = pl.BlockSpec((None, 1, d), lambda i, f: (layer, 0, 0))
    return pl.pallas_call(
        functools.partial(_mlp_kernel, alpha=alpha),
        grid=(r // tm, ff // tf),
        in_specs=[pl.BlockSpec((tm, d), lambda i, f: (i, 0)),
                  pl.BlockSpec((None, d, tf), lambda i, f: (layer, 0, f)),
                  pl.BlockSpec((None, tf, d), lambda i, f: (layer, f, 0)),
                  vec, vec],
        out_specs=pl.BlockSpec((tm, d), lambda i, f: (i, 0)),
        out_shape=jax.ShapeDtypeStruct((r, d), F32),
        scratch_shapes=[pltpu.VMEM((tm, d), BF16), pltpu.VMEM((tm, d), F32)],
        compiler_params=_params(("parallel", "arbitrary")),
        name="mlp",
    )(x, w1, w2, g, b)


def _pool_project(ms, x, w_ref, sc_ref, g_ref, b_ref, alpha):
    y = jnp.concatenate([_dot(ms[gi].astype(BF16), w_ref[gi]) for gi in range(len(ms))], axis=1)
    return _layer_norm(alpha * x + y * sc_ref[...], g_ref[...], b_ref[...])


def _pool_prompt_kernel(x_ref, w_ref, sc_ref, g_ref, b_ref, out_ref, ebuf, *, tm, alpha, pad):
    i = pl.program_id(0)
    d = x_ref.shape[1]
    gw = d // len(POOL_WINDOWS)

    @pl.when(i == 0)
    def _():
        ebuf[0:pad, :] = jnp.zeros((pad, d), F32)

    x = x_ref[...]
    ebuf[pad:pad + tm, :] = x
    pos = i * tm + lax.broadcasted_iota(jnp.int32, (tm, 1), 0)
    ms = []
    for gi, win in enumerate(POOL_WINDOWS):
        cols = slice(gi * gw, (gi + 1) * gw)
        ws = x[:, cols]
        for kk in range(1, win):
            ws = ws + ebuf[pad - kk:pad - kk + tm, cols]
        cnt = jnp.minimum(win, pos + 1).astype(F32)
        ms.append(ws / cnt - x[:, cols])
    ebuf[0:pad, :] = ebuf[tm:tm + pad, :]
    out_ref[...] = _pool_project(ms, x, w_ref, sc_ref, g_ref, b_ref, alpha)


def _pool_prompt(x, w, sc, g, b, *, alpha, tm):
    r, d = x.shape
    n_g, gw, _ = w.shape
    pad = 16
    assert max(POOL_WINDOWS) - 1 <= pad
    full = lambda shape: pl.BlockSpec(shape, lambda i: (0,) * len(shape))
    return pl.pallas_call(
        functools.partial(_pool_prompt_kernel, tm=tm, alpha=alpha, pad=pad),
        grid=(r // tm,),
        in_specs=[pl.BlockSpec((tm, d), lambda i: (i, 0)),
                  full((n_g, gw, gw)), full((1, d)), full((1, d)), full((1, d))],
        out_specs=pl.BlockSpec((tm, d), lambda i: (i, 0)),
        out_shape=jax.ShapeDtypeStruct((r, d), F32),
        scratch_shapes=[pltpu.VMEM((tm + pad, d), F32)],
        compiler_params=_params(("arbitrary",)),
        name="pool_prompt",
    )(x, w, sc, g, b)


def _pool_sample_kernel(x_ref, st_ref, w_ref, sc_ref, g_ref, b_ref, out_ref, *, alpha):
    s_len, bsz, d = x_ref.shape
    n_buf = st_ref.shape[0]
    gw = d // len(POOL_WINDOWS)
    ext = [st_ref[jj] for jj in range(n_buf)] + [x_ref[tt] for tt in range(s_len)]
    x = jnp.concatenate(ext[n_buf:], axis=0)
    ms = []
    for gi, win in enumerate(POOL_WINDOWS):
        cols = slice(gi * gw, (gi + 1) * gw)
        rows = []
        for tt in range(s_len):
            ws = ext[n_buf + tt][:, cols]
            for kk in range(1, win):
                ws = ws + ext[n_buf + tt - kk][:, cols]
            rows.append(ws / float(win) - ext[n_buf + tt][:, cols])
        ms.append(jnp.concatenate(rows, axis=0))
    y = _pool_project(ms, x, w_ref, sc_ref, g_ref, b_ref, alpha)
    for tt in range(s_len):
        out_ref[tt] = y[tt * bsz:(tt + 1) * bsz, :]


def _pool_sample(x, state, w, sc, g, b, *, alpha):
    return pl.pallas_call(
        functools.partial(_pool_sample_kernel, alpha=alpha),
        out_shape=jax.ShapeDtypeStruct(x.shape, F32),
        compiler_params=pltpu.CompilerParams(vmem_limit_bytes=VMEM_MB * 2**20),
        name="pool_sample",
    )(x, state, w, sc, g, b)


def kernel(x_prompt, x_sample, cache_k, cache_v, state_conv, state_lru, state_pool, page_table, meta_tokens, w_in, lam_q1, lam_k1, lam_q2, lam_k2, sub_norm_g, conv_w, conv_b, gate_a_w, gate_a_b, gate_x_w, gate_x_b, lru_lambda, w_out_ab, pool_w, pool_scale, mix_ln_g, mix_ln_b, w_ff1, w_ff2, ff_ln_g, ff_ln_b):
    n_prompt, seq, d = x_prompt.shape
    bsz, s_len, _ = x_sample.shape
    depth = w_ff1.shape[0]
    assert n_prompt == 1 and depth == 2
    n_meta = meta_tokens.shape[0]
    n_heads, hd2 = cache_k.shape[-2:]
    hd = hd2 // 2
    aw = n_heads * hd2
    bw = state_lru.shape[-1]
    assert aw == bw and w_in.shape[-1] == 3 * aw + 2 * bw
    page = cache_k.shape[2]
    n_buf = state_pool.shape[2]
    cw = conv_w.shape[1]
    assert s_len >= cw - 1
    alpha = (2.0 * depth) ** 0.25
    lam_init = 0.8 - 0.6 * math.exp(-0.3 * 0)

    t_len = n_meta + seq
    n_s = bsz * s_len
    rows = -(-(t_len + n_s) // ROW_ALIGN) * ROW_ALIGN
    xs_tm = x_sample.transpose(1, 0, 2).reshape(n_s, d)
    x0 = jnp.concatenate([meta_tokens, x_prompt[0], xs_tm,
                          jnp.zeros((rows - t_len - n_s, d), F32)], axis=0)

    row2 = lambda v: v.reshape(1, -1)
    lam_params = (row2(lam_q1[0]), row2(lam_k1[0]), row2(lam_q2[0]), row2(lam_k2[0]))
    sub_g = row2(sub_norm_g[0])

    assert rows - T_ATT < t_len
    z, qk, vb, k_rows, v_rows = _in_proj(x0, w_in[0].astype(BF16), t_len=t_len, tm=T_ATT,
                                         tn=TN_PROJ, n_heads=n_heads, hd=hd)
    zs = z[t_len:t_len + n_s]

    o = _attn_prompt(qk, vb, lam_params, sub_g, t=T_ATT, hd=hd, lam_init=lam_init)

    split = lambda c: zs[:, c * aw:(c + 1) * aw].reshape(s_len, bsz, n_heads, 2, hd)
    nr = 2 * n_heads * s_len
    q_s = split(0).transpose(1, 3, 2, 0, 4).reshape(bsz, nr, hd)
    kn = split(1).transpose(1, 0, 3, 2, 4)[:, :, :, :, None, :]
    kn = jnp.broadcast_to(kn, (bsz, s_len, 2, n_heads, s_len, hd)).reshape(bsz, s_len, nr, hd)
    vn = split(2).transpose(1, 0, 3, 2, 4)[:, :, :, None, :, None, :]
    vn = jnp.broadcast_to(vn, (bsz, s_len, 2, 2, n_heads, s_len, hd)).reshape(bsz, s_len, 2 * nr, hd)
    page_rows = lambda c: (c[0].reshape(-1, page, n_heads, 2, hd).transpose(0, 1, 3, 2, 4)
                           .reshape(-1, page * 2 * n_heads, hd))
    o_s = _attn_sample(page_table, q_s, kn, vn, page_rows(cache_k), page_rows(cache_v), lam_params,
                       sub_g, n_heads=n_heads, hd=hd, s_len=s_len, page=page,
                       gp=min(PAGES_PER_STEP, page_table.shape[1]), lam_init=lam_init)
    o_s = o_s.reshape(bsz, n_heads, s_len, hd2).transpose(1, 2, 0, 3).reshape(n_heads, n_s, hd2)
    o = lax.dynamic_update_slice(o, o_s.astype(BF16), (0, t_len, 0))

    gaw, gxw = gate_a_w[0].astype(BF16), gate_x_w[0].astype(BF16)
    lru_args = (conv_w[0], row2(conv_b[0]), gaw, row2(gate_a_b[0]), gxw, row2(gate_x_b[0]),
                row2(lru_lambda[0]))
    yb, h_last_p = _lru_prompt(z, *lru_args, t_len=t_len, tm=TM_LRU)
    xb_s = zs[:, 3 * aw:3 * aw + bw].reshape(s_len, bsz, bw)
    gb_s = zs[:, 3 * aw + bw:].reshape(s_len, bsz, bw)
    yb_s, h_last_s = _lru_sample(xb_s, gb_s, state_conv[0].transpose(1, 0, 2), state_lru[0], *lru_args)
    yb = lax.dynamic_update_slice(yb, yb_s.reshape(n_s, bw), (t_len, 0))

    x1 = _out_proj(o, yb, x0, w_out_ab[0].astype(BF16), row2(mix_ln_g[0]), row2(mix_ln_b[0]),
                   alpha=alpha, tm=TM_OUT)
    w1b, w2b = w_ff1.astype(BF16), w_ff2.astype(BF16)
    ff_g, ff_b = ff_ln_g[:, None, :], ff_ln_b[:, None, :]
    mlp = functools.partial(_mlp, alpha=alpha, tm=TM_MLP, tf=TF_MLP)
    x2 = mlp(x1, w1b, w2b, ff_g, ff_b, 0)

    pool_args = (pool_w[0].astype(BF16), row2(pool_scale[0]), row2(mix_ln_g[1]), row2(mix_ln_b[1]))
    x3 = _pool_prompt(x2, *pool_args, alpha=alpha, tm=TM_POOL)
    x2_s = x2[t_len:t_len + n_s].reshape(s_len, bsz, d)
    x3_s = _pool_sample(x2_s, state_pool[0].transpose(1, 0, 2), *pool_args, alpha=alpha)
    x3 = lax.dynamic_update_slice(x3, x3_s.reshape(n_s, d), (t_len, 0))
    x4 = mlp(x3, w1b, w2b, ff_g, ff_b, 1)

    to_bt = lambda v: v.reshape(s_len, bsz, -1).transpose(1, 0, 2)
    kv_p = lambda v: (v.reshape(t_len, 2, n_heads, hd).transpose(0, 2, 1, 3)
                      .reshape(1, 1, t_len, n_heads, hd2))
    kv_s = lambda c: to_bt(zs[:, c * aw:(c + 1) * aw]).reshape(1, bsz, s_len, n_heads, hd2)
    y_prompt = x4[n_meta:t_len][None]
    y_sample = to_bt(x4[t_len:t_len + n_s])
    new_conv_prompt = z[t_len - (cw - 1):t_len, 3 * aw:3 * aw + bw][None, None]
    new_lru_prompt = h_last_p[None]
    new_pool_prompt = x2[t_len - n_buf:t_len][None, None]
    new_conv_sample = xb_s[s_len - (cw - 1):].transpose(1, 0, 2)[None]
    new_lru_sample = h_last_s[None]
    new_pool_sample = jnp.concatenate([state_pool[0], x2_s.transpose(1, 0, 2)], axis=1)[:, -n_buf:][None]
    return (y_prompt, y_sample, kv_p(k_rows), kv_p(v_rows), new_conv_prompt, new_lru_prompt, new_pool_prompt,
            kv_s(1), kv_s(2), new_conv_sample, new_lru_sample, new_pool_sample)
```

```python
import functools
import math

import jax
import jax.numpy as jnp
from jax import lax
from jax.experimental import pallas as pl
from jax.experimental.pallas import tpu as pltpu

F32 = jnp.float32
BF16 = jnp.bfloat16

LN_EPS = 1e-5
RMS_EPS = 1e-5
LRU_C = 8.0
POOL_WINDOWS = (2, 4, 8, 16)
NEG = -1e30
LOG2E = 1.4426950408889634
LANES = 128
N_AUG = 3

ROW_ALIGN = 768
T_ATT = 768
TN_PROJ = 1024
TM_LRU = 256
TM_OUT = 384
TM_MLP = 768
TF_MLP = 512
TM_POOL = 256
ATT_HEAD_UNROLL = 2
PAGES_PER_STEP = 16
VMEM_MB = 56


def _params(sem, mb=VMEM_MB):
    return pltpu.CompilerParams(dimension_semantics=sem, vmem_limit_bytes=mb * 2**20)


def _layer_norm(y, g, b):
    mu = jnp.mean(y, axis=-1, keepdims=True)
    d = y - mu
    var = jnp.mean(d * d, axis=-1, keepdims=True)
    return d * lax.rsqrt(var + LN_EPS) * g + b


def _dot(a, b):
    return jnp.dot(a, b, preferred_element_type=F32)


def _dot_nt(a, b):
    return lax.dot_general(a, b, (((1,), (1,)), ((), ())), preferred_element_type=F32)


def _sigmoid(x):
    return 1.0 / (1.0 + jnp.exp(-x))


def _gelu_tanh(x):
    c = math.sqrt(2.0 / math.pi)
    return 0.5 * x * (1.0 + jnp.tanh(c * (x + 0.044715 * (x * x * x))))


def _lambda_value(lq1, lk1, lq2, lk2, lam_init):
    s1 = jnp.sum(lq1 * lk1, axis=-1, keepdims=True)
    s2 = jnp.sum(lq2 * lk2, axis=-1, keepdims=True)
    return jnp.exp(s1) - jnp.exp(s2) + lam_init


def _div_pow2(x, n):
    assert n & (n - 1) == 0
    return lax.shift_right_logical(x, n.bit_length() - 1)


def _mod_pow2(x, n):
    assert n & (n - 1) == 0
    return lax.bitwise_and(x, n - 1)


BF16_ROWS = 16


def _cast_specs(job, n_steps, step_of):
    mat, row_start, rows = job
    cols = mat.shape[1]
    units = rows // BF16_ROWS
    assert units * BF16_ROWS == rows
    n_blocks = max(k for k in range(1, n_steps + 1) if units % k == 0)
    br = rows // n_blocks
    assert row_start % br == 0
    first = row_start // br
    block = lambda *idx: jnp.minimum(step_of(*idx), n_blocks - 1)
    return (pl.BlockSpec((br, cols), lambda *idx: (first + block(*idx), 0)),
            pl.BlockSpec((br, cols), lambda *idx: (block(*idx), 0)))


def _cast_shape(job):
    mat, _, rows = job
    return jax.ShapeDtypeStruct((rows, mat.shape[1]), BF16)


def _cast_blocks(srcs, dsts):
    for src, dst in zip(srcs, dsts):
        dst[...] = src[...].astype(BF16)


def _head_slope(h, n_heads):
    return 2.0 ** (-8.0 * (h + 1) / n_heads)


def _in_proj_kernel(x_ref, w_ref, z_ref, qk_ref, vb_ref, kr_ref, vr_ref, *, n_heads, hd):
    j = pl.program_id(1)
    z = _dot(x_ref[...].astype(BF16), w_ref[...])
    z_ref[...] = z
    tm = z.shape[0]
    lane = lax.broadcasted_iota(jnp.int32, (tm, hd), 1)

    def cache_rows(dst_ref):
        for c in range(2):
            for h in range(n_heads):
                col = (2 * h + c) * hd
                dst_ref[pl.ds(c * n_heads + h, tm, stride=2 * n_heads), :] = z[:, col:col + hd]

    @pl.when(j == 0)
    def _():
        ones = jnp.where(lane < N_AUG, 1.0, 0.0).astype(BF16)
        q_scale = hd ** -0.5 * LOG2E
        for h in range(n_heads):
            for c in range(2):
                col = (2 * h + c) * hd
                qk_ref[0, h, :, 2 * hd * c:2 * hd * c + hd] = (z[:, col:col + hd] * q_scale).astype(BF16)
                qk_ref[0, h, :, 2 * hd * c + hd:2 * hd * (c + 1)] = ones

    @pl.when(j == 1)
    def _():
        row = lax.broadcasted_iota(jnp.int32, (tm, hd), 0).astype(F32)
        for h in range(n_heads):
            a = row * (_head_slope(h, n_heads) * LOG2E)
            hi = a.astype(BF16).astype(F32)
            mid = (a - hi).astype(BF16).astype(F32)
            lo = a - hi - mid
            aug = jnp.where(lane == 0, hi, jnp.where(lane == 1, mid, jnp.where(lane == 2, lo, 0.0)))
            aug = aug.astype(BF16)
            for c in range(2):
                col = (2 * h + c) * hd
                qk_ref[0, h, :, 2 * hd * c:2 * hd * c + hd] = z[:, col:col + hd].astype(BF16)
                qk_ref[0, h, :, 2 * hd * c + hd:2 * hd * (c + 1)] = aug
        cache_rows(kr_ref)

    @pl.when(j == 2)
    def _():
        for h in range(n_heads):
            vb_ref[h] = z[:, 2 * hd * h:2 * hd * (h + 1)].astype(BF16)
        cache_rows(vr_ref)


def _in_proj(x, w, *, t_len, tm, tn, n_heads, hd):
    r, d = x.shape
    n = w.shape[1]
    assert hd == LANES and tn == 2 * hd * n_heads
    kr = 2 * n_heads
    cache_blk = pl.BlockSpec((tm * kr, hd), lambda i, j: (i, 0))
    cache_shape = jax.ShapeDtypeStruct((t_len * kr, hd), F32)
    return pl.pallas_call(
        functools.partial(_in_proj_kernel, n_heads=n_heads, hd=hd),
        grid=(r // tm, n // tn),
        in_specs=[pl.BlockSpec((tm, d), lambda i, j: (i, 0)),
                  pl.BlockSpec((d, tn), lambda i, j: (0, j))],
        out_specs=[pl.BlockSpec((tm, tn), lambda i, j: (i, j)),
                   pl.BlockSpec((1, n_heads, tm, 4 * hd), lambda i, j: (jnp.minimum(j, 1), 0, i, 0)),
                   pl.BlockSpec((n_heads, tm, 2 * hd), lambda i, j: (0, i, 0)),
                   cache_blk, cache_blk],
        out_shape=[jax.ShapeDtypeStruct((r, n), F32),
                   jax.ShapeDtypeStruct((2, n_heads, r, 4 * hd), BF16),
                   jax.ShapeDtypeStruct((n_heads, r, 2 * hd), BF16),
                   cache_shape, cache_shape],
        compiler_params=_params(("arbitrary", "arbitrary")),
        name="in_proj",
    )(x, w)


def _attn_prompt_kernel(qi_ref, kj_ref, qa_ref, ka_ref, vb_ref, sl_ref, lq1_ref, lk1_ref, lq2_ref,
                        lk2_ref, g_ref, *rest, t, n_heads, hd, lam_init, n_cast):
    cast_src = rest[:n_cast]
    o_ref = rest[n_cast]
    cast_dst = rest[n_cast + 1:2 * n_cast + 1]
    m_sc, l_sc, acc_sc = rest[2 * n_cast + 1:]
    _cast_blocks(cast_src, cast_dst)
    p = pl.program_id(0)
    i = qi_ref[p]
    j = kj_ref[p]

    @pl.when(j == 0)
    def _():
        m_sc[...] = jnp.full_like(m_sc, NEG)
        l_sc[...] = jnp.zeros_like(l_sc)
        acc_sc[...] = jnp.zeros_like(acc_sc)

    def tile(masked):
        if masked:
            keep = (lax.broadcasted_iota(jnp.int32, (t, t), 1)
                    <= lax.broadcasted_iota(jnp.int32, (t, t), 0))
        tile_dist = ((j - i) * t).astype(F32)

        def head(h, carry):
            shift = sl_ref[h][:, 0:1] * tile_dist
            for c in range(2):
                s = _dot_nt(qa_ref[h, :, 2 * hd * c:2 * hd * (c + 1)],
                            ka_ref[h, :, 2 * hd * c:2 * hd * (c + 1)])
                if masked:
                    s = jnp.where(keep, s, NEG)
                idx = 2 * h + c
                m_old = m_sc[idx]
                m_new = jnp.maximum(m_old, jnp.max(s, axis=1, keepdims=True) + shift)
                alpha = jnp.exp2(m_old - m_new)
                pr = jnp.exp2(s - (m_new - shift))
                l_sc[idx] = alpha * l_sc[idx] + jnp.sum(pr, axis=1, keepdims=True)
                acc_sc[idx] = alpha * acc_sc[idx] + _dot(pr.astype(BF16), vb_ref[h])
                m_sc[idx] = m_new
            return carry

        lax.fori_loop(0, n_heads, head, 0, unroll=ATT_HEAD_UNROLL)

    @pl.when(j < i)
    def _():
        tile(False)

    @pl.when(j == i)
    def _():
        tile(True)
        lam = _lambda_value(lq1_ref[...], lk1_ref[...], lq2_ref[...], lk2_ref[...], lam_init)
        g = g_ref[...]
        for h in range(n_heads):
            o1 = acc_sc[2 * h] / l_sc[2 * h]
            o2 = acc_sc[2 * h + 1] / l_sc[2 * h + 1]
            oh = o1 - lam * o2
            oh = oh * lax.rsqrt(jnp.mean(oh * oh, axis=-1, keepdims=True) + RMS_EPS)
            o_ref[h] = (oh * g * (1.0 - lam_init)).astype(o_ref.dtype)


def _attn_prompt(qk, vb, lam_params, sub_g, *, t, hd, lam_init, cast=()):
    _, n_heads, r, _ = qk.shape
    nq = r // t
    pairs = [(i, j) for i in range(nq) for j in range(i + 1)]
    qi = jnp.asarray([i for i, _ in pairs], jnp.int32)
    kj = jnp.asarray([j for _, j in pairs], jnp.int32)
    slopes = jnp.asarray([[[_head_slope(h, n_heads) * LOG2E] * LANES] for h in range(n_heads)], F32)
    vec = lambda n: pl.BlockSpec((1, n), lambda p, qi, kj: (0, 0))
    cast_specs = [_cast_specs(job, len(pairs), lambda p, qi, kj: p) for job in cast]
    grid_spec = pltpu.PrefetchScalarGridSpec(
        num_scalar_prefetch=2,
        grid=(len(pairs),),
        in_specs=[pl.BlockSpec((None, n_heads, t, 4 * hd), lambda p, qi, kj: (0, 0, qi[p], 0)),
                  pl.BlockSpec((None, n_heads, t, 4 * hd), lambda p, qi, kj: (1, 0, kj[p], 0)),
                  pl.BlockSpec((n_heads, t, 2 * hd), lambda p, qi, kj: (0, kj[p], 0)),
                  pl.BlockSpec((n_heads, 1, LANES), lambda p, qi, kj: (0, 0, 0)),
                  vec(hd), vec(hd), vec(hd), vec(hd), vec(2 * hd)] + [s[0] for s in cast_specs],
        out_specs=[pl.BlockSpec((n_heads, t, 2 * hd), lambda p, qi, kj: (0, qi[p], 0))]
                  + [s[1] for s in cast_specs],
        scratch_shapes=[pltpu.VMEM((2 * n_heads, t, 1), F32),
                        pltpu.VMEM((2 * n_heads, t, 1), F32),
                        pltpu.VMEM((2 * n_heads, t, 2 * hd), F32)],
    )
    kern = functools.partial(_attn_prompt_kernel, t=t, n_heads=n_heads, hd=hd, lam_init=lam_init,
                             n_cast=len(cast))
    return pl.pallas_call(
        kern, grid_spec=grid_spec,
        out_shape=[jax.ShapeDtypeStruct((n_heads, r, 2 * hd), BF16)] + [_cast_shape(job) for job in cast],
        compiler_params=_params(("arbitrary",)),
        name="attn_prompt",
    )(qi, kj, qk, qk, vb, slopes, *lam_params, sub_g, *[job[0] for job in cast])


def _attn_sample_kernel(pt_ref, q_ref, kn_ref, vn_ref, *rest, n_heads, hd, s_len, past_len, page,
                        gp, lam_init):
    del pt_ref
    page_refs = rest[:2 * gp]
    (lq1_ref, lk1_ref, lq2_ref, lk2_ref, g_ref, o_ref,
     qb_sc, bias_sc, m_sc, l_sc, acc_sc) = rest[2 * gp:]
    p = pl.program_id(1)
    nr = 2 * n_heads * s_len
    kr = 2 * n_heads
    ncol = page * kr
    half = nr // 2
    scale = hd ** -0.5

    def row_info(shape):
        row = lax.broadcasted_iota(jnp.int32, shape, 0)
        tok = _mod_pow2(row, s_len)
        head = _mod_pow2(_div_pow2(row, s_len), n_heads)
        cmap = _div_pow2(row, s_len * n_heads)
        slope = jnp.zeros(shape, F32)
        for h in range(n_heads):
            slope = jnp.where(head == h, _head_slope(h, n_heads), slope)
        return tok, head, cmap, slope

    @pl.when(p == 0)
    def _():
        qb_sc[...] = (q_ref[0] * scale).astype(BF16)
        tok, head, cmap, slope = row_info((nr, ncol))
        col = lax.broadcasted_iota(jnp.int32, (nr, ncol), 1)
        key = _div_pow2(col, kr)
        own = _mod_pow2(col, kr) == cmap * n_heads + head
        bias_sc[...] = jnp.where(own, -slope * (past_len + tok - key).astype(F32), NEG)
        m_sc[...] = jnp.full_like(m_sc, NEG)
        l_sc[...] = jnp.zeros_like(l_sc)
        acc_sc[...] = jnp.zeros_like(acc_sc)

    tok1, _, _, slope1 = row_info((nr, 1))
    m = m_sc[...]
    l = l_sc[...]
    acc = acc_sc[...]
    scores, shifts = [], []
    m_new = m
    for gi in range(gp):
        kp = page_refs[gi][0].astype(BF16)
        s = _dot_nt(qb_sc[...], kp) + bias_sc[...]
        shift = slope1 * ((p * gp + gi) * page).astype(F32)
        m_new = jnp.maximum(m_new, jnp.max(s, axis=1, keepdims=True) + shift)
        scores.append(s)
        shifts.append(shift)
    alpha = jnp.exp(m - m_new)
    l = alpha * l
    pv = jnp.zeros((2 * nr, hd), F32)
    for gi in range(gp):
        pr = jnp.exp(scores[gi] - (m_new - shifts[gi]))
        l = l + jnp.sum(pr, axis=1, keepdims=True)
        top, bot = pr[:half], pr[half:]
        chunks = range(ncol // LANES)
        top_r = jnp.concatenate(
            [pltpu.roll(top[:, LANES * cc:LANES * (cc + 1)], n_heads, 1) for cc in chunks], axis=1)
        bot_r = jnp.concatenate(
            [pltpu.roll(bot[:, LANES * cc:LANES * (cc + 1)], LANES - n_heads, 1) for cc in chunks], axis=1)
        lhs = jnp.concatenate([top, bot_r, top_r, bot], axis=0).astype(BF16)
        vp = page_refs[gp + gi][0].astype(BF16)
        pv = pv + _dot(lhs, vp)
    acc = jnp.concatenate([alpha, alpha], axis=0) * acc + pv
    m = m_new
    m_sc[...] = m
    l_sc[...] = l
    acc_sc[...] = acc

    @pl.when(p == pl.num_programs(1) - 1)
    def _():
        qf = q_ref[0] * scale
        s_new = []
        for t2 in range(s_len):
            st = jnp.sum(qf * kn_ref[0, t2], axis=1, keepdims=True)
            st = st - slope1 * (tok1 - t2).astype(F32)
            s_new.append(jnp.where(t2 <= tok1, st, NEG))
        m_fin = m
        for st in s_new:
            m_fin = jnp.maximum(m_fin, st)
        a_fin = jnp.exp(m - m_fin)
        l_fin = a_fin * l
        acc_f = jnp.concatenate([a_fin, a_fin], axis=0) * acc
        for t2 in range(s_len):
            pt = jnp.exp(s_new[t2] - m_fin)
            l_fin = l_fin + pt
            acc_f = acc_f + jnp.concatenate([pt, pt], axis=0) * vn_ref[0, t2]
        lam = _lambda_value(lq1_ref[...], lk1_ref[...], lq2_ref[...], lk2_ref[...], lam_init)
        halves = []
        for c2 in range(2):
            a = acc_f[c2 * nr:(c2 + 1) * nr] / l_fin
            halves.append(a[:half] - lam * a[half:])
        ms = (jnp.sum(halves[0] * halves[0], axis=1, keepdims=True)
              + jnp.sum(halves[1] * halves[1], axis=1, keepdims=True)) / (2.0 * hd)
        rs = lax.rsqrt(ms + RMS_EPS) * (1.0 - lam_init)
        for c2 in range(2):
            o_ref[0, :, c2 * hd:(c2 + 1) * hd] = halves[c2] * rs * g_ref[:, c2 * hd:(c2 + 1) * hd]


def _attn_sample(page_table, q, kn, vn, ck, cv, lam_params, sub_g, *, n_heads, hd, s_len, page,
                 gp, lam_init):
    bsz, nr, _ = q.shape
    n_pages = page_table.shape[1]
    assert n_pages % gp == 0 and hd == LANES and nr == 2 * n_heads * s_len
    ncol = ck.shape[1]
    pg = lambda gi: pl.BlockSpec((1, ncol, hd), lambda b, p, pt: (pt[b, p * gp + gi], 0, 0))
    vec = lambda n: pl.BlockSpec((1, n), lambda b, p, pt: (0, 0))
    grid_spec = pltpu.PrefetchScalarGridSpec(
        num_scalar_prefetch=1,
        grid=(bsz, n_pages // gp),
        in_specs=[pl.BlockSpec((1, nr, hd), lambda b, p, pt: (b, 0, 0)),
                  pl.BlockSpec((1, s_len, nr, hd), lambda b, p, pt: (b, 0, 0, 0)),
                  pl.BlockSpec((1, s_len, 2 * nr, hd), lambda b, p, pt: (b, 0, 0, 0))]
                 + [pg(gi) for gi in range(gp)] + [pg(gi) for gi in range(gp)]
                 + [vec(hd), vec(hd), vec(hd), vec(hd), vec(2 * hd)],
        out_specs=pl.BlockSpec((1, nr // 2, 2 * hd), lambda b, p, pt: (b, 0, 0)),
        scratch_shapes=[pltpu.VMEM((nr, hd), BF16),
                        pltpu.VMEM((nr, ncol), F32),
                        pltpu.VMEM((nr, 1), F32),
                        pltpu.VMEM((nr, 1), F32),
                        pltpu.VMEM((2 * nr, hd), F32)],
    )
    kern = functools.partial(_attn_sample_kernel, n_heads=n_heads, hd=hd, s_len=s_len,
                             past_len=n_pages * page, page=page, gp=gp, lam_init=lam_init)
    return pl.pallas_call(
        kern, grid_spec=grid_spec,
        out_shape=jax.ShapeDtypeStruct((bsz, nr // 2, 2 * hd), F32),
        compiler_params=_params(("arbitrary", "arbitrary")),
        name="attn_sample",
    )(page_table, q, kn, vn, *([ck] * gp), *([cv] * gp), *lam_params, sub_g)


def _lru_gates(xc, gaw_ref, gab_ref, gxw_ref, gxb_ref, lam_ref):
    n_blk, bs, _ = gaw_ref.shape
    xcb = xc.astype(BF16)
    ra = jnp.concatenate([_dot(xcb[:, n * bs:(n + 1) * bs], gaw_ref[n]) for n in range(n_blk)], axis=1)
    rx = jnp.concatenate([_dot(xcb[:, n * bs:(n + 1) * bs], gxw_ref[n]) for n in range(n_blk)], axis=1)
    r = _sigmoid(ra + gab_ref[...])
    ig = _sigmoid(rx + gxb_ref[...])
    neg_lam = -lam_ref[...]
    softplus = jnp.maximum(neg_lam, 0.0) + jnp.log1p(jnp.exp(-jnp.abs(neg_lam)))
    log_a = -LRU_C * r * softplus
    a = jnp.exp(log_a)
    mult = jnp.sqrt(-jnp.tanh(log_a) * (a * a + 1.0))
    return a, ig, mult


def _lru_prompt_kernel(xb_ref, gb_ref, cw_ref, cb_ref, gaw_ref, gab_ref, gxw_ref, gxb_ref, lam_ref,
                       *rest, tm, conv_w, last_tile, last_row, n_cast):
    cast_src = rest[:n_cast]
    yb_ref, hl_ref = rest[n_cast:n_cast + 2]
    cast_dst = rest[n_cast + 2:2 * n_cast + 2]
    xbuf, hbuf, hcar = rest[2 * n_cast + 2:]
    _cast_blocks(cast_src, cast_dst)
    i = pl.program_id(0)
    w = xb_ref.shape[1]
    n_grp = tm // 8

    @pl.when(i == 0)
    def _():
        xbuf[0:8, :] = jnp.zeros((8, w), F32)
        hcar[...] = jnp.zeros_like(hcar)

    xbuf[8:8 + tm, :] = xb_ref[...]
    xc = cb_ref[...] + cw_ref[conv_w - 1:conv_w, :] * xb_ref[...]
    for jj in range(conv_w - 1):
        start = 8 - (conv_w - 1) + jj
        xc = xc + cw_ref[jj:jj + 1, :] * xbuf[start:start + tm, :]
    xbuf[0:8, :] = xbuf[tm:tm + 8, :]

    a, ig, mult = _lru_gates(xc, gaw_ref, gab_ref, gxw_ref, gxb_ref, lam_ref)
    grow = i * tm + lax.broadcasted_iota(jnp.int32, (tm, 1), 0)
    mult = jnp.where(grow == 0, 1.0, mult)
    u = xc * ig * mult

    a3 = a.reshape(n_grp, 8, w)
    u3 = u.reshape(n_grp, 8, w)
    sub = lax.broadcasted_iota(jnp.int32, (n_grp, 8, w), 1)
    for sh in (1, 2, 4):
        a_prev = pltpu.roll(a3, sh, axis=1)
        u_prev = pltpu.roll(u3, sh, axis=1)
        ok = sub >= sh
        u3 = jnp.where(ok, a3 * u_prev + u3, u3)
        a3 = jnp.where(ok, a3 * a_prev, a3)
    h_prev = hcar[...]
    for gi in range(n_grp):
        hg = a3[gi] * h_prev + u3[gi]
        hbuf[8 * gi:8 * (gi + 1), :] = hg
        h_prev = hg[7:8, :]
    hcar[...] = h_prev

    yb_ref[...] = (hbuf[...] * _gelu_tanh(gb_ref[...])).astype(yb_ref.dtype)

    @pl.when(i == last_tile)
    def _():
        hl_ref[...] = hbuf[last_row:last_row + 1, :]


def _lru_prompt(z, conv_w, conv_b, gaw, gab, gxw, gxb, lru_lambda, *, t_len, tm, cast=()):
    r = z.shape[0]
    w = conv_b.shape[1]
    cw = conv_w.shape[0]
    n_blk, bs, _ = gaw.shape
    n_steps = r // tm
    full = lambda shape: pl.BlockSpec(shape, lambda i: (0,) * len(shape))
    cast_specs = [_cast_specs(job, n_steps, lambda i: i) for job in cast]
    kern = functools.partial(_lru_prompt_kernel, tm=tm, conv_w=cw, n_cast=len(cast),
                             last_tile=(t_len - 1) // tm, last_row=(t_len - 1) % tm)
    return pl.pallas_call(
        kern,
        grid=(n_steps,),
        in_specs=[pl.BlockSpec((tm, w), lambda i: (i, 3)),
                  pl.BlockSpec((tm, w), lambda i: (i, 4)),
                  full((cw, w)), full((1, w)),
                  full((n_blk, bs, bs)), full((1, w)),
                  full((n_blk, bs, bs)), full((1, w)), full((1, w))] + [s[0] for s in cast_specs],
        out_specs=[pl.BlockSpec((tm, w), lambda i: (i, 0)), full((1, w))] + [s[1] for s in cast_specs],
        out_shape=[jax.ShapeDtypeStruct((r, w), BF16), jax.ShapeDtypeStruct((1, w), F32)]
                  + [_cast_shape(job) for job in cast],
        scratch_shapes=[pltpu.VMEM((tm + 8, w), F32), pltpu.VMEM((tm, w), F32),
                        pltpu.VMEM((1, w), F32)],
        compiler_params=_params(("arbitrary",)),
        name="lru_prompt",
    )(z, z, conv_w, conv_b, gaw, gab, gxw, gxb, lru_lambda, *[job[0] for job in cast])


def _lru_sample_kernel(xb_ref, gb_ref, cs_ref, h0_ref, cw_ref, cb_ref, gaw_ref, gab_ref, gxw_ref,
                       gxb_ref, lam_ref, yb_ref, hl_ref, *, conv_w):
    s_len, bsz, w = xb_ref.shape
    xp = [cs_ref[jj] for jj in range(conv_w - 1)] + [xb_ref[tt] for tt in range(s_len)]
    xcs = []
    for tt in range(s_len):
        xc = cb_ref[...] + cw_ref[0:1, :] * xp[tt]
        for jj in range(1, conv_w):
            xc = xc + cw_ref[jj:jj + 1, :] * xp[tt + jj]
        xcs.append(xc)
    xc = jnp.concatenate(xcs, axis=0)
    a, ig, mult = _lru_gates(xc, gaw_ref, gab_ref, gxw_ref, gxb_ref, lam_ref)
    u = xc * ig * mult
    h = h0_ref[...]
    for tt in range(s_len):
        rows = slice(tt * bsz, (tt + 1) * bsz)
        h = a[rows, :] * h + u[rows, :]
        yb_ref[tt] = (h * _gelu_tanh(gb_ref[tt])).astype(yb_ref.dtype)
    hl_ref[...] = h


def _lru_sample(xb, gb, conv_state, h0, conv_w, conv_b, gaw, gab, gxw, gxb, lru_lambda):
    s_len, bsz, w = xb.shape
    kern = functools.partial(_lru_sample_kernel, conv_w=conv_w.shape[0])
    return pl.pallas_call(
        kern,
        out_shape=[jax.ShapeDtypeStruct((s_len, bsz, w), BF16), jax.ShapeDtypeStruct((bsz, w), F32)],
        compiler_params=pltpu.CompilerParams(vmem_limit_bytes=VMEM_MB * 2**20),
        name="lru_sample",
    )(xb, gb, conv_state, h0, conv_w, conv_b, gaw, gab, gxw, gxb, lru_lambda)


def _out_proj_kernel(o_ref, yb_ref, x_ref, w_ref, g_ref, b_ref, out_ref, *, alpha):
    n_heads, _, hw = o_ref.shape
    aw = n_heads * hw
    mixed = _dot(yb_ref[...], w_ref[aw:, :])
    for h in range(n_heads):
        mixed = mixed + _dot(o_ref[h], w_ref[hw * h:hw * (h + 1), :])
    out_ref[...] = _layer_norm(alpha * x_ref[...] + mixed, g_ref[...], b_ref[...])


def _out_proj(o, yb, x, w, g, b, *, alpha, tm):
    r, d = x.shape
    n_heads, _, hw = o.shape
    bw = yb.shape[1]
    full = lambda shape: pl.BlockSpec(shape, lambda i: (0,) * len(shape))
    return pl.pallas_call(
        functools.partial(_out_proj_kernel, alpha=alpha),
        grid=(r // tm,),
        in_specs=[pl.BlockSpec((n_heads, tm, hw), lambda i: (0, i, 0)),
                  pl.BlockSpec((tm, bw), lambda i: (i, 0)),
                  pl.BlockSpec((tm, d), lambda i: (i, 0)),
                  full((n_heads * hw + bw, d)), full((1, d)), full((1, d))],
        out_specs=pl.BlockSpec((tm, d), lambda i: (i, 0)),
        out_shape=jax.ShapeDtypeStruct((r, d), F32),
        compiler_params=_params(("parallel",)),
        name="out_proj",
    )(o, yb, x, w, g, b)


def _mlp_kernel(x_ref, w1_ref, w2_ref, g_ref, b_ref, out_ref, xb_sc, acc_sc, *, alpha):
    f = pl.program_id(1)

    @pl.when(f == 0)
    def _():
        xb_sc[...] = x_ref[...].astype(BF16)
        acc_sc[...] = jnp.zeros_like(acc_sc)

    h = jnp.maximum(_dot(xb_sc[...], w1_ref[...]), 0.0)
    acc_sc[...] += _dot((h * h).astype(BF16), w2_ref[...])

    @pl.when(f == pl.num_programs(1) - 1)
    def _():
        out_ref[...] = _layer_norm(alpha * x_ref[...] + acc_sc[...], g_ref[...], b_ref[...])


def _mlp_final_kernel(x_ref, w1_ref, w2_ref, g_ref, b_ref, yp_ref, ys_ref, xb_sc, acc_sc, *, alpha,
                      n_meta, n_tail, s_start, n_s):
    i = pl.program_id(0)
    f = pl.program_id(1)
    last = pl.num_programs(0) - 1

    @pl.when(f == 0)
    def _():
        xb_sc[...] = x_ref[...].astype(BF16)
        acc_sc[...] = jnp.zeros_like(acc_sc)

    h = jnp.maximum(_dot(xb_sc[...], w1_ref[...]), 0.0)
    acc_sc[...] += _dot((h * h).astype(BF16), w2_ref[...])

    def result():
        return _layer_norm(alpha * x_ref[...] + acc_sc[...], g_ref[...], b_ref[...])

    @pl.when(jnp.logical_and(f == pl.num_programs(1) - 1, i < last))
    def _():
        yp_ref[...] = result()

    @pl.when(jnp.logical_and(f == pl.num_programs(1) - 1, i == last))
    def _():
        res = result()
        yp_ref[0:n_tail, :] = res[n_meta:n_meta + n_tail, :]
        ys_ref[...] = res[s_start:s_start + n_s, :]


def _mlp(x, w1, w2, g, b, *, alpha, tm, tf):
    r, d = x.shape
    ff = w1.shape[1]
    vec = pl.BlockSpec((1, d), lambda i, f: (0, 0))
    return pl.pallas_call(
        functools.partial(_mlp_kernel, alpha=alpha),
        grid=(r // tm, ff // tf),
        in_specs=[pl.BlockSpec((tm, d), lambda i, f: (i, 0)),
                  pl.BlockSpec((d, tf), lambda i, f: (0, f)),
                  pl.BlockSpec((tf, d), lambda i, f: (f, 0)),
                  vec, vec],
        out_specs=pl.BlockSpec((tm, d), lambda i, f: (i, 0)),
        out_shape=jax.ShapeDtypeStruct((r, d), F32),
        scratch_shapes=[pltpu.VMEM((tm, d), BF16), pltpu.VMEM((tm, d), F32)],
        compiler_params=_params(("parallel", "arbitrary")),
        name="mlp",
    )(x, w1, w2, g, b)


def _mlp_final(x, w1, w2, g, b, *, alpha, tm, tf, n_meta, seq, n_s):
    r, d = x.shape
    ff = w1.shape[1]
    n_i = r // tm
    last_row = (n_i - 1) * tm
    n_tail = seq - last_row
    s_start = n_meta + seq - last_row
    assert pl.cdiv(seq, tm) == n_i and 0 < n_tail <= tm - n_meta and s_start + n_s <= tm
    assert n_meta % 8 == 0
    vec = pl.BlockSpec((1, d), lambda i, f: (0, 0))
    x_rows = lambda i, f: (pl.multiple_of(jnp.where(i < n_i - 1, n_meta + tm * i, last_row), 8), 0)
    kern = functools.partial(_mlp_final_kernel, alpha=alpha, n_meta=n_meta, n_tail=n_tail,
                             s_start=s_start, n_s=n_s)
    return pl.pallas_call(
        kern,
        grid=(n_i, ff // tf),
        in_specs=[pl.BlockSpec((pl.Element(tm), pl.Element(d)), x_rows),
                  pl.BlockSpec((d, tf), lambda i, f: (0, f)),
                  pl.BlockSpec((tf, d), lambda i, f: (f, 0)),
                  vec, vec],
        out_specs=[pl.BlockSpec((tm, d), lambda i, f: (i, 0)),
                   pl.BlockSpec((n_s, d), lambda i, f: (0, 0))],
        out_shape=[jax.ShapeDtypeStruct((seq, d), F32), jax.ShapeDtypeStruct((n_s, d), F32)],
        scratch_shapes=[pltpu.VMEM((tm, d), BF16), pltpu.VMEM((tm, d), F32)],
        compiler_params=_params(("arbitrary", "arbitrary")),
        name="mlp_final",
    )(x, w1, w2, g, b)


def _pool_project(ms, x, w_ref, sc_ref, g_ref, b_ref, alpha):
    y = jnp.concatenate([_dot(ms[gi].astype(BF16), w_ref[gi]) for gi in range(len(ms))], axis=1)
    return _layer_norm(alpha * x + y * sc_ref[...], g_ref[...], b_ref[...])


def _pool_prompt_kernel(x_ref, w_ref, sc_ref, g_ref, b_ref, out_ref, ebuf, *, tm, alpha, pad):
    i = pl.program_id(0)
    d = x_ref.shape[1]
    gw = d // len(POOL_WINDOWS)

    @pl.when(i == 0)
    def _():
        ebuf[0:pad, :] = jnp.zeros((pad, d), F32)

    x = x_ref[...]
    ebuf[pad:pad + tm, :] = x
    pos = i * tm + lax.broadcasted_iota(jnp.int32, (tm, 1), 0)
    ms = []
    for gi, win in enumerate(POOL_WINDOWS):
        cols = slice(gi * gw, (gi + 1) * gw)
        ws = x[:, cols]
        for kk in range(1, win):
            ws = ws + ebuf[pad - kk:pad - kk + tm, cols]
        cnt = jnp.minimum(win, pos + 1).astype(F32)
        ms.append(ws / cnt - x[:, cols])
    ebuf[0:pad, :] = ebuf[tm:tm + pad, :]
    out_ref[...] = _pool_project(ms, x, w_ref, sc_ref, g_ref, b_ref, alpha)


def _pool_prompt(x, w, sc, g, b, *, alpha, tm):
    r, d = x.shape
    n_g, gw, _ = w.shape
    pad = 16
    assert max(POOL_WINDOWS) - 1 <= pad
    full = lambda shape: pl.BlockSpec(shape, lambda i: (0,) * len(shape))
    return pl.pallas_call(
        functools.partial(_pool_prompt_kernel, tm=tm, alpha=alpha, pad=pad),
        grid=(r // tm,),
        in_specs=[pl.BlockSpec((tm, d), lambda i: (i, 0)),
                  full((n_g, gw, gw)), full((1, d)), full((1, d)), full((1, d))],
        out_specs=pl.BlockSpec((tm, d), lambda i: (i, 0)),
        out_shape=jax.ShapeDtypeStruct((r, d), F32),
        scratch_shapes=[pltpu.VMEM((tm + pad, d), F32)],
        compiler_params=_params(("arbitrary",)),
        name="pool_prompt",
    )(x, w, sc, g, b)


def _pool_sample_kernel(x_ref, st_ref, w_ref, sc_ref, g_ref, b_ref, out_ref, *, alpha):
    s_len, bsz, d = x_ref.shape
    n_buf = st_ref.shape[0]
    gw = d // len(POOL_WINDOWS)
    ext = [st_ref[jj] for jj in range(n_buf)] + [x_ref[tt] for tt in range(s_len)]
    x = jnp.concatenate(ext[n_buf:], axis=0)
    ms = []
    for gi, win in enumerate(POOL_WINDOWS):
        cols = slice(gi * gw, (gi + 1) * gw)
        rows = []
        for tt in range(s_len):
            ws = ext[n_buf + tt][:, cols]
            for kk in range(1, win):
                ws = ws + ext[n_buf + tt - kk][:, cols]
            rows.append(ws / float(win) - ext[n_buf + tt][:, cols])
        ms.append(jnp.concatenate(rows, axis=0))
    y = _pool_project(ms, x, w_ref, sc_ref, g_ref, b_ref, alpha)
    for tt in range(s_len):
        out_ref[tt] = y[tt * bsz:(tt + 1) * bsz, :]


def _pool_sample(x, state, w, sc, g, b, *, alpha):
    return pl.pallas_call(
        functools.partial(_pool_sample_kernel, alpha=alpha),
        out_shape=jax.ShapeDtypeStruct(x.shape, F32),
        compiler_params=pltpu.CompilerParams(vmem_limit_bytes=VMEM_MB * 2**20),
        name="pool_sample",
    )(x, state, w, sc, g, b)


def kernel(x_prompt, x_sample, cache_k, cache_v, state_conv, state_lru, state_pool, page_table, meta_tokens, w_in, lam_q1, lam_k1, lam_q2, lam_k2, sub_norm_g, conv_w, conv_b, gate_a_w, gate_a_b, gate_x_w, gate_x_b, lru_lambda, w_out_ab, pool_w, pool_scale, mix_ln_g, mix_ln_b, w_ff1, w_ff2, ff_ln_g, ff_ln_b):
    n_prompt, seq, d = x_prompt.shape
    bsz, s_len, _ = x_sample.shape
    depth = w_ff1.shape[0]
    assert n_prompt == 1 and depth == 2
    n_meta = meta_tokens.shape[0]
    n_heads, hd2 = cache_k.shape[-2:]
    hd = hd2 // 2
    aw = n_heads * hd2
    bw = state_lru.shape[-1]
    assert aw == bw and w_in.shape[-1] == 3 * aw + 2 * bw
    page = cache_k.shape[2]
    n_buf = state_pool.shape[2]
    cw = conv_w.shape[1]
    assert s_len >= cw - 1
    alpha = (2.0 * depth) ** 0.25
    lam_init = 0.8 - 0.6 * math.exp(-0.3 * 0)

    t_len = n_meta + seq
    n_s = bsz * s_len
    rows = -(-(t_len + n_s) // ROW_ALIGN) * ROW_ALIGN
    xs_tm = x_sample.transpose(1, 0, 2).reshape(n_s, d)
    x0 = jnp.concatenate([meta_tokens, x_prompt[0], xs_tm,
                          jnp.zeros((rows - t_len - n_s, d), F32)], axis=0)

    row2 = lambda v: v.reshape(1, -1)
    lam_params = (row2(lam_q1[0]), row2(lam_k1[0]), row2(lam_q2[0]), row2(lam_k2[0]))
    sub_g = row2(sub_norm_g[0])

    assert rows - T_ATT < t_len
    z, qk, vb, k_rows, v_rows = _in_proj(x0, w_in[0].astype(BF16), t_len=t_len, tm=T_ATT,
                                         tn=TN_PROJ, n_heads=n_heads, hd=hd)
    zs = z[t_len:t_len + n_s]

    d_ff = w_ff1.shape[2]
    w1_all, w2_all = w_ff1.reshape(depth * d, d_ff), w_ff2.reshape(depth * d_ff, d)
    o, w1b0, w2b0, w_out_b = _attn_prompt(
        qk, vb, lam_params, sub_g, t=T_ATT, hd=hd, lam_init=lam_init,
        cast=((w1_all, 0, d), (w2_all, 0, d_ff), (w_out_ab[0], 0, d)))

    split = lambda c: zs[:, c * aw:(c + 1) * aw].reshape(s_len, bsz, n_heads, 2, hd)
    nr = 2 * n_heads * s_len
    q_s = split(0).transpose(1, 3, 2, 0, 4).reshape(bsz, nr, hd)
    kn = split(1).transpose(1, 0, 3, 2, 4)[:, :, :, :, None, :]
    kn = jnp.broadcast_to(kn, (bsz, s_len, 2, n_heads, s_len, hd)).reshape(bsz, s_len, nr, hd)
    vn = split(2).transpose(1, 0, 3, 2, 4)[:, :, :, None, :, None, :]
    vn = jnp.broadcast_to(vn, (bsz, s_len, 2, 2, n_heads, s_len, hd)).reshape(bsz, s_len, 2 * nr, hd)
    page_rows = lambda c: (c[0].reshape(-1, page, n_heads, 2, hd).transpose(0, 1, 3, 2, 4)
                           .reshape(-1, page * 2 * n_heads, hd))
    o_s = _attn_sample(page_table, q_s, kn, vn, page_rows(cache_k), page_rows(cache_v), lam_params,
                       sub_g, n_heads=n_heads, hd=hd, s_len=s_len, page=page,
                       gp=min(PAGES_PER_STEP, page_table.shape[1]), lam_init=lam_init)
    o_s = o_s.reshape(bsz, n_heads, s_len, hd2).transpose(1, 2, 0, 3).reshape(n_heads, n_s, hd2)
    o = lax.dynamic_update_slice(o, o_s.astype(BF16), (0, t_len, 0))

    gaw, gxw = gate_a_w[0].astype(BF16), gate_x_w[0].astype(BF16)
    lru_args = (conv_w[0], row2(conv_b[0]), gaw, row2(gate_a_b[0]), gxw, row2(gate_x_b[0]),
                row2(lru_lambda[0]))
    yb, h_last_p, w1b1, w2b1 = _lru_prompt(z, *lru_args, t_len=t_len, tm=TM_LRU,
                                           cast=((w1_all, d, d), (w2_all, d_ff, d_ff)))
    xb_s = zs[:, 3 * aw:3 * aw + bw].reshape(s_len, bsz, bw)
    gb_s = zs[:, 3 * aw + bw:].reshape(s_len, bsz, bw)
    yb_s, h_last_s = _lru_sample(xb_s, gb_s, state_conv[0].transpose(1, 0, 2), state_lru[0], *lru_args)
    yb = lax.dynamic_update_slice(yb, yb_s.reshape(n_s, bw), (t_len, 0))

    x1 = _out_proj(o, yb, x0, w_out_b, row2(mix_ln_g[0]), row2(mix_ln_b[0]), alpha=alpha, tm=TM_OUT)
    x2 = _mlp(x1, w1b0, w2b0, row2(ff_ln_g[0]), row2(ff_ln_b[0]), alpha=alpha, tm=TM_MLP, tf=TF_MLP)

    pool_args = (pool_w[0].astype(BF16), row2(pool_scale[0]), row2(mix_ln_g[1]), row2(mix_ln_b[1]))
    x3 = _pool_prompt(x2, *pool_args, alpha=alpha, tm=TM_POOL)
    x2_s = x2[t_len:t_len + n_s].reshape(s_len, bsz, d)
    x3_s = _pool_sample(x2_s, state_pool[0].transpose(1, 0, 2), *pool_args, alpha=alpha)
    x3 = lax.dynamic_update_slice(x3, x3_s.reshape(n_s, d), (t_len, 0))
    y_p, y_s = _mlp_final(x3, w1b1, w2b1, row2(ff_ln_g[1]), row2(ff_ln_b[1]), alpha=alpha, tm=TM_MLP,
                          tf=TF_MLP, n_meta=n_meta, seq=seq, n_s=n_s)

    to_bt = lambda v: v.reshape(s_len, bsz, -1).transpose(1, 0, 2)
    kv_p = lambda v: (v.reshape(t_len, 2, n_heads, hd).transpose(0, 2, 1, 3)
                      .reshape(1, 1, t_len, n_heads, hd2))
    kv_s = lambda c: to_bt(zs[:, c * aw:(c + 1) * aw]).reshape(1, bsz, s_len, n_heads, hd2)
    y_prompt = y_p[None]
    y_sample = to_bt(y_s)
    new_conv_prompt = z[t_len - (cw - 1):t_len, 3 * aw:3 * aw + bw][None, None]
    new_lru_prompt = h_last_p[None]
    new_pool_prompt = x2[t_len - n_buf:t_len][None, None]
    new_conv_sample = xb_s[s_len - (cw - 1):].transpose(1, 0, 2)[None]
    new_lru_sample = h_last_s[None]
    new_pool_sample = jnp.concatenate([state_pool[0], x2_s.transpose(1, 0, 2)], axis=1)[:, -n_buf:][None]
    return (y_prompt, y_sample, kv_p(k_rows), kv_p(v_rows), new_conv_prompt, new_lru_prompt, new_pool_prompt,
            kv_s(1), kv_s(2), new_conv_sample, new_lru_sample, new_pool_sample)
```

```python
import functools
import math

import jax
import jax.numpy as jnp
from jax import lax
from jax.experimental import pallas as pl
from jax.experimental.pallas import tpu as pltpu

F32 = jnp.float32
BF16 = jnp.bfloat16

LN_EPS = 1e-5
RMS_EPS = 1e-5
LRU_C = 8.0
POOL_WINDOWS = (2, 4, 8, 16)
NEG = -1e30
LOG2E = 1.4426950408889634
LANES = 128
N_AUG = 3

ROW_ALIGN = 768
T_ATT = 768
TN_PROJ = 1024
TM_LRU = 256
TM_OUT = 384
TM_MLP = 768
TF_MLP = 512
TM_POOL = 256
ATT_HEAD_UNROLL = 2
PAGES_PER_STEP = 16
VMEM_MB = 56
VMEM_MB_IN_PROJ = 60


def _params(sem, mb=VMEM_MB):
    return pltpu.CompilerParams(dimension_semantics=sem, vmem_limit_bytes=mb * 2**20)


def _layer_norm(y, g, b):
    mu = jnp.mean(y, axis=-1, keepdims=True)
    d = y - mu
    var = jnp.mean(d * d, axis=-1, keepdims=True)
    return d * lax.rsqrt(var + LN_EPS) * g + b


def _dot(a, b):
    return jnp.dot(a, b, preferred_element_type=F32)


def _dot_nt(a, b):
    return lax.dot_general(a, b, (((1,), (1,)), ((), ())), preferred_element_type=F32)


def _sigmoid(x):
    return 1.0 / (1.0 + jnp.exp(-x))


def _gelu_tanh(x):
    c = math.sqrt(2.0 / math.pi)
    return 0.5 * x * (1.0 + jnp.tanh(c * (x + 0.044715 * (x * x * x))))


def _lambda_value(lq1, lk1, lq2, lk2, lam_init):
    s1 = jnp.sum(lq1 * lk1, axis=-1, keepdims=True)
    s2 = jnp.sum(lq2 * lk2, axis=-1, keepdims=True)
    return jnp.exp(s1) - jnp.exp(s2) + lam_init


def _div_pow2(x, n):
    assert n & (n - 1) == 0
    return lax.shift_right_logical(x, n.bit_length() - 1)


def _mod_pow2(x, n):
    assert n & (n - 1) == 0
    return lax.bitwise_and(x, n - 1)


BF16_ROWS = 16


def _cast_specs(job, n_steps, step_of):
    mat, row_start, rows = job
    cols = mat.shape[1]
    units = rows // BF16_ROWS
    assert units * BF16_ROWS == rows
    n_blocks = max(k for k in range(1, n_steps + 1) if units % k == 0)
    br = rows // n_blocks
    assert row_start % br == 0
    first = row_start // br
    block = lambda *idx: jnp.minimum(step_of(*idx), n_blocks - 1)
    return (pl.BlockSpec((br, cols), lambda *idx: (first + block(*idx), 0)),
            pl.BlockSpec((br, cols), lambda *idx: (block(*idx), 0)))


def _cast_shape(job):
    mat, _, rows = job
    return jax.ShapeDtypeStruct((rows, mat.shape[1]), BF16)


def _cast_blocks(srcs, dsts):
    for src, dst in zip(srcs, dsts):
        dst[...] = src[...].astype(BF16)


def _head_slope(h, n_heads):
    return 2.0 ** (-8.0 * (h + 1) / n_heads)


def _in_proj_kernel(x_ref, w_ref, zt_ref, zb_ref, qk_ref, vb_ref, kr_ref, vr_ref, *, n_heads, hd):
    i = pl.program_id(0)
    j = pl.program_id(1)
    z = _dot(x_ref[...].astype(BF16), w_ref[...])
    tm = z.shape[0]

    @pl.when(j >= 3)
    def _():
        zb_ref[...] = z

    @pl.when(jnp.logical_and(i == pl.num_programs(0) - 1, j < 3))
    def _():
        zt_ref[...] = z

    lane = lax.broadcasted_iota(jnp.int32, (tm, hd), 1)

    def cache_rows(dst_ref):
        for c in range(2):
            for h in range(n_heads):
                col = (2 * h + c) * hd
                dst_ref[pl.ds(c * n_heads + h, tm, stride=2 * n_heads), :] = z[:, col:col + hd]

    @pl.when(j == 0)
    def _():
        ones = jnp.where(lane < N_AUG, 1.0, 0.0).astype(BF16)
        q_scale = hd ** -0.5 * LOG2E
        for h in range(n_heads):
            for c in range(2):
                col = (2 * h + c) * hd
                qk_ref[0, h, :, 2 * hd * c:2 * hd * c + hd] = (z[:, col:col + hd] * q_scale).astype(BF16)
                qk_ref[0, h, :, 2 * hd * c + hd:2 * hd * (c + 1)] = ones

    @pl.when(j == 1)
    def _():
        row = lax.broadcasted_iota(jnp.int32, (tm, hd), 0).astype(F32)
        for h in range(n_heads):
            a = row * (_head_slope(h, n_heads) * LOG2E)
            hi = a.astype(BF16).astype(F32)
            mid = (a - hi).astype(BF16).astype(F32)
            lo = a - hi - mid
            aug = jnp.where(lane == 0, hi, jnp.where(lane == 1, mid, jnp.where(lane == 2, lo, 0.0)))
            aug = aug.astype(BF16)
            for c in range(2):
                col = (2 * h + c) * hd
                qk_ref[0, h, :, 2 * hd * c:2 * hd * c + hd] = z[:, col:col + hd].astype(BF16)
                qk_ref[0, h, :, 2 * hd * c + hd:2 * hd * (c + 1)] = aug
        cache_rows(kr_ref)

    @pl.when(j == 2)
    def _():
        for h in range(n_heads):
            vb_ref[h] = z[:, 2 * hd * h:2 * hd * (h + 1)].astype(BF16)
        cache_rows(vr_ref)


def _in_proj(x, w, *, t_len, tm, tn, n_heads, hd):
    r, d = x.shape
    n = w.shape[1]
    assert hd == LANES and tn == 2 * hd * n_heads and n == 5 * tn
    kr = 2 * n_heads
    n_i = r // tm
    cache_blk = pl.BlockSpec((tm * kr, hd), lambda i, j: (i, 0))
    cache_shape = jax.ShapeDtypeStruct((t_len * kr, hd), F32)
    return pl.pallas_call(
        functools.partial(_in_proj_kernel, n_heads=n_heads, hd=hd),
        grid=(n_i, n // tn),
        in_specs=[pl.BlockSpec((tm, d), lambda i, j: (i, 0)),
                  pl.BlockSpec((d, tn), lambda i, j: (0, j))],
        out_specs=[
                   pl.BlockSpec((tm, tn), lambda i, j: (0, jnp.where(i == n_i - 1, jnp.minimum(j, 2), 0))),
                   pl.BlockSpec((tm, tn), lambda i, j: (i, jnp.maximum(j - 3, 0))),
                   pl.BlockSpec((1, n_heads, tm, 4 * hd), lambda i, j: (jnp.minimum(j, 1), 0, i, 0)),
                   pl.BlockSpec((n_heads, tm, 2 * hd), lambda i, j: (0, i, 0)),
                   cache_blk, cache_blk],
        out_shape=[jax.ShapeDtypeStruct((tm, 3 * tn), F32),
                   jax.ShapeDtypeStruct((r, 2 * tn), F32),
                   jax.ShapeDtypeStruct((2, n_heads, r, 4 * hd), BF16),
                   jax.ShapeDtypeStruct((n_heads, r, 2 * hd), BF16),
                   cache_shape, cache_shape],
        compiler_params=_params(("arbitrary", "arbitrary"), mb=VMEM_MB_IN_PROJ),
        name="in_proj",
    )(x, w)


def _attn_prompt_kernel(qi_ref, kj_ref, qa_ref, ka_ref, vb_ref, sl_ref, lq1_ref, lk1_ref, lq2_ref,
                        lk2_ref, g_ref, *rest, t, n_heads, hd, lam_init, n_cast):
    cast_src = rest[:n_cast]
    o_ref = rest[n_cast]
    cast_dst = rest[n_cast + 1:2 * n_cast + 1]
    m_sc, l_sc, acc_sc = rest[2 * n_cast + 1:]
    _cast_blocks(cast_src, cast_dst)
    p = pl.program_id(0)
    i = qi_ref[p]
    j = kj_ref[p]

    @pl.when(j == 0)
    def _():
        m_sc[...] = jnp.full_like(m_sc, NEG)
        l_sc[...] = jnp.zeros_like(l_sc)
        acc_sc[...] = jnp.zeros_like(acc_sc)

    def tile(masked):
        if masked:
            keep = (lax.broadcasted_iota(jnp.int32, (t, t), 1)
                    <= lax.broadcasted_iota(jnp.int32, (t, t), 0))
        tile_dist = ((j - i) * t).astype(F32)

        def head(h, carry):
            shift = sl_ref[h][:, 0:1] * tile_dist
            for c in range(2):
                s = _dot_nt(qa_ref[h, :, 2 * hd * c:2 * hd * (c + 1)],
                            ka_ref[h, :, 2 * hd * c:2 * hd * (c + 1)])
                if masked:
                    s = jnp.where(keep, s, NEG)
                idx = 2 * h + c
                m_old = m_sc[idx]
                m_new = jnp.maximum(m_old, jnp.max(s, axis=1, keepdims=True) + shift)
                alpha = jnp.exp2(m_old - m_new)
                pr = jnp.exp2(s - (m_new - shift))
                l_sc[idx] = alpha * l_sc[idx] + jnp.sum(pr, axis=1, keepdims=True)
                acc_sc[idx] = alpha * acc_sc[idx] + _dot(pr.astype(BF16), vb_ref[h])
                m_sc[idx] = m_new
            return carry

        lax.fori_loop(0, n_heads, head, 0, unroll=ATT_HEAD_UNROLL)

    @pl.when(j < i)
    def _():
        tile(False)

    @pl.when(j == i)
    def _():
        tile(True)
        lam = _lambda_value(lq1_ref[...], lk1_ref[...], lq2_ref[...], lk2_ref[...], lam_init)
        g = g_ref[...]
        for h in range(n_heads):
            o1 = acc_sc[2 * h] / l_sc[2 * h]
            o2 = acc_sc[2 * h + 1] / l_sc[2 * h + 1]
            oh = o1 - lam * o2
            oh = oh * lax.rsqrt(jnp.mean(oh * oh, axis=-1, keepdims=True) + RMS_EPS)
            o_ref[h] = (oh * g * (1.0 - lam_init)).astype(o_ref.dtype)


def _attn_prompt(qk, vb, lam_params, sub_g, *, t, hd, lam_init, cast=()):
    _, n_heads, r, _ = qk.shape
    nq = r // t
    pairs = [(i, j) for i in range(nq) for j in range(i + 1)]
    qi = jnp.asarray([i for i, _ in pairs], jnp.int32)
    kj = jnp.asarray([j for _, j in pairs], jnp.int32)
    slopes = jnp.asarray([[[_head_slope(h, n_heads) * LOG2E] * LANES] for h in range(n_heads)], F32)
    vec = lambda n: pl.BlockSpec((1, n), lambda p, qi, kj: (0, 0))
    cast_specs = [_cast_specs(job, len(pairs), lambda p, qi, kj: p) for job in cast]
    grid_spec = pltpu.PrefetchScalarGridSpec(
        num_scalar_prefetch=2,
        grid=(len(pairs),),
        in_specs=[pl.BlockSpec((None, n_heads, t, 4 * hd), lambda p, qi, kj: (0, 0, qi[p], 0)),
                  pl.BlockSpec((None, n_heads, t, 4 * hd), lambda p, qi, kj: (1, 0, kj[p], 0)),
                  pl.BlockSpec((n_heads, t, 2 * hd), lambda p, qi, kj: (0, kj[p], 0)),
                  pl.BlockSpec((n_heads, 1, LANES), lambda p, qi, kj: (0, 0, 0)),
                  vec(hd), vec(hd), vec(hd), vec(hd), vec(2 * hd)] + [s[0] for s in cast_specs],
        out_specs=[pl.BlockSpec((n_heads, t, 2 * hd), lambda p, qi, kj: (0, qi[p], 0))]
                  + [s[1] for s in cast_specs],
        scratch_shapes=[pltpu.VMEM((2 * n_heads, t, 1), F32),
                        pltpu.VMEM((2 * n_heads, t, 1), F32),
                        pltpu.VMEM((2 * n_heads, t, 2 * hd), F32)],
    )
    kern = functools.partial(_attn_prompt_kernel, t=t, n_heads=n_heads, hd=hd, lam_init=lam_init,
                             n_cast=len(cast))
    return pl.pallas_call(
        kern, grid_spec=grid_spec,
        out_shape=[jax.ShapeDtypeStruct((n_heads, r, 2 * hd), BF16)] + [_cast_shape(job) for job in cast],
        compiler_params=_params(("arbitrary",)),
        name="attn_prompt",
    )(qi, kj, qk, qk, vb, slopes, *lam_params, sub_g, *[job[0] for job in cast])


def _attn_sample_kernel(pt_ref, q_ref, kn_ref, vn_ref, *rest, n_heads, hd, s_len, past_len, page,
                        gp, lam_init):
    del pt_ref
    page_refs = rest[:2 * gp]
    (lq1_ref, lk1_ref, lq2_ref, lk2_ref, g_ref, o_ref,
     qb_sc, bias_sc, m_sc, l_sc, acc_sc) = rest[2 * gp:]
    p = pl.program_id(1)
    nr = 2 * n_heads * s_len
    kr = 2 * n_heads
    ncol = page * kr
    half = nr // 2
    scale = hd ** -0.5

    def row_info(shape):
        row = lax.broadcasted_iota(jnp.int32, shape, 0)
        tok = _mod_pow2(row, s_len)
        head = _mod_pow2(_div_pow2(row, s_len), n_heads)
        cmap = _div_pow2(row, s_len * n_heads)
        slope = jnp.zeros(shape, F32)
        for h in range(n_heads):
            slope = jnp.where(head == h, _head_slope(h, n_heads), slope)
        return tok, head, cmap, slope

    @pl.when(p == 0)
    def _():
        qb_sc[...] = (q_ref[0] * scale).astype(BF16)
        tok, head, cmap, slope = row_info((nr, ncol))
        col = lax.broadcasted_iota(jnp.int32, (nr, ncol), 1)
        key = _div_pow2(col, kr)
        own = _mod_pow2(col, kr) == cmap * n_heads + head
        bias_sc[...] = jnp.where(own, -slope * (past_len + tok - key).astype(F32), NEG)
        m_sc[...] = jnp.full_like(m_sc, NEG)
        l_sc[...] = jnp.zeros_like(l_sc)
        acc_sc[...] = jnp.zeros_like(acc_sc)

    tok1, _, _, slope1 = row_info((nr, 1))
    m = m_sc[...]
    l = l_sc[...]
    acc = acc_sc[...]
    scores, shifts = [], []
    m_new = m
    for gi in range(gp):
        kp = page_refs[gi][0].astype(BF16)
        s = _dot_nt(qb_sc[...], kp) + bias_sc[...]
        shift = slope1 * ((p * gp + gi) * page).astype(F32)
        m_new = jnp.maximum(m_new, jnp.max(s, axis=1, keepdims=True) + shift)
        scores.append(s)
        shifts.append(shift)
    alpha = jnp.exp(m - m_new)
    l = alpha * l
    pv = jnp.zeros((2 * nr, hd), F32)
    for gi in range(gp):
        pr = jnp.exp(scores[gi] - (m_new - shifts[gi]))
        l = l + jnp.sum(pr, axis=1, keepdims=True)
        top, bot = pr[:half], pr[half:]
        chunks = range(ncol // LANES)
        top_r = jnp.concatenate(
            [pltpu.roll(top[:, LANES * cc:LANES * (cc + 1)], n_heads, 1) for cc in chunks], axis=1)
        bot_r = jnp.concatenate(
            [pltpu.roll(bot[:, LANES * cc:LANES * (cc + 1)], LANES - n_heads, 1) for cc in chunks], axis=1)
        lhs = jnp.concatenate([top, bot_r, top_r, bot], axis=0).astype(BF16)
        vp = page_refs[gp + gi][0].astype(BF16)
        pv = pv + _dot(lhs, vp)
    acc = jnp.concatenate([alpha, alpha], axis=0) * acc + pv
    m = m_new
    m_sc[...] = m
    l_sc[...] = l
    acc_sc[...] = acc

    @pl.when(p == pl.num_programs(1) - 1)
    def _():
        qf = q_ref[0] * scale
        s_new = []
        for t2 in range(s_len):
            st = jnp.sum(qf * kn_ref[0, t2], axis=1, keepdims=True)
            st = st - slope1 * (tok1 - t2).astype(F32)
            s_new.append(jnp.where(t2 <= tok1, st, NEG))
        m_fin = m
        for st in s_new:
            m_fin = jnp.maximum(m_fin, st)
        a_fin = jnp.exp(m - m_fin)
        l_fin = a_fin * l
        acc_f = jnp.concatenate([a_fin, a_fin], axis=0) * acc
        for t2 in range(s_len):
            pt = jnp.exp(s_new[t2] - m_fin)
            l_fin = l_fin + pt
            acc_f = acc_f + jnp.concatenate([pt, pt], axis=0) * vn_ref[0, t2]
        lam = _lambda_value(lq1_ref[...], lk1_ref[...], lq2_ref[...], lk2_ref[...], lam_init)
        halves = []
        for c2 in range(2):
            a = acc_f[c2 * nr:(c2 + 1) * nr] / l_fin
            halves.append(a[:half] - lam * a[half:])
        ms = (jnp.sum(halves[0] * halves[0], axis=1, keepdims=True)
              + jnp.sum(halves[1] * halves[1], axis=1, keepdims=True)) / (2.0 * hd)
        rs = lax.rsqrt(ms + RMS_EPS) * (1.0 - lam_init)
        for c2 in range(2):
            o_ref[0, :, c2 * hd:(c2 + 1) * hd] = halves[c2] * rs * g_ref[:, c2 * hd:(c2 + 1) * hd]


def _attn_sample(page_table, q, kn, vn, ck, cv, lam_params, sub_g, *, n_heads, hd, s_len, page,
                 gp, lam_init):
    bsz, nr, _ = q.shape
    n_pages = page_table.shape[1]
    assert n_pages % gp == 0 and hd == LANES and nr == 2 * n_heads * s_len
    ncol = ck.shape[1]
    pg = lambda gi: pl.BlockSpec((1, ncol, hd), lambda b, p, pt: (pt[b, p * gp + gi], 0, 0))
    vec = lambda n: pl.BlockSpec((1, n), lambda b, p, pt: (0, 0))
    grid_spec = pltpu.PrefetchScalarGridSpec(
        num_scalar_prefetch=1,
        grid=(bsz, n_pages // gp),
        in_specs=[pl.BlockSpec((1, nr, hd), lambda b, p, pt: (b, 0, 0)),
                  pl.BlockSpec((1, s_len, nr, hd), lambda b, p, pt: (b, 0, 0, 0)),
                  pl.BlockSpec((1, s_len, 2 * nr, hd), lambda b, p, pt: (b, 0, 0, 0))]
                 + [pg(gi) for gi in range(gp)] + [pg(gi) for gi in range(gp)]
                 + [vec(hd), vec(hd), vec(hd), vec(hd), vec(2 * hd)],
        out_specs=pl.BlockSpec((1, nr // 2, 2 * hd), lambda b, p, pt: (b, 0, 0)),
        scratch_shapes=[pltpu.VMEM((nr, hd), BF16),
                        pltpu.VMEM((nr, ncol), F32),
                        pltpu.VMEM((nr, 1), F32),
                        pltpu.VMEM((nr, 1), F32),
                        pltpu.VMEM((2 * nr, hd), F32)],
    )
    kern = functools.partial(_attn_sample_kernel, n_heads=n_heads, hd=hd, s_len=s_len,
                             past_len=n_pages * page, page=page, gp=gp, lam_init=lam_init)
    return pl.pallas_call(
        kern, grid_spec=grid_spec,
        out_shape=jax.ShapeDtypeStruct((bsz, nr // 2, 2 * hd), F32),
        compiler_params=_params(("arbitrary", "arbitrary")),
        name="attn_sample",
    )(page_table, q, kn, vn, *([ck] * gp), *([cv] * gp), *lam_params, sub_g)


def _lru_gates(xc, gaw_ref, gab_ref, gxw_ref, gxb_ref, lam_ref):
    n_blk, bs, _ = gaw_ref.shape
    xcb = xc.astype(BF16)
    ra = jnp.concatenate([_dot(xcb[:, n * bs:(n + 1) * bs], gaw_ref[n]) for n in range(n_blk)], axis=1)
    rx = jnp.concatenate([_dot(xcb[:, n * bs:(n + 1) * bs], gxw_ref[n]) for n in range(n_blk)], axis=1)
    r = _sigmoid(ra + gab_ref[...])
    ig = _sigmoid(rx + gxb_ref[...])
    neg_lam = -lam_ref[...]
    softplus = jnp.maximum(neg_lam, 0.0) + jnp.log1p(jnp.exp(-jnp.abs(neg_lam)))
    log_a = -LRU_C * r * softplus
    a = jnp.exp(log_a)
    mult = jnp.sqrt(-jnp.tanh(log_a) * (a * a + 1.0))
    return a, ig, mult


def _lru_prompt_kernel(xb_ref, gb_ref, cw_ref, cb_ref, gaw_ref, gab_ref, gxw_ref, gxb_ref, lam_ref,
                       *rest, tm, conv_w, last_tile, last_row, n_cast):
    cast_src = rest[:n_cast]
    yb_ref, hl_ref = rest[n_cast:n_cast + 2]
    cast_dst = rest[n_cast + 2:2 * n_cast + 2]
    xbuf, hbuf, hcar = rest[2 * n_cast + 2:]
    _cast_blocks(cast_src, cast_dst)
    i = pl.program_id(0)
    w = xb_ref.shape[1]
    n_grp = tm // 8

    @pl.when(i == 0)
    def _():
        xbuf[0:8, :] = jnp.zeros((8, w), F32)
        hcar[...] = jnp.zeros_like(hcar)

    xbuf[8:8 + tm, :] = xb_ref[...]
    xc = cb_ref[...] + cw_ref[conv_w - 1:conv_w, :] * xb_ref[...]
    for jj in range(conv_w - 1):
        start = 8 - (conv_w - 1) + jj
        xc = xc + cw_ref[jj:jj + 1, :] * xbuf[start:start + tm, :]
    xbuf[0:8, :] = xbuf[tm:tm + 8, :]

    a, ig, mult = _lru_gates(xc, gaw_ref, gab_ref, gxw_ref, gxb_ref, lam_ref)
    grow = i * tm + lax.broadcasted_iota(jnp.int32, (tm, 1), 0)
    mult = jnp.where(grow == 0, 1.0, mult)
    u = xc * ig * mult

    a3 = a.reshape(n_grp, 8, w)
    u3 = u.reshape(n_grp, 8, w)
    sub = lax.broadcasted_iota(jnp.int32, (n_grp, 8, w), 1)
    for sh in (1, 2, 4):
        a_prev = pltpu.roll(a3, sh, axis=1)
        u_prev = pltpu.roll(u3, sh, axis=1)
        ok = sub >= sh
        u3 = jnp.where(ok, a3 * u_prev + u3, u3)
        a3 = jnp.where(ok, a3 * a_prev, a3)
    h_prev = hcar[...]
    for gi in range(n_grp):
        hg = a3[gi] * h_prev + u3[gi]
        hbuf[8 * gi:8 * (gi + 1), :] = hg
        h_prev = hg[7:8, :]
    hcar[...] = h_prev

    yb_ref[...] = (hbuf[...] * _gelu_tanh(gb_ref[...])).astype(yb_ref.dtype)

    @pl.when(i == last_tile)
    def _():
        hl_ref[...] = hbuf[last_row:last_row + 1, :]


def _lru_prompt(z, conv_w, conv_b, gaw, gab, gxw, gxb, lru_lambda, *, t_len, tm, cast=()):
    r = z.shape[0]
    w = conv_b.shape[1]
    cw = conv_w.shape[0]
    n_blk, bs, _ = gaw.shape
    n_steps = r // tm
    full = lambda shape: pl.BlockSpec(shape, lambda i: (0,) * len(shape))
    cast_specs = [_cast_specs(job, n_steps, lambda i: i) for job in cast]
    kern = functools.partial(_lru_prompt_kernel, tm=tm, conv_w=cw, n_cast=len(cast),
                             last_tile=(t_len - 1) // tm, last_row=(t_len - 1) % tm)
    return pl.pallas_call(
        kern,
        grid=(n_steps,),
        in_specs=[pl.BlockSpec((tm, w), lambda i: (i, 0)),
                  pl.BlockSpec((tm, w), lambda i: (i, 1)),
                  full((cw, w)), full((1, w)),
                  full((n_blk, bs, bs)), full((1, w)),
                  full((n_blk, bs, bs)), full((1, w)), full((1, w))] + [s[0] for s in cast_specs],
        out_specs=[pl.BlockSpec((tm, w), lambda i: (i, 0)), full((1, w))] + [s[1] for s in cast_specs],
        out_shape=[jax.ShapeDtypeStruct((r, w), BF16), jax.ShapeDtypeStruct((1, w), F32)]
                  + [_cast_shape(job) for job in cast],
        scratch_shapes=[pltpu.VMEM((tm + 8, w), F32), pltpu.VMEM((tm, w), F32),
                        pltpu.VMEM((1, w), F32)],
        compiler_params=_params(("arbitrary",)),
        name="lru_prompt",
    )(z, z, conv_w, conv_b, gaw, gab, gxw, gxb, lru_lambda, *[job[0] for job in cast])


def _lru_sample_kernel(xb_ref, gb_ref, cs_ref, h0_ref, cw_ref, cb_ref, gaw_ref, gab_ref, gxw_ref,
                       gxb_ref, lam_ref, yb_ref, hl_ref, *, conv_w):
    s_len, bsz, w = xb_ref.shape
    xp = [cs_ref[jj] for jj in range(conv_w - 1)] + [xb_ref[tt] for tt in range(s_len)]
    xcs = []
    for tt in range(s_len):
        xc = cb_ref[...] + cw_ref[0:1, :] * xp[tt]
        for jj in range(1, conv_w):
            xc = xc + cw_ref[jj:jj + 1, :] * xp[tt + jj]
        xcs.append(xc)
    xc = jnp.concatenate(xcs, axis=0)
    a, ig, mult = _lru_gates(xc, gaw_ref, gab_ref, gxw_ref, gxb_ref, lam_ref)
    u = xc * ig * mult
    h = h0_ref[...]
    for tt in range(s_len):
        rows = slice(tt * bsz, (tt + 1) * bsz)
        h = a[rows, :] * h + u[rows, :]
        yb_ref[tt] = (h * _gelu_tanh(gb_ref[tt])).astype(yb_ref.dtype)
    hl_ref[...] = h


def _lru_sample(xb, gb, conv_state, h0, conv_w, conv_b, gaw, gab, gxw, gxb, lru_lambda):
    s_len, bsz, w = xb.shape
    kern = functools.partial(_lru_sample_kernel, conv_w=conv_w.shape[0])
    return pl.pallas_call(
        kern,
        out_shape=[jax.ShapeDtypeStruct((s_len, bsz, w), BF16), jax.ShapeDtypeStruct((bsz, w), F32)],
        compiler_params=pltpu.CompilerParams(vmem_limit_bytes=VMEM_MB * 2**20),
        name="lru_sample",
    )(xb, gb, conv_state, h0, conv_w, conv_b, gaw, gab, gxw, gxb, lru_lambda)


def _out_proj_kernel(o_ref, yb_ref, x_ref, w_ref, g_ref, b_ref, out_ref, *, alpha):
    n_heads, _, hw = o_ref.shape
    aw = n_heads * hw
    mixed = _dot(yb_ref[...], w_ref[aw:, :])
    for h in range(n_heads):
        mixed = mixed + _dot(o_ref[h], w_ref[hw * h:hw * (h + 1), :])
    out_ref[...] = _layer_norm(alpha * x_ref[...] + mixed, g_ref[...], b_ref[...])


def _out_proj(o, yb, x, w, g, b, *, alpha, tm):
    r, d = x.shape
    n_heads, _, hw = o.shape
    bw = yb.shape[1]
    full = lambda shape: pl.BlockSpec(shape, lambda i: (0,) * len(shape))
    return pl.pallas_call(
        functools.partial(_out_proj_kernel, alpha=alpha),
        grid=(r // tm,),
        in_specs=[pl.BlockSpec((n_heads, tm, hw), lambda i: (0, i, 0)),
                  pl.BlockSpec((tm, bw), lambda i: (i, 0)),
                  pl.BlockSpec((tm, d), lambda i: (i, 0)),
                  full((n_heads * hw + bw, d)), full((1, d)), full((1, d))],
        out_specs=pl.BlockSpec((tm, d), lambda i: (i, 0)),
        out_shape=jax.ShapeDtypeStruct((r, d), F32),
        compiler_params=_params(("parallel",)),
        name="out_proj",
    )(o, yb, x, w, g, b)


def _mlp_kernel(x_ref, w1_ref, w2_ref, g_ref, b_ref, out_ref, xb_sc, acc_sc, *, alpha):
    f = pl.program_id(1)

    @pl.when(f == 0)
    def _():
        xb_sc[...] = x_ref[...].astype(BF16)
        acc_sc[...] = jnp.zeros_like(acc_sc)

    h = jnp.maximum(_dot(xb_sc[...], w1_ref[...]), 0.0)
    acc_sc[...] += _dot((h * h).astype(BF16), w2_ref[...])

    @pl.when(f == pl.num_programs(1) - 1)
    def _():
        out_ref[...] = _layer_norm(alpha * x_ref[...] + acc_sc[...], g_ref[...], b_ref[...])


def _mlp_final_kernel(x_ref, w1_ref, w2_ref, g_ref, b_ref, yp_ref, ys_ref, xb_sc, acc_sc, *, alpha,
                      n_meta, n_tail, s_start, n_s):
    i = pl.program_id(0)
    f = pl.program_id(1)
    last = pl.num_programs(0) - 1

    @pl.when(f == 0)
    def _():
        xb_sc[...] = x_ref[...].astype(BF16)
        acc_sc[...] = jnp.zeros_like(acc_sc)

    h = jnp.maximum(_dot(xb_sc[...], w1_ref[...]), 0.0)
    acc_sc[...] += _dot((h * h).astype(BF16), w2_ref[...])

    def result():
        return _layer_norm(alpha * x_ref[...] + acc_sc[...], g_ref[...], b_ref[...])

    @pl.when(jnp.logical_and(f == pl.num_programs(1) - 1, i < last))
    def _():
        yp_ref[...] = result()

    @pl.when(jnp.logical_and(f == pl.num_programs(1) - 1, i == last))
    def _():
        res = result()
        yp_ref[0:n_tail, :] = res[n_meta:n_meta + n_tail, :]
        ys_ref[...] = res[s_start:s_start + n_s, :]


def _mlp(x, w1, w2, g, b, *, alpha, tm, tf):
    r, d = x.shape
    ff = w1.shape[1]
    vec = pl.BlockSpec((1, d), lambda i, f: (0, 0))
    return pl.pallas_call(
        functools.partial(_mlp_kernel, alpha=alpha),
        grid=(r // tm, ff // tf),
        in_specs=[pl.BlockSpec((tm, d), lambda i, f: (i, 0)),
                  pl.BlockSpec((d, tf), lambda i, f: (0, f)),
                  pl.BlockSpec((tf, d), lambda i, f: (f, 0)),
                  vec, vec],
        out_specs=pl.BlockSpec((tm, d), lambda i, f: (i, 0)),
        out_shape=jax.ShapeDtypeStruct((r, d), F32),
        scratch_shapes=[pltpu.VMEM((tm, d), BF16), pltpu.VMEM((tm, d), F32)],
        compiler_params=_params(("parallel", "arbitrary")),
        name="mlp",
    )(x, w1, w2, g, b)


def _mlp_final(x, w1, w2, g, b, *, alpha, tm, tf, n_meta, seq, n_s):
    r, d = x.shape
    ff = w1.shape[1]
    n_i = r // tm
    last_row = (n_i - 1) * tm
    n_tail = seq - last_row
    s_start = n_meta + seq - last_row
    assert pl.cdiv(seq, tm) == n_i and 0 < n_tail <= tm - n_meta and s_start + n_s <= tm
    assert n_meta % 8 == 0
    vec = pl.BlockSpec((1, d), lambda i, f: (0, 0))
    x_rows = lambda i, f: (pl.multiple_of(jnp.where(i < n_i - 1, n_meta + tm * i, last_row), 8), 0)
    kern = functools.partial(_mlp_final_kernel, alpha=alpha, n_meta=n_meta, n_tail=n_tail,
                             s_start=s_start, n_s=n_s)
    return pl.pallas_call(
        kern,
        grid=(n_i, ff // tf),
        in_specs=[pl.BlockSpec((pl.Element(tm), pl.Element(d)), x_rows),
                  pl.BlockSpec((d, tf), lambda i, f: (0, f)),
                  pl.BlockSpec((tf, d), lambda i, f: (f, 0)),
                  vec, vec],
        out_specs=[pl.BlockSpec((tm, d), lambda i, f: (i, 0)),
                   pl.BlockSpec((n_s, d), lambda i, f: (0, 0))],
        out_shape=[jax.ShapeDtypeStruct((seq, d), F32), jax.ShapeDtypeStruct((n_s, d), F32)],
        scratch_shapes=[pltpu.VMEM((tm, d), BF16), pltpu.VMEM((tm, d), F32)],
        compiler_params=_params(("arbitrary", "arbitrary")),
        name="mlp_final",
    )(x, w1, w2, g, b)


def _pool_project(ms, x, w_ref, sc_ref, g_ref, b_ref, alpha):
    y = jnp.concatenate([_dot(ms[gi].astype(BF16), w_ref[gi]) for gi in range(len(ms))], axis=1)
    return _layer_norm(alpha * x + y * sc_ref[...], g_ref[...], b_ref[...])


def _pool_prompt_kernel(x_ref, w_ref, sc_ref, g_ref, b_ref, out_ref, *bufs, tm, alpha, pad):
    i = pl.program_id(0)
    n_g = len(POOL_WINDOWS)
    gw = x_ref.shape[1] // n_g

    @pl.when(i == 0)
    def _():
        for buf in bufs:
            buf[0:pad, :] = jnp.zeros((pad, buf.shape[1]), F32)

    x = x_ref[...]
    pos = i * tm + lax.broadcasted_iota(jnp.int32, (tm, 1), 0)
    ms = []
    cur = x
    for k, win in enumerate(POOL_WINDOWS):
        half = win // 2
        buf = bufs[k]
        buf[pad:pad + tm, :] = cur
        ws = cur + buf[pad - half:pad - half + tm, :]
        buf[0:pad, :] = buf[tm:tm + pad, :]
        cnt = jnp.minimum(win, pos + 1).astype(F32)
        ms.append(ws[:, :gw] / cnt - x[:, k * gw:(k + 1) * gw])
        if k + 1 < n_g:
            cur = ws[:, gw:]
    out_ref[...] = _pool_project(ms, x, w_ref, sc_ref, g_ref, b_ref, alpha)


def _pool_prompt(x, w, sc, g, b, *, alpha, tm):
    r, d = x.shape
    n_g, gw, _ = w.shape
    pad = 16
    assert POOL_WINDOWS == tuple(2 ** (k + 1) for k in range(n_g)) and max(POOL_WINDOWS) // 2 <= pad
    full = lambda shape: pl.BlockSpec(shape, lambda i: (0,) * len(shape))
    return pl.pallas_call(
        functools.partial(_pool_prompt_kernel, tm=tm, alpha=alpha, pad=pad),
        grid=(r // tm,),
        in_specs=[pl.BlockSpec((tm, d), lambda i: (i, 0)),
                  full((n_g, gw, gw)), full((1, d)), full((1, d)), full((1, d))],
        out_specs=pl.BlockSpec((tm, d), lambda i: (i, 0)),
        out_shape=jax.ShapeDtypeStruct((r, d), F32),
        scratch_shapes=[pltpu.VMEM((tm + pad, (n_g - k) * gw), F32) for k in range(n_g)],
        compiler_params=_params(("arbitrary",)),
        name="pool_prompt",
    )(x, w, sc, g, b)


def _pool_sample_kernel(x_ref, st_ref, w_ref, sc_ref, g_ref, b_ref, out_ref, *, alpha):
    s_len, bsz, d = x_ref.shape
    n_buf = st_ref.shape[0]
    gw = d // len(POOL_WINDOWS)
    ext = [st_ref[jj] for jj in range(n_buf)] + [x_ref[tt] for tt in range(s_len)]
    x = jnp.concatenate(ext[n_buf:], axis=0)
    ms = []
    for gi, win in enumerate(POOL_WINDOWS):
        cols = slice(gi * gw, (gi + 1) * gw)
        rows = []
        for tt in range(s_len):
            ws = ext[n_buf + tt][:, cols]
            for kk in range(1, win):
                ws = ws + ext[n_buf + tt - kk][:, cols]
            rows.append(ws / float(win) - ext[n_buf + tt][:, cols])
        ms.append(jnp.concatenate(rows, axis=0))
    y = _pool_project(ms, x, w_ref, sc_ref, g_ref, b_ref, alpha)
    for tt in range(s_len):
        out_ref[tt] = y[tt * bsz:(tt + 1) * bsz, :]


def _pool_sample(x, state, w, sc, g, b, *, alpha):
    return pl.pallas_call(
        functools.partial(_pool_sample_kernel, alpha=alpha),
        out_shape=jax.ShapeDtypeStruct(x.shape, F32),
        compiler_params=pltpu.CompilerParams(vmem_limit_bytes=VMEM_MB * 2**20),
        name="pool_sample",
    )(x, state, w, sc, g, b)


def kernel(x_prompt, x_sample, cache_k, cache_v, state_conv, state_lru, state_pool, page_table, meta_tokens, w_in, lam_q1, lam_k1, lam_q2, lam_k2, sub_norm_g, conv_w, conv_b, gate_a_w, gate_a_b, gate_x_w, gate_x_b, lru_lambda, w_out_ab, pool_w, pool_scale, mix_ln_g, mix_ln_b, w_ff1, w_ff2, ff_ln_g, ff_ln_b):
    n_prompt, seq, d = x_prompt.shape
    bsz, s_len, _ = x_sample.shape
    depth = w_ff1.shape[0]
    assert n_prompt == 1 and depth == 2
    n_meta = meta_tokens.shape[0]
    n_heads, hd2 = cache_k.shape[-2:]
    hd = hd2 // 2
    aw = n_heads * hd2
    bw = state_lru.shape[-1]
    assert aw == bw and w_in.shape[-1] == 3 * aw + 2 * bw
    page = cache_k.shape[2]
    n_buf = state_pool.shape[2]
    cw = conv_w.shape[1]
    assert s_len >= cw - 1
    alpha = (2.0 * depth) ** 0.25
    lam_init = 0.8 - 0.6 * math.exp(-0.3 * 0)

    t_len = n_meta + seq
    n_s = bsz * s_len
    rows = -(-(t_len + n_s) // ROW_ALIGN) * ROW_ALIGN
    xs_tm = x_sample.transpose(1, 0, 2).reshape(n_s, d)
    x0 = jnp.concatenate([meta_tokens, x_prompt[0], xs_tm,
                          jnp.zeros((rows - t_len - n_s, d), F32)], axis=0)

    row2 = lambda v: v.reshape(1, -1)
    lam_params = (row2(lam_q1[0]), row2(lam_k1[0]), row2(lam_q2[0]), row2(lam_k2[0]))
    sub_g = row2(sub_norm_g[0])

    assert rows - T_ATT < t_len
    zt, zb, qk, vb, k_rows, v_rows = _in_proj(x0, w_in[0].astype(BF16), t_len=t_len, tm=T_ATT,
                                              tn=TN_PROJ, n_heads=n_heads, hd=hd)
    s0 = t_len - (rows - T_ATT)
    assert s0 + n_s <= T_ATT
    zs = jnp.concatenate([zt[s0:s0 + n_s], zb[t_len:t_len + n_s]], axis=1)

    d_ff = w_ff1.shape[2]
    w1_all, w2_all = w_ff1.reshape(depth * d, d_ff), w_ff2.reshape(depth * d_ff, d)
    o, w1b0, w2b0, w_out_b = _attn_prompt(
        qk, vb, lam_params, sub_g, t=T_ATT, hd=hd, lam_init=lam_init,
        cast=((w1_all, 0, d), (w2_all, 0, d_ff), (w_out_ab[0], 0, d)))

    split = lambda c: zs[:, c * aw:(c + 1) * aw].reshape(s_len, bsz, n_heads, 2, hd)
    nr = 2 * n_heads * s_len
    q_s = split(0).transpose(1, 3, 2, 0, 4).reshape(bsz, nr, hd)
    kn = split(1).transpose(1, 0, 3, 2, 4)[:, :, :, :, None, :]
    kn = jnp.broadcast_to(kn, (bsz, s_len, 2, n_heads, s_len, hd)).reshape(bsz, s_len, nr, hd)
    vn = split(2).transpose(1, 0, 3, 2, 4)[:, :, :, None, :, None, :]
    vn = jnp.broadcast_to(vn, (bsz, s_len, 2, 2, n_heads, s_len, hd)).reshape(bsz, s_len, 2 * nr, hd)
    page_rows = lambda c: (c[0].reshape(-1, page, n_heads, 2, hd).transpose(0, 1, 3, 2, 4)
                           .reshape(-1, page * 2 * n_heads, hd))
    o_s = _attn_sample(page_table, q_s, kn, vn, page_rows(cache_k), page_rows(cache_v), lam_params,
                       sub_g, n_heads=n_heads, hd=hd, s_len=s_len, page=page,
                       gp=min(PAGES_PER_STEP, page_table.shape[1]), lam_init=lam_init)
    o_s = o_s.reshape(bsz, n_heads, s_len, hd2).transpose(1, 2, 0, 3).reshape(n_heads, n_s, hd2)
    o = lax.dynamic_update_slice(o, o_s.astype(BF16), (0, t_len, 0))

    gaw, gxw = gate_a_w[0].astype(BF16), gate_x_w[0].astype(BF16)
    lru_args = (conv_w[0], row2(conv_b[0]), gaw, row2(gate_a_b[0]), gxw, row2(gate_x_b[0]),
                row2(lru_lambda[0]))
    yb, h_last_p, w1b1, w2b1 = _lru_prompt(zb, *lru_args, t_len=t_len, tm=TM_LRU,
                                           cast=((w1_all, d, d), (w2_all, d_ff, d_ff)))
    xb_s = zs[:, 3 * aw:3 * aw + bw].reshape(s_len, bsz, bw)
    gb_s = zs[:, 3 * aw + bw:].reshape(s_len, bsz, bw)
    yb_s, h_last_s = _lru_sample(xb_s, gb_s, state_conv[0].transpose(1, 0, 2), state_lru[0], *lru_args)
    yb = lax.dynamic_update_slice(yb, yb_s.reshape(n_s, bw), (t_len, 0))

    x1 = _out_proj(o, yb, x0, w_out_b, row2(mix_ln_g[0]), row2(mix_ln_b[0]), alpha=alpha, tm=TM_OUT)
    x2 = _mlp(x1, w1b0, w2b0, row2(ff_ln_g[0]), row2(ff_ln_b[0]), alpha=alpha, tm=TM_MLP, tf=TF_MLP)

    pool_args = (pool_w[0].astype(BF16), row2(pool_scale[0]), row2(mix_ln_g[1]), row2(mix_ln_b[1]))
    x3 = _pool_prompt(x2, *pool_args, alpha=alpha, tm=TM_POOL)
    x2_s = x2[t_len:t_len + n_s].reshape(s_len, bsz, d)
    x3_s = _pool_sample(x2_s, state_pool[0].transpose(1, 0, 2), *pool_args, alpha=alpha)
    x3 = lax.dynamic_update_slice(x3, x3_s.reshape(n_s, d), (t_len, 0))
    y_p, y_s = _mlp_final(x3, w1b1, w2b1, row2(ff_ln_g[1]), row2(ff_ln_b[1]), alpha=alpha, tm=TM_MLP,
                          tf=TF_MLP, n_meta=n_meta, seq=seq, n_s=n_s)

    to_bt = lambda v: v.reshape(s_len, bsz, -1).transpose(1, 0, 2)
    kv_p = lambda v: (v.reshape(t_len, 2, n_heads, hd).transpose(0, 2, 1, 3)
                      .reshape(1, 1, t_len, n_heads, hd2))
    kv_s = lambda c: to_bt(zs[:, c * aw:(c + 1) * aw]).reshape(1, bsz, s_len, n_heads, hd2)
    y_prompt = y_p[None]
    y_sample = to_bt(y_s)
    new_conv_prompt = zb[t_len - (cw - 1):t_len, :bw][None, None]
    new_lru_prompt = h_last_p[None]
    new_pool_prompt = x2[t_len - n_buf:t_len][None, None]
    new_conv_sample = xb_s[s_len - (cw - 1):].transpose(1, 0, 2)[None]
    new_lru_sample = h_last_s[None]
    new_pool_sample = jnp.concatenate([state_pool[0], x2_s.transpose(1, 0, 2)], axis=1)[:, -n_buf:][None]
    return (y_prompt, y_sample, kv_p(k_rows), kv_p(v_rows), new_conv_prompt, new_lru_prompt, new_pool_prompt,
            kv_s(1), kv_s(2), new_conv_sample, new_lru_sample, new_pool_sample)
```

```python
import functools
import math

import jax
import jax.numpy as jnp
from jax import lax
from jax.experimental import pallas as pl
from jax.experimental.pallas import tpu as pltpu

F32 = jnp.float32
BF16 = jnp.bfloat16

LN_EPS = 1e-5
RMS_EPS = 1e-5
LRU_C = 8.0
POOL_WINDOWS = (2, 4, 8, 16)
NEG = -1e30
LOG2E = 1.4426950408889634
LANES = 128
N_AUG = 3

ROW_ALIGN = 768
T_ATT = 768
TN_PROJ = 1024
TM_LRU = 256
TM_OUT = 384
TM_MLP = 768
TF_MLP = 512
TM_POOL = 256
ATT_HEAD_UNROLL = 2
PAGES_PER_STEP = 16
VMEM_MB = 56
VMEM_MB_IN_PROJ = 60


def _params(sem, mb=VMEM_MB):
    return pltpu.CompilerParams(dimension_semantics=sem, vmem_limit_bytes=mb * 2**20)


def _layer_norm(y, g, b):
    mu = jnp.mean(y, axis=-1, keepdims=True)
    d = y - mu
    var = jnp.mean(d * d, axis=-1, keepdims=True)
    return d * lax.rsqrt(var + LN_EPS) * g + b


def _dot(a, b):
    return jnp.dot(a, b, preferred_element_type=F32)


def _dot_nt(a, b):
    return lax.dot_general(a, b, (((1,), (1,)), ((), ())), preferred_element_type=F32)


def _sigmoid(x):
    return 1.0 / (1.0 + jnp.exp(-x))


def _gelu_tanh(x):
    c = math.sqrt(2.0 / math.pi)
    return 0.5 * x * (1.0 + jnp.tanh(c * (x + 0.044715 * (x * x * x))))


def _lambda_value(lq1, lk1, lq2, lk2, lam_init):
    s1 = jnp.sum(lq1 * lk1, axis=-1, keepdims=True)
    s2 = jnp.sum(lq2 * lk2, axis=-1, keepdims=True)
    return jnp.exp(s1) - jnp.exp(s2) + lam_init


def _div_pow2(x, n):
    assert n & (n - 1) == 0
    return lax.shift_right_logical(x, n.bit_length() - 1)


def _mod_pow2(x, n):
    assert n & (n - 1) == 0
    return lax.bitwise_and(x, n - 1)


BF16_ROWS = 16


def _cast_specs(job, n_steps, step_of):
    mat, row_start, rows = job
    cols = mat.shape[1]
    units = rows // BF16_ROWS
    assert units * BF16_ROWS == rows
    n_blocks = max(k for k in range(1, n_steps + 1) if units % k == 0)
    br = rows // n_blocks
    assert row_start % br == 0
    first = row_start // br
    block = lambda *idx: jnp.minimum(step_of(*idx), n_blocks - 1)
    return (pl.BlockSpec((br, cols), lambda *idx: (first + block(*idx), 0)),
            pl.BlockSpec((br, cols), lambda *idx: (block(*idx), 0)))


def _cast_shape(job):
    mat, _, rows = job
    return jax.ShapeDtypeStruct((rows, mat.shape[1]), BF16)


def _cast_blocks(srcs, dsts):
    for src, dst in zip(srcs, dsts):
        dst[...] = src[...].astype(BF16)


def _head_slope(h, n_heads):
    return 2.0 ** (-8.0 * (h + 1) / n_heads)


def _in_proj_kernel(x_ref, w_ref, zt_ref, zb_ref, qk_ref, vb_ref, kr_ref, vr_ref, *, n_heads, hd):
    i = pl.program_id(0)
    j = pl.program_id(1)
    z = _dot(x_ref[...].astype(BF16), w_ref[...])
    tm = z.shape[0]

    @pl.when(j >= 3)
    def _():
        zb_ref[...] = z

    @pl.when(jnp.logical_and(i == pl.num_programs(0) - 1, j < 3))
    def _():
        zt_ref[...] = z

    lane = lax.broadcasted_iota(jnp.int32, (tm, hd), 1)

    def cache_rows(dst_ref):
        for c in range(2):
            for h in range(n_heads):
                col = (2 * h + c) * hd
                dst_ref[pl.ds(c * n_heads + h, tm, stride=2 * n_heads), :] = z[:, col:col + hd]

    @pl.when(j == 0)
    def _():
        ones = jnp.where(lane < N_AUG, 1.0, 0.0).astype(BF16)
        q_scale = hd ** -0.5 * LOG2E
        for h in range(n_heads):
            for c in range(2):
                col = (2 * h + c) * hd
                qk_ref[0, h, :, 2 * hd * c:2 * hd * c + hd] = (z[:, col:col + hd] * q_scale).astype(BF16)
                qk_ref[0, h, :, 2 * hd * c + hd:2 * hd * (c + 1)] = ones

    @pl.when(j == 1)
    def _():
        row = lax.broadcasted_iota(jnp.int32, (tm, hd), 0).astype(F32)
        for h in range(n_heads):
            a = row * (_head_slope(h, n_heads) * LOG2E)
            hi = a.astype(BF16).astype(F32)
            mid = (a - hi).astype(BF16).astype(F32)
            lo = a - hi - mid
            aug = jnp.where(lane == 0, hi, jnp.where(lane == 1, mid, jnp.where(lane == 2, lo, 0.0)))
            aug = aug.astype(BF16)
            for c in range(2):
                col = (2 * h + c) * hd
                qk_ref[0, h, :, 2 * hd * c:2 * hd * c + hd] = z[:, col:col + hd].astype(BF16)
                qk_ref[0, h, :, 2 * hd * c + hd:2 * hd * (c + 1)] = aug
        cache_rows(kr_ref)

    @pl.when(j == 2)
    def _():
        for h in range(n_heads):
            vb_ref[h] = z[:, 2 * hd * h:2 * hd * (h + 1)].astype(BF16)
        cache_rows(vr_ref)


def _in_proj(x, w, *, t_len, tm, tn, n_heads, hd):
    r, d = x.shape
    n = w.shape[1]
    assert hd == LANES and tn == 2 * hd * n_heads and n == 5 * tn
    kr = 2 * n_heads
    n_i = r // tm
    row_after = lambda i, j, j_written: jnp.where(jnp.logical_and(j > j_written, i < n_i - 1), i + 1, i)
    qk_part = lambda i, j: jnp.where(j == 1, 1, jnp.where(jnp.logical_and(j > 1, i == n_i - 1), 1, 0))
    cache_blk = lambda j_written: pl.BlockSpec((tm * kr, hd), lambda i, j: (row_after(i, j, j_written), 0))
    cache_shape = jax.ShapeDtypeStruct((t_len * kr, hd), F32)
    return pl.pallas_call(
        functools.partial(_in_proj_kernel, n_heads=n_heads, hd=hd),
        grid=(n_i, n // tn),
        in_specs=[pl.BlockSpec((tm, d), lambda i, j: (i, 0)),
                  pl.BlockSpec((d, tn), lambda i, j: (0, j))],
        out_specs=[
                   pl.BlockSpec((tm, tn), lambda i, j: (0, jnp.where(i == n_i - 1, jnp.minimum(j, 2), 0))),
                   pl.BlockSpec((tm, tn), lambda i, j: (i, jnp.maximum(j - 3, 0))),
                   pl.BlockSpec((1, n_heads, tm, 4 * hd),
                                lambda i, j: (qk_part(i, j), 0, row_after(i, j, 1), 0)),
                   pl.BlockSpec((n_heads, tm, 2 * hd), lambda i, j: (0, row_after(i, j, 2), 0)),
                   cache_blk(1), cache_blk(2)],
        out_shape=[jax.ShapeDtypeStruct((tm, 3 * tn), F32),
                   jax.ShapeDtypeStruct((r, 2 * tn), F32),
                   jax.ShapeDtypeStruct((2, n_heads, r, 4 * hd), BF16),
                   jax.ShapeDtypeStruct((n_heads, r, 2 * hd), BF16),
                   cache_shape, cache_shape],
        compiler_params=_params(("arbitrary", "arbitrary"), mb=VMEM_MB_IN_PROJ),
        name="in_proj",
    )(x, w)


def _attn_prompt_kernel(qi_ref, kj_ref, qa_ref, ka_ref, vb_ref, sl_ref, lq1_ref, lk1_ref, lq2_ref,
                        lk2_ref, g_ref, *rest, t, n_heads, hd, lam_init, n_cast):
    cast_src = rest[:n_cast]
    o_ref = rest[n_cast]
    cast_dst = rest[n_cast + 1:2 * n_cast + 1]
    m_sc, l_sc, acc_sc = rest[2 * n_cast + 1:]
    _cast_blocks(cast_src, cast_dst)
    p = pl.program_id(0)
    i = qi_ref[p]
    j = kj_ref[p]

    @pl.when(j == 0)
    def _():
        m_sc[...] = jnp.full_like(m_sc, NEG)
        l_sc[...] = jnp.zeros_like(l_sc)
        acc_sc[...] = jnp.zeros_like(acc_sc)

    def tile(masked):
        if masked:
            keep = (lax.broadcasted_iota(jnp.int32, (t, t), 1)
                    <= lax.broadcasted_iota(jnp.int32, (t, t), 0))
        tile_dist = ((j - i) * t).astype(F32)

        def head(h, carry):
            shift = sl_ref[h][:, 0:1] * tile_dist
            for c in range(2):
                s = _dot_nt(qa_ref[h, :, 2 * hd * c:2 * hd * (c + 1)],
                            ka_ref[h, :, 2 * hd * c:2 * hd * (c + 1)])
                if masked:
                    s = jnp.where(keep, s, NEG)
                idx = 2 * h + c
                m_old = m_sc[idx]
                m_new = jnp.maximum(m_old, jnp.max(s, axis=1, keepdims=True) + shift)
                alpha = jnp.exp2(m_old - m_new)
                pr = jnp.exp2(s - (m_new - shift))
                l_sc[idx] = alpha * l_sc[idx] + jnp.sum(pr, axis=1, keepdims=True)
                acc_sc[idx] = alpha * acc_sc[idx] + _dot(pr.astype(BF16), vb_ref[h])
                m_sc[idx] = m_new
            return carry

        lax.fori_loop(0, n_heads, head, 0, unroll=ATT_HEAD_UNROLL)

    @pl.when(j < i)
    def _():
        tile(False)

    @pl.when(j == i)
    def _():
        tile(True)
        lam = _lambda_value(lq1_ref[...], lk1_ref[...], lq2_ref[...], lk2_ref[...], lam_init)
        g = g_ref[...]
        for h in range(n_heads):
            o1 = acc_sc[2 * h] / l_sc[2 * h]
            o2 = acc_sc[2 * h + 1] / l_sc[2 * h + 1]
            oh = o1 - lam * o2
            oh = oh * lax.rsqrt(jnp.mean(oh * oh, axis=-1, keepdims=True) + RMS_EPS)
            o_ref[h] = (oh * g * (1.0 - lam_init)).astype(o_ref.dtype)


def _attn_prompt(qk, vb, lam_params, sub_g, *, t, hd, lam_init, cast=()):
    _, n_heads, r, _ = qk.shape
    nq = r // t
    pairs = [(i, j) for i in range(nq) for j in range(i + 1)]
    qi = jnp.asarray([i for i, _ in pairs], jnp.int32)
    kj = jnp.asarray([j for _, j in pairs], jnp.int32)
    slopes = jnp.asarray([[[_head_slope(h, n_heads) * LOG2E] * LANES] for h in range(n_heads)], F32)
    vec = lambda n: pl.BlockSpec((1, n), lambda p, qi, kj: (0, 0))
    cast_specs = [_cast_specs(job, len(pairs), lambda p, qi, kj: p) for job in cast]
    grid_spec = pltpu.PrefetchScalarGridSpec(
        num_scalar_prefetch=2,
        grid=(len(pairs),),
        in_specs=[pl.BlockSpec((None, n_heads, t, 4 * hd), lambda p, qi, kj: (0, 0, qi[p], 0)),
                  pl.BlockSpec((None, n_heads, t, 4 * hd), lambda p, qi, kj: (1, 0, kj[p], 0)),
                  pl.BlockSpec((n_heads, t, 2 * hd), lambda p, qi, kj: (0, kj[p], 0)),
                  pl.BlockSpec((n_heads, 1, LANES), lambda p, qi, kj: (0, 0, 0)),
                  vec(hd), vec(hd), vec(hd), vec(hd), vec(2 * hd)] + [s[0] for s in cast_specs],
        out_specs=[pl.BlockSpec((n_heads, t, 2 * hd), lambda p, qi, kj: (0, qi[p], 0))]
                  + [s[1] for s in cast_specs],
        scratch_shapes=[pltpu.VMEM((2 * n_heads, t, 1), F32),
                        pltpu.VMEM((2 * n_heads, t, 1), F32),
                        pltpu.VMEM((2 * n_heads, t, 2 * hd), F32)],
    )
    kern = functools.partial(_attn_prompt_kernel, t=t, n_heads=n_heads, hd=hd, lam_init=lam_init,
                             n_cast=len(cast))
    return pl.pallas_call(
        kern, grid_spec=grid_spec,
        out_shape=[jax.ShapeDtypeStruct((n_heads, r, 2 * hd), BF16)] + [_cast_shape(job) for job in cast],
        compiler_params=_params(("arbitrary",)),
        name="attn_prompt",
    )(qi, kj, qk, qk, vb, slopes, *lam_params, sub_g, *[job[0] for job in cast])


def _attn_sample_kernel(pt_ref, q_ref, kn_ref, vn_ref, *rest, n_heads, hd, s_len, past_len, page,
                        gp, lam_init):
    del pt_ref
    page_refs = rest[:2 * gp]
    (lq1_ref, lk1_ref, lq2_ref, lk2_ref, g_ref, o_ref,
     qb_sc, bias_sc, m_sc, l_sc, acc_sc) = rest[2 * gp:]
    p = pl.program_id(1)
    nr = 2 * n_heads * s_len
    kr = 2 * n_heads
    ncol = page * kr
    half = nr // 2
    scale = hd ** -0.5

    def row_info(shape):
        row = lax.broadcasted_iota(jnp.int32, shape, 0)
        tok = _mod_pow2(row, s_len)
        head = _mod_pow2(_div_pow2(row, s_len), n_heads)
        cmap = _div_pow2(row, s_len * n_heads)
        slope = jnp.zeros(shape, F32)
        for h in range(n_heads):
            slope = jnp.where(head == h, _head_slope(h, n_heads), slope)
        return tok, head, cmap, slope

    @pl.when(p == 0)
    def _():
        qb_sc[...] = (q_ref[0] * scale).astype(BF16)
        tok, head, cmap, slope = row_info((nr, ncol))
        col = lax.broadcasted_iota(jnp.int32, (nr, ncol), 1)
        key = _div_pow2(col, kr)
        own = _mod_pow2(col, kr) == cmap * n_heads + head
        bias_sc[...] = jnp.where(own, -slope * (past_len + tok - key).astype(F32), NEG)
        m_sc[...] = jnp.full_like(m_sc, NEG)
        l_sc[...] = jnp.zeros_like(l_sc)
        acc_sc[...] = jnp.zeros_like(acc_sc)

    tok1, _, _, slope1 = row_info((nr, 1))
    m = m_sc[...]
    l = l_sc[...]
    acc = acc_sc[...]
    scores, shifts = [], []
    m_new = m
    for gi in range(gp):
        kp = page_refs[gi][0].astype(BF16)
        s = _dot_nt(qb_sc[...], kp) + bias_sc[...]
        shift = slope1 * ((p * gp + gi) * page).astype(F32)
        m_new = jnp.maximum(m_new, jnp.max(s, axis=1, keepdims=True) + shift)
        scores.append(s)
        shifts.append(shift)
    alpha = jnp.exp(m - m_new)
    l = alpha * l
    pv = jnp.zeros((2 * nr, hd), F32)
    for gi in range(gp):
        pr = jnp.exp(scores[gi] - (m_new - shifts[gi]))
        l = l + jnp.sum(pr, axis=1, keepdims=True)
        top, bot = pr[:half], pr[half:]
        chunks = range(ncol // LANES)
        top_r = jnp.concatenate(
            [pltpu.roll(top[:, LANES * cc:LANES * (cc + 1)], n_heads, 1) for cc in chunks], axis=1)
        bot_r = jnp.concatenate(
            [pltpu.roll(bot[:, LANES * cc:LANES * (cc + 1)], LANES - n_heads, 1) for cc in chunks], axis=1)
        lhs = jnp.concatenate([top, bot_r, top_r, bot], axis=0).astype(BF16)
        vp = page_refs[gp + gi][0].astype(BF16)
        pv = pv + _dot(lhs, vp)
    acc = jnp.concatenate([alpha, alpha], axis=0) * acc + pv
    m = m_new
    m_sc[...] = m
    l_sc[...] = l
    acc_sc[...] = acc

    @pl.when(p == pl.num_programs(1) - 1)
    def _():
        qf = q_ref[0] * scale
        s_new = []
        for t2 in range(s_len):
            st = jnp.sum(qf * kn_ref[0, t2], axis=1, keepdims=True)
            st = st - slope1 * (tok1 - t2).astype(F32)
            s_new.append(jnp.where(t2 <= tok1, st, NEG))
        m_fin = m
        for st in s_new:
            m_fin = jnp.maximum(m_fin, st)
        a_fin = jnp.exp(m - m_fin)
        l_fin = a_fin * l
        acc_f = jnp.concatenate([a_fin, a_fin], axis=0) * acc
        for t2 in range(s_len):
            pt = jnp.exp(s_new[t2] - m_fin)
            l_fin = l_fin + pt
            acc_f = acc_f + jnp.concatenate([pt, pt], axis=0) * vn_ref[0, t2]
        lam = _lambda_value(lq1_ref[...], lk1_ref[...], lq2_ref[...], lk2_ref[...], lam_init)
        halves = []
        for c2 in range(2):
            a = acc_f[c2 * nr:(c2 + 1) * nr] / l_fin
            halves.append(a[:half] - lam * a[half:])
        ms = (jnp.sum(halves[0] * halves[0], axis=1, keepdims=True)
              + jnp.sum(halves[1] * halves[1], axis=1, keepdims=True)) / (2.0 * hd)
        rs = lax.rsqrt(ms + RMS_EPS) * (1.0 - lam_init)
        for c2 in range(2):
            o_ref[0, :, c2 * hd:(c2 + 1) * hd] = halves[c2] * rs * g_ref[:, c2 * hd:(c2 + 1) * hd]


def _attn_sample(page_table, q, kn, vn, ck, cv, lam_params, sub_g, *, n_heads, hd, s_len, page,
                 gp, lam_init):
    bsz, nr, _ = q.shape
    n_pages = page_table.shape[1]
    assert n_pages % gp == 0 and hd == LANES and nr == 2 * n_heads * s_len
    ncol = ck.shape[1]
    pg = lambda gi: pl.BlockSpec((1, ncol, hd), lambda b, p, pt: (pt[b, p * gp + gi], 0, 0))
    vec = lambda n: pl.BlockSpec((1, n), lambda b, p, pt: (0, 0))
    grid_spec = pltpu.PrefetchScalarGridSpec(
        num_scalar_prefetch=1,
        grid=(bsz, n_pages // gp),
        in_specs=[pl.BlockSpec((1, nr, hd), lambda b, p, pt: (b, 0, 0)),
                  pl.BlockSpec((1, s_len, nr, hd), lambda b, p, pt: (b, 0, 0, 0)),
                  pl.BlockSpec((1, s_len, 2 * nr, hd), lambda b, p, pt: (b, 0, 0, 0))]
                 + [pg(gi) for gi in range(gp)] + [pg(gi) for gi in range(gp)]
                 + [vec(hd), vec(hd), vec(hd), vec(hd), vec(2 * hd)],
        out_specs=pl.BlockSpec((1, nr // 2, 2 * hd), lambda b, p, pt: (b, 0, 0)),
        scratch_shapes=[pltpu.VMEM((nr, hd), BF16),
                        pltpu.VMEM((nr, ncol), F32),
                        pltpu.VMEM((nr, 1), F32),
                        pltpu.VMEM((nr, 1), F32),
                        pltpu.VMEM((2 * nr, hd), F32)],
    )
    kern = functools.partial(_attn_sample_kernel, n_heads=n_heads, hd=hd, s_len=s_len,
                             past_len=n_pages * page, page=page, gp=gp, lam_init=lam_init)
    return pl.pallas_call(
        kern, grid_spec=grid_spec,
        out_shape=jax.ShapeDtypeStruct((bsz, nr // 2, 2 * hd), F32),
        compiler_params=_params(("arbitrary", "arbitrary")),
        name="attn_sample",
    )(page_table, q, kn, vn, *([ck] * gp), *([cv] * gp), *lam_params, sub_g)


def _lru_gates(xc, gaw_ref, gab_ref, gxw_ref, gxb_ref, lam_ref):
    n_blk, bs, _ = gaw_ref.shape
    xcb = xc.astype(BF16)
    ra = jnp.concatenate([_dot(xcb[:, n * bs:(n + 1) * bs], gaw_ref[n]) for n in range(n_blk)], axis=1)
    rx = jnp.concatenate([_dot(xcb[:, n * bs:(n + 1) * bs], gxw_ref[n]) for n in range(n_blk)], axis=1)
    r = _sigmoid(ra + gab_ref[...])
    ig = _sigmoid(rx + gxb_ref[...])
    neg_lam = -lam_ref[...]
    softplus = jnp.maximum(neg_lam, 0.0) + jnp.log1p(jnp.exp(-jnp.abs(neg_lam)))
    log_a = -LRU_C * r * softplus
    a = jnp.exp(log_a)
    mult = jnp.sqrt(-jnp.tanh(log_a) * (a * a + 1.0))
    return a, ig, mult


def _lru_prompt_kernel(xb_ref, gb_ref, cw_ref, cb_ref, gaw_ref, gab_ref, gxw_ref, gxb_ref, lam_ref,
                       *rest, tm, conv_w, last_tile, last_row, n_cast):
    cast_src = rest[:n_cast]
    yb_ref, hl_ref = rest[n_cast:n_cast + 2]
    cast_dst = rest[n_cast + 2:2 * n_cast + 2]
    xbuf, hbuf, hcar = rest[2 * n_cast + 2:]
    _cast_blocks(cast_src, cast_dst)
    i = pl.program_id(0)
    w = xb_ref.shape[1]
    n_grp = tm // 8

    @pl.when(i == 0)
    def _():
        xbuf[0:8, :] = jnp.zeros((8, w), F32)
        hcar[...] = jnp.zeros_like(hcar)

    xbuf[8:8 + tm, :] = xb_ref[...]
    xc = cb_ref[...] + cw_ref[conv_w - 1:conv_w, :] * xb_ref[...]
    for jj in range(conv_w - 1):
        start = 8 - (conv_w - 1) + jj
        xc = xc + cw_ref[jj:jj + 1, :] * xbuf[start:start + tm, :]
    xbuf[0:8, :] = xbuf[tm:tm + 8, :]

    a, ig, mult = _lru_gates(xc, gaw_ref, gab_ref, gxw_ref, gxb_ref, lam_ref)
    grow = i * tm + lax.broadcasted_iota(jnp.int32, (tm, 1), 0)
    mult = jnp.where(grow == 0, 1.0, mult)
    u = xc * ig * mult

    a3 = a.reshape(n_grp, 8, w)
    u3 = u.reshape(n_grp, 8, w)
    sub = lax.broadcasted_iota(jnp.int32, (n_grp, 8, w), 1)
    for sh in (1, 2, 4):
        a_prev = pltpu.roll(a3, sh, axis=1)
        u_prev = pltpu.roll(u3, sh, axis=1)
        ok = sub >= sh
        u3 = jnp.where(ok, a3 * u_prev + u3, u3)
        a3 = jnp.where(ok, a3 * a_prev, a3)
    h_prev = hcar[...]
    for gi in range(n_grp):
        hg = a3[gi] * h_prev + u3[gi]
        hbuf[8 * gi:8 * (gi + 1), :] = hg
        h_prev = hg[7:8, :]
    hcar[...] = h_prev

    yb_ref[...] = (hbuf[...] * _gelu_tanh(gb_ref[...])).astype(yb_ref.dtype)

    @pl.when(i == last_tile)
    def _():
        hl_ref[...] = hbuf[last_row:last_row + 1, :]


def _lru_prompt(z, conv_w, conv_b, gaw, gab, gxw, gxb, lru_lambda, *, t_len, tm, cast=()):
    r = z.shape[0]
    w = conv_b.shape[1]
    cw = conv_w.shape[0]
    n_blk, bs, _ = gaw.shape
    n_steps = r // tm
    full = lambda shape: pl.BlockSpec(shape, lambda i: (0,) * len(shape))
    cast_specs = [_cast_specs(job, n_steps, lambda i: i) for job in cast]
    kern = functools.partial(_lru_prompt_kernel, tm=tm, conv_w=cw, n_cast=len(cast),
                             last_tile=(t_len - 1) // tm, last_row=(t_len - 1) % tm)
    return pl.pallas_call(
        kern,
        grid=(n_steps,),
        in_specs=[pl.BlockSpec((tm, w), lambda i: (i, 0)),
                  pl.BlockSpec((tm, w), lambda i: (i, 1)),
                  full((cw, w)), full((1, w)),
                  full((n_blk, bs, bs)), full((1, w)),
                  full((n_blk, bs, bs)), full((1, w)), full((1, w))] + [s[0] for s in cast_specs],
        out_specs=[pl.BlockSpec((tm, w), lambda i: (i, 0)), full((1, w))] + [s[1] for s in cast_specs],
        out_shape=[jax.ShapeDtypeStruct((r, w), BF16), jax.ShapeDtypeStruct((1, w), F32)]
                  + [_cast_shape(job) for job in cast],
        scratch_shapes=[pltpu.VMEM((tm + 8, w), F32), pltpu.VMEM((tm, w), F32),
                        pltpu.VMEM((1, w), F32)],
        compiler_params=_params(("arbitrary",)),
        name="lru_prompt",
    )(z, z, conv_w, conv_b, gaw, gab, gxw, gxb, lru_lambda, *[job[0] for job in cast])


def _lru_sample_kernel(xb_ref, gb_ref, cs_ref, h0_ref, cw_ref, cb_ref, gaw_ref, gab_ref, gxw_ref,
                       gxb_ref, lam_ref, yb_ref, hl_ref, *, conv_w):
    s_len, bsz, w = xb_ref.shape
    xp = [cs_ref[jj] for jj in range(conv_w - 1)] + [xb_ref[tt] for tt in range(s_len)]
    xcs = []
    for tt in range(s_len):
        xc = cb_ref[...] + cw_ref[0:1, :] * xp[tt]
        for jj in range(1, conv_w):
            xc = xc + cw_ref[jj:jj + 1, :] * xp[tt + jj]
        xcs.append(xc)
    xc = jnp.concatenate(xcs, axis=0)
    a, ig, mult = _lru_gates(xc, gaw_ref, gab_ref, gxw_ref, gxb_ref, lam_ref)
    u = xc * ig * mult
    h = h0_ref[...]
    for tt in range(s_len):
        rows = slice(tt * bsz, (tt + 1) * bsz)
        h = a[rows, :] * h + u[rows, :]
        yb_ref[tt] = (h * _gelu_tanh(gb_ref[tt])).astype(yb_ref.dtype)
    hl_ref[...] = h


def _lru_sample(xb, gb, conv_state, h0, conv_w, conv_b, gaw, gab, gxw, gxb, lru_lambda):
    s_len, bsz, w = xb.shape
    kern = functools.partial(_lru_sample_kernel, conv_w=conv_w.shape[0])
    return pl.pallas_call(
        kern,
        out_shape=[jax.ShapeDtypeStruct((s_len, bsz, w), BF16), jax.ShapeDtypeStruct((bsz, w), F32)],
        compiler_params=pltpu.CompilerParams(vmem_limit_bytes=VMEM_MB * 2**20),
        name="lru_sample",
    )(xb, gb, conv_state, h0, conv_w, conv_b, gaw, gab, gxw, gxb, lru_lambda)


def _out_proj_kernel(o_ref, yb_ref, x_ref, w_ref, g_ref, b_ref, out_ref, *, alpha):
    n_heads, _, hw = o_ref.shape
    aw = n_heads * hw
    mixed = _dot(yb_ref[...], w_ref[aw:, :])
    for h in range(n_heads):
        mixed = mixed + _dot(o_ref[h], w_ref[hw * h:hw * (h + 1), :])
    out_ref[...] = _layer_norm(alpha * x_ref[...] + mixed, g_ref[...], b_ref[...])


def _out_proj(o, yb, x, w, g, b, *, alpha, tm):
    r, d = x.shape
    n_heads, _, hw = o.shape
    bw = yb.shape[1]
    full = lambda shape: pl.BlockSpec(shape, lambda i: (0,) * len(shape))
    return pl.pallas_call(
        functools.partial(_out_proj_kernel, alpha=alpha),
        grid=(r // tm,),
        in_specs=[pl.BlockSpec((n_heads, tm, hw), lambda i: (0, i, 0)),
                  pl.BlockSpec((tm, bw), lambda i: (i, 0)),
                  pl.BlockSpec((tm, d), lambda i: (i, 0)),
                  full((n_heads * hw + bw, d)), full((1, d)), full((1, d))],
        out_specs=pl.BlockSpec((tm, d), lambda i: (i, 0)),
        out_shape=jax.ShapeDtypeStruct((r, d), F32),
        compiler_params=_params(("parallel",)),
        name="out_proj",
    )(o, yb, x, w, g, b)


def _mlp_kernel(x_ref, w1_ref, w2_ref, g_ref, b_ref, out_ref, xb_sc, acc_sc, *, alpha):
    f = pl.program_id(1)

    @pl.when(f == 0)
    def _():
        xb_sc[...] = x_ref[...].astype(BF16)
        acc_sc[...] = jnp.zeros_like(acc_sc)

    h = jnp.maximum(_dot(xb_sc[...], w1_ref[...]), 0.0)
    acc_sc[...] += _dot((h * h).astype(BF16), w2_ref[...])

    @pl.when(f == pl.num_programs(1) - 1)
    def _():
        out_ref[...] = _layer_norm(alpha * x_ref[...] + acc_sc[...], g_ref[...], b_ref[...])


def _mlp_final_kernel(x_ref, w1_ref, w2_ref, g_ref, b_ref, yp_ref, ys_ref, xb_sc, acc_sc, *, alpha,
                      n_meta, n_tail, s_start, n_s):
    i = pl.program_id(0)
    f = pl.program_id(1)
    last = pl.num_programs(0) - 1

    @pl.when(f == 0)
    def _():
        xb_sc[...] = x_ref[...].astype(BF16)
        acc_sc[...] = jnp.zeros_like(acc_sc)

    h = jnp.maximum(_dot(xb_sc[...], w1_ref[...]), 0.0)
    acc_sc[...] += _dot((h * h).astype(BF16), w2_ref[...])

    def result():
        return _layer_norm(alpha * x_ref[...] + acc_sc[...], g_ref[...], b_ref[...])

    @pl.when(jnp.logical_and(f == pl.num_programs(1) - 1, i < last))
    def _():
        yp_ref[...] = result()

    @pl.when(jnp.logical_and(f == pl.num_programs(1) - 1, i == last))
    def _():
        res = result()
        yp_ref[0:n_tail, :] = res[n_meta:n_meta + n_tail, :]
        ys_ref[...] = res[s_start:s_start + n_s, :]


def _mlp(x, w1, w2, g, b, *, alpha, tm, tf):
    r, d = x.shape
    ff = w1.shape[1]
    vec = pl.BlockSpec((1, d), lambda i, f: (0, 0))
    return pl.pallas_call(
        functools.partial(_mlp_kernel, alpha=alpha),
        grid=(r // tm, ff // tf),
        in_specs=[pl.BlockSpec((tm, d), lambda i, f: (i, 0)),
                  pl.BlockSpec((d, tf), lambda i, f: (0, f)),
                  pl.BlockSpec((tf, d), lambda i, f: (f, 0)),
                  vec, vec],
        out_specs=pl.BlockSpec((tm, d), lambda i, f: (i, 0)),
        out_shape=jax.ShapeDtypeStruct((r, d), F32),
        scratch_shapes=[pltpu.VMEM((tm, d), BF16), pltpu.VMEM((tm, d), F32)],
        compiler_params=_params(("parallel", "arbitrary")),
        name="mlp",
    )(x, w1, w2, g, b)


def _mlp_final(x, w1, w2, g, b, *, alpha, tm, tf, n_meta, seq, n_s):
    r, d = x.shape
    ff = w1.shape[1]
    n_i = r // tm
    last_row = (n_i - 1) * tm
    n_tail = seq - last_row
    s_start = n_meta + seq - last_row
    assert pl.cdiv(seq, tm) == n_i and 0 < n_tail <= tm - n_meta and s_start + n_s <= tm
    assert n_meta % 8 == 0
    vec = pl.BlockSpec((1, d), lambda i, f: (0, 0))
    x_rows = lambda i, f: (pl.multiple_of(jnp.where(i < n_i - 1, n_meta + tm * i, last_row), 8), 0)
    kern = functools.partial(_mlp_final_kernel, alpha=alpha, n_meta=n_meta, n_tail=n_tail,
                             s_start=s_start, n_s=n_s)
    return pl.pallas_call(
        kern,
        grid=(n_i, ff // tf),
        in_specs=[pl.BlockSpec((pl.Element(tm), pl.Element(d)), x_rows),
                  pl.BlockSpec((d, tf), lambda i, f: (0, f)),
                  pl.BlockSpec((tf, d), lambda i, f: (f, 0)),
                  vec, vec],
        out_specs=[pl.BlockSpec((tm, d), lambda i, f: (i, 0)),
                   pl.BlockSpec((n_s, d), lambda i, f: (0, 0))],
        out_shape=[jax.ShapeDtypeStruct((seq, d), F32), jax.ShapeDtypeStruct((n_s, d), F32)],
        scratch_shapes=[pltpu.VMEM((tm, d), BF16), pltpu.VMEM((tm, d), F32)],
        compiler_params=_params(("arbitrary", "arbitrary")),
        name="mlp_final",
    )(x, w1, w2, g, b)


def _pool_project(ms, x, w_ref, sc_ref, g_ref, b_ref, alpha):
    y = jnp.concatenate([_dot(ms[gi].astype(BF16), w_ref[gi]) for gi in range(len(ms))], axis=1)
    return _layer_norm(alpha * x + y * sc_ref[...], g_ref[...], b_ref[...])


def _pool_prompt_kernel(x_ref, w_ref, sc_ref, g_ref, b_ref, out_ref, *bufs, tm, alpha, pad):
    i = pl.program_id(0)
    n_g = len(POOL_WINDOWS)
    gw = x_ref.shape[1] // n_g

    @pl.when(i == 0)
    def _():
        for buf in bufs:
            buf[0:pad, :] = jnp.zeros((pad, buf.shape[1]), F32)

    x = x_ref[...]
    pos = i * tm + lax.broadcasted_iota(jnp.int32, (tm, 1), 0)
    ms = []
    cur = x
    for k, win in enumerate(POOL_WINDOWS):
        half = win // 2
        buf = bufs[k]
        buf[pad:pad + tm, :] = cur
        ws = cur + buf[pad - half:pad - half + tm, :]
        buf[0:pad, :] = buf[tm:tm + pad, :]
        cnt = jnp.minimum(win, pos + 1).astype(F32)
        ms.append(ws[:, :gw] / cnt - x[:, k * gw:(k + 1) * gw])
        if k + 1 < n_g:
            cur = ws[:, gw:]
    out_ref[...] = _pool_project(ms, x, w_ref, sc_ref, g_ref, b_ref, alpha)


def _pool_prompt(x, w, sc, g, b, *, alpha, tm):
    r, d = x.shape
    n_g, gw, _ = w.shape
    pad = 16
    assert POOL_WINDOWS == tuple(2 ** (k + 1) for k in range(n_g)) and max(POOL_WINDOWS) // 2 <= pad
    full = lambda shape: pl.BlockSpec(shape, lambda i: (0,) * len(shape))
    return pl.pallas_call(
        functools.partial(_pool_prompt_kernel, tm=tm, alpha=alpha, pad=pad),
        grid=(r // tm,),
        in_specs=[pl.BlockSpec((tm, d), lambda i: (i, 0)),
                  full((n_g, gw, gw)), full((1, d)), full((1, d)), full((1, d))],
        out_specs=pl.BlockSpec((tm, d), lambda i: (i, 0)),
        out_shape=jax.ShapeDtypeStruct((r, d), F32),
        scratch_shapes=[pltpu.VMEM((tm + pad, (n_g - k) * gw), F32) for k in range(n_g)],
        compiler_params=_params(("arbitrary",)),
        name="pool_prompt",
    )(x, w, sc, g, b)


def _pool_sample_kernel(x_ref, st_ref, w_ref, sc_ref, g_ref, b_ref, out_ref, *, alpha):
    s_len, bsz, d = x_ref.shape
    n_buf = st_ref.shape[0]
    gw = d // len(POOL_WINDOWS)
    ext = [st_ref[jj] for jj in range(n_buf)] + [x_ref[tt] for tt in range(s_len)]
    x = jnp.concatenate(ext[n_buf:], axis=0)
    ms = []
    for gi, win in enumerate(POOL_WINDOWS):
        cols = slice(gi * gw, (gi + 1) * gw)
        rows = []
        for tt in range(s_len):
            ws = ext[n_buf + tt][:, cols]
            for kk in range(1, win):
                ws = ws + ext[n_buf + tt - kk][:, cols]
            rows.append(ws / float(win) - ext[n_buf + tt][:, cols])
        ms.append(jnp.concatenate(rows, axis=0))
    y = _pool_project(ms, x, w_ref, sc_ref, g_ref, b_ref, alpha)
    for tt in range(s_len):
        out_ref[tt] = y[tt * bsz:(tt + 1) * bsz, :]


def _pool_sample(x, state, w, sc, g, b, *, alpha):
    return pl.pallas_call(
        functools.partial(_pool_sample_kernel, alpha=alpha),
        out_shape=jax.ShapeDtypeStruct(x.shape, F32),
        compiler_params=pltpu.CompilerParams(vmem_limit_bytes=VMEM_MB * 2**20),
        name="pool_sample",
    )(x, state, w, sc, g, b)


def kernel(x_prompt, x_sample, cache_k, cache_v, state_conv, state_lru, state_pool, page_table, meta_tokens, w_in, lam_q1, lam_k1, lam_q2, lam_k2, sub_norm_g, conv_w, conv_b, gate_a_w, gate_a_b, gate_x_w, gate_x_b, lru_lambda, w_out_ab, pool_w, pool_scale, mix_ln_g, mix_ln_b, w_ff1, w_ff2, ff_ln_g, ff_ln_b):
    n_prompt, seq, d = x_prompt.shape
    bsz, s_len, _ = x_sample.shape
    depth = w_ff1.shape[0]
    assert n_prompt == 1 and depth == 2
    n_meta = meta_tokens.shape[0]
    n_heads, hd2 = cache_k.shape[-2:]
    hd = hd2 // 2
    aw = n_heads * hd2
    bw = state_lru.shape[-1]
    assert aw == bw and w_in.shape[-1] == 3 * aw + 2 * bw
    page = cache_k.shape[2]
    n_buf = state_pool.shape[2]
    cw = conv_w.shape[1]
    assert s_len >= cw - 1
    alpha = (2.0 * depth) ** 0.25
    lam_init = 0.8 - 0.6 * math.exp(-0.3 * 0)

    t_len = n_meta + seq
    n_s = bsz * s_len
    rows = -(-(t_len + n_s) // ROW_ALIGN) * ROW_ALIGN
    xs_tm = x_sample.transpose(1, 0, 2).reshape(n_s, d)
    x0 = jnp.concatenate([meta_tokens, x_prompt[0], xs_tm,
                          jnp.zeros((rows - t_len - n_s, d), F32)], axis=0)

    row2 = lambda v: v.reshape(1, -1)
    lam_params = (row2(lam_q1[0]), row2(lam_k1[0]), row2(lam_q2[0]), row2(lam_k2[0]))
    sub_g = row2(sub_norm_g[0])

    assert rows - T_ATT < t_len
    zt, zb, qk, vb, k_rows, v_rows = _in_proj(x0, w_in[0].astype(BF16), t_len=t_len, tm=T_ATT,
                                              tn=TN_PROJ, n_heads=n_heads, hd=hd)
    s0 = t_len - (rows - T_ATT)
    assert s0 + n_s <= T_ATT
    zs = jnp.concatenate([zt[s0:s0 + n_s], zb[t_len:t_len + n_s]], axis=1)

    d_ff = w_ff1.shape[2]
    w1_all, w2_all = w_ff1.reshape(depth * d, d_ff), w_ff2.reshape(depth * d_ff, d)
    o, w1b0, w2b0, w_out_b = _attn_prompt(
        qk, vb, lam_params, sub_g, t=T_ATT, hd=hd, lam_init=lam_init,
        cast=((w1_all, 0, d), (w2_all, 0, d_ff), (w_out_ab[0], 0, d)))

    split = lambda c: zs[:, c * aw:(c + 1) * aw].reshape(s_len, bsz, n_heads, 2, hd)
    nr = 2 * n_heads * s_len
    q_s = split(0).transpose(1, 3, 2, 0, 4).reshape(bsz, nr, hd)
    kn = split(1).transpose(1, 0, 3, 2, 4)[:, :, :, :, None, :]
    kn = jnp.broadcast_to(kn, (bsz, s_len, 2, n_heads, s_len, hd)).reshape(bsz, s_len, nr, hd)
    vn = split(2).transpose(1, 0, 3, 2, 4)[:, :, :, None, :, None, :]
    vn = jnp.broadcast_to(vn, (bsz, s_len, 2, 2, n_heads, s_len, hd)).reshape(bsz, s_len, 2 * nr, hd)
    page_rows = lambda c: (c[0].reshape(-1, page, n_heads, 2, hd).transpose(0, 1, 3, 2, 4)
                           .reshape(-1, page * 2 * n_heads, hd))
    o_s = _attn_sample(page_table, q_s, kn, vn, page_rows(cache_k), page_rows(cache_v), lam_params,
                       sub_g, n_heads=n_heads, hd=hd, s_len=s_len, page=page,
                       gp=min(PAGES_PER_STEP, page_table.shape[1]), lam_init=lam_init)
    o_s = o_s.reshape(bsz, n_heads, s_len, hd2).transpose(1, 2, 0, 3).reshape(n_heads, n_s, hd2)
    o = lax.dynamic_update_slice(o, o_s.astype(BF16), (0, t_len, 0))

    gaw, gxw = gate_a_w[0].astype(BF16), gate_x_w[0].astype(BF16)
    lru_args = (conv_w[0], row2(conv_b[0]), gaw, row2(gate_a_b[0]), gxw, row2(gate_x_b[0]),
                row2(lru_lambda[0]))
    yb, h_last_p, w1b1, w2b1 = _lru_prompt(zb, *lru_args, t_len=t_len, tm=TM_LRU,
                                           cast=((w1_all, d, d), (w2_all, d_ff, d_ff)))
    xb_s = zs[:, 3 * aw:3 * aw + bw].reshape(s_len, bsz, bw)
    gb_s = zs[:, 3 * aw + bw:].reshape(s_len, bsz, bw)
    yb_s, h_last_s = _lru_sample(xb_s, gb_s, state_conv[0].transpose(1, 0, 2), state_lru[0], *lru_args)
    yb = lax.dynamic_update_slice(yb, yb_s.reshape(n_s, bw), (t_len, 0))

    x1 = _out_proj(o, yb, x0, w_out_b, row2(mix_ln_g[0]), row2(mix_ln_b[0]), alpha=alpha, tm=TM_OUT)
    x2 = _mlp(x1, w1b0, w2b0, row2(ff_ln_g[0]), row2(ff_ln_b[0]), alpha=alpha, tm=TM_MLP, tf=TF_MLP)

    pool_args = (pool_w[0].astype(BF16), row2(pool_scale[0]), row2(mix_ln_g[1]), row2(mix_ln_b[1]))
    x3 = _pool_prompt(x2, *pool_args, alpha=alpha, tm=TM_POOL)
    x2_s = x2[t_len:t_len + n_s].reshape(s_len, bsz, d)
    x3_s = _pool_sample(x2_s, state_pool[0].transpose(1, 0, 2), *pool_args, alpha=alpha)
    x3 = lax.dynamic_update_slice(x3, x3_s.reshape(n_s, d), (t_len, 0))
    y_p, y_s = _mlp_final(x3, w1b1, w2b1, row2(ff_ln_g[1]), row2(ff_ln_b[1]), alpha=alpha, tm=TM_MLP,
                          tf=TF_MLP, n_meta=n_meta, seq=seq, n_s=n_s)

    to_bt = lambda v: v.reshape(s_len, bsz, -1).transpose(1, 0, 2)
    kv_p = lambda v: (v.reshape(t_len, 2, n_heads, hd).transpose(0, 2, 1, 3)
                      .reshape(1, 1, t_len, n_heads, hd2))
    kv_s = lambda c: to_bt(zs[:, c * aw:(c + 1) * aw]).reshape(1, bsz, s_len, n_heads, hd2)
    y_prompt = y_p[None]
    y_sample = to_bt(y_s)
    new_conv_prompt = zb[t_len - (cw - 1):t_len, :bw][None, None]
    new_lru_prompt = h_last_p[None]
    new_pool_prompt = x2[t_len - n_buf:t_len][None, None]
    new_conv_sample = xb_s[s_len - (cw - 1):].transpose(1, 0, 2)[None]
    new_lru_sample = h_last_s[None]
    new_pool_sample = jnp.concatenate([state_pool[0], x2_s.transpose(1, 0, 2)], axis=1)[:, -n_buf:][None]
    return (y_prompt, y_sample, kv_p(k_rows), kv_p(v_rows), new_conv_prompt, new_lru_prompt, new_pool_prompt,
            kv_s(1), kv_s(2), new_conv_sample, new_lru_sample, new_pool_sample)
```

```python
import functools
import math

import jax
import jax.numpy as jnp
from jax import lax
from jax.experimental import pallas as pl
from jax.experimental.pallas import tpu as pltpu

F32 = jnp.float32
BF16 = jnp.bfloat16

LN_EPS = 1e-5
RMS_EPS = 1e-5
LRU_C = 8.0
POOL_WINDOWS = (2, 4, 8, 16)
NEG = -1e30
LOG2E = 1.4426950408889634
LANES = 128
N_AUG = 3

ROW_ALIGN = 768
T_ATT = 768
TN_PROJ = 1024
TM_LRU = 256
TM_OUT = 384
TM_MLP = 768
TF_MLP = 512
TM_POOL = 256
ATT_HEAD_UNROLL = 2
PAGES_PER_STEP = 8
PAGE_RING = 3
VMEM_MB = 56
VMEM_MB_IN_PROJ = 60


def _params(sem, mb=VMEM_MB):
    return pltpu.CompilerParams(dimension_semantics=sem, vmem_limit_bytes=mb * 2**20)


def _layer_norm(y, g, b):
    mu = jnp.mean(y, axis=-1, keepdims=True)
    d = y - mu
    var = jnp.mean(d * d, axis=-1, keepdims=True)
    return d * lax.rsqrt(var + LN_EPS) * g + b


def _dot(a, b):
    return jnp.dot(a, b, preferred_element_type=F32)


def _dot_nt(a, b):
    return lax.dot_general(a, b, (((1,), (1,)), ((), ())), preferred_element_type=F32)


def _sigmoid(x):
    return 1.0 / (1.0 + jnp.exp(-x))


def _gelu_tanh(x):
    c = math.sqrt(2.0 / math.pi)
    return 0.5 * x * (1.0 + jnp.tanh(c * (x + 0.044715 * (x * x * x))))


def _lambda_value(lq1, lk1, lq2, lk2, lam_init):
    s1 = jnp.sum(lq1 * lk1, axis=-1, keepdims=True)
    s2 = jnp.sum(lq2 * lk2, axis=-1, keepdims=True)
    return jnp.exp(s1) - jnp.exp(s2) + lam_init


def _div_pow2(x, n):
    assert n & (n - 1) == 0
    return lax.shift_right_logical(x, n.bit_length() - 1)


def _mod_pow2(x, n):
    assert n & (n - 1) == 0
    return lax.bitwise_and(x, n - 1)


BF16_ROWS = 16


def _cast_specs(job, n_steps, step_of):
    mat, row_start, rows = job
    cols = mat.shape[1]
    units = rows // BF16_ROWS
    assert units * BF16_ROWS == rows
    n_blocks = max(k for k in range(1, n_steps + 1) if units % k == 0)
    br = rows // n_blocks
    assert row_start % br == 0
    first = row_start // br
    block = lambda *idx: jnp.minimum(step_of(*idx), n_blocks - 1)
    return (pl.BlockSpec((br, cols), lambda *idx: (first + block(*idx), 0)),
            pl.BlockSpec((br, cols), lambda *idx: (block(*idx), 0)))


def _cast_shape(job):
    mat, _, rows = job
    return jax.ShapeDtypeStruct((rows, mat.shape[1]), BF16)


def _cast_blocks(srcs, dsts):
    for src, dst in zip(srcs, dsts):
        dst[...] = src[...].astype(BF16)


def _head_slope(h, n_heads):
    return 2.0 ** (-8.0 * (h + 1) / n_heads)


def _in_proj_kernel(x_ref, w_ref, zt_ref, zb_ref, qk_ref, vb_ref, kr_ref, vr_ref, *, n_heads, hd):
    i = pl.program_id(0)
    j = pl.program_id(1)
    z = _dot(x_ref[...].astype(BF16), w_ref[...])
    tm = z.shape[0]

    @pl.when(j >= 3)
    def _():
        zb_ref[...] = z

    @pl.when(jnp.logical_and(i == pl.num_programs(0) - 1, j < 3))
    def _():
        zt_ref[...] = z

    lane = lax.broadcasted_iota(jnp.int32, (tm, hd), 1)

    def cache_rows(dst_ref):
        for c in range(2):
            for h in range(n_heads):
                col = (2 * h + c) * hd
                dst_ref[pl.ds(c * n_heads + h, tm, stride=2 * n_heads), :] = z[:, col:col + hd]

    @pl.when(j == 0)
    def _():
        ones = jnp.where(lane < N_AUG, 1.0, 0.0).astype(BF16)
        q_scale = hd ** -0.5 * LOG2E
        for h in range(n_heads):
            for c in range(2):
                col = (2 * h + c) * hd
                qk_ref[0, h, :, 2 * hd * c:2 * hd * c + hd] = (z[:, col:col + hd] * q_scale).astype(BF16)
                qk_ref[0, h, :, 2 * hd * c + hd:2 * hd * (c + 1)] = ones

    @pl.when(j == 1)
    def _():
        row = lax.broadcasted_iota(jnp.int32, (tm, hd), 0).astype(F32)
        for h in range(n_heads):
            a = row * (_head_slope(h, n_heads) * LOG2E)
            hi = a.astype(BF16).astype(F32)
            mid = (a - hi).astype(BF16).astype(F32)
            lo = a - hi - mid
            aug = jnp.where(lane == 0, hi, jnp.where(lane == 1, mid, jnp.where(lane == 2, lo, 0.0)))
            aug = aug.astype(BF16)
            for c in range(2):
                col = (2 * h + c) * hd
                qk_ref[0, h, :, 2 * hd * c:2 * hd * c + hd] = z[:, col:col + hd].astype(BF16)
                qk_ref[0, h, :, 2 * hd * c + hd:2 * hd * (c + 1)] = aug
        cache_rows(kr_ref)

    @pl.when(j == 2)
    def _():
        for h in range(n_heads):
            vb_ref[h] = z[:, 2 * hd * h:2 * hd * (h + 1)].astype(BF16)
        cache_rows(vr_ref)


def _in_proj(x, w, *, t_len, tm, tn, n_heads, hd):
    r, d = x.shape
    n = w.shape[1]
    assert hd == LANES and tn == 2 * hd * n_heads and n == 5 * tn
    kr = 2 * n_heads
    n_i = r // tm
    cache_blk = pl.BlockSpec((tm * kr, hd), lambda i, j: (i, 0))
    cache_shape = jax.ShapeDtypeStruct((t_len * kr, hd), F32)
    return pl.pallas_call(
        functools.partial(_in_proj_kernel, n_heads=n_heads, hd=hd),
        grid=(n_i, n // tn),
        in_specs=[pl.BlockSpec((tm, d), lambda i, j: (i, 0)),
                  pl.BlockSpec((d, tn), lambda i, j: (0, j))],
        out_specs=[
                   pl.BlockSpec((tm, tn), lambda i, j: (0, jnp.where(i == n_i - 1, jnp.minimum(j, 2), 0))),
                   pl.BlockSpec((tm, tn), lambda i, j: (i, jnp.maximum(j - 3, 0))),
                   pl.BlockSpec((1, n_heads, tm, 4 * hd), lambda i, j: (jnp.minimum(j, 1), 0, i, 0)),
                   pl.BlockSpec((n_heads, tm, 2 * hd), lambda i, j: (0, i, 0)),
                   cache_blk, cache_blk],
        out_shape=[jax.ShapeDtypeStruct((tm, 3 * tn), F32),
                   jax.ShapeDtypeStruct((r, 2 * tn), F32),
                   jax.ShapeDtypeStruct((2, n_heads, r, 4 * hd), BF16),
                   jax.ShapeDtypeStruct((n_heads, r, 2 * hd), BF16),
                   cache_shape, cache_shape],
        compiler_params=_params(("arbitrary", "arbitrary"), mb=VMEM_MB_IN_PROJ),
        name="in_proj",
    )(x, w)


def _attn_prompt_kernel(qi_ref, kj_ref, qa_ref, ka_ref, vb_ref, sl_ref, lq1_ref, lk1_ref, lq2_ref,
                        lk2_ref, g_ref, *rest, t, n_heads, hd, lam_init, n_cast):
    cast_src = rest[:n_cast]
    o_ref = rest[n_cast]
    cast_dst = rest[n_cast + 1:2 * n_cast + 1]
    m_sc, l_sc, acc_sc = rest[2 * n_cast + 1:]
    _cast_blocks(cast_src, cast_dst)
    p = pl.program_id(0)
    i = qi_ref[p]
    j = kj_ref[p]

    @pl.when(j == 0)
    def _():
        m_sc[...] = jnp.full_like(m_sc, NEG)
        l_sc[...] = jnp.zeros_like(l_sc)
        acc_sc[...] = jnp.zeros_like(acc_sc)

    def tile(masked):
        if masked:
            keep = (lax.broadcasted_iota(jnp.int32, (t, t), 1)
                    <= lax.broadcasted_iota(jnp.int32, (t, t), 0))
        tile_dist = ((j - i) * t).astype(F32)

        def head(h, carry):
            shift = sl_ref[h][:, 0:1] * tile_dist
            for c in range(2):
                s = _dot_nt(qa_ref[h, :, 2 * hd * c:2 * hd * (c + 1)],
                            ka_ref[h, :, 2 * hd * c:2 * hd * (c + 1)])
                if masked:
                    s = jnp.where(keep, s, NEG)
                idx = 2 * h + c
                m_old = m_sc[idx]
                m_new = jnp.maximum(m_old, jnp.max(s, axis=1, keepdims=True) + shift)
                alpha = jnp.exp2(m_old - m_new)
                pr = jnp.exp2(s - (m_new - shift))
                l_sc[idx] = alpha * l_sc[idx] + jnp.sum(pr, axis=1, keepdims=True)
                acc_sc[idx] = alpha * acc_sc[idx] + _dot(pr.astype(BF16), vb_ref[h])
                m_sc[idx] = m_new
            return carry

        lax.fori_loop(0, n_heads, head, 0, unroll=ATT_HEAD_UNROLL)

    @pl.when(j < i)
    def _():
        tile(False)

    @pl.when(j == i)
    def _():
        tile(True)
        lam = _lambda_value(lq1_ref[...], lk1_ref[...], lq2_ref[...], lk2_ref[...], lam_init)
        g = g_ref[...]
        for h in range(n_heads):
            o1 = acc_sc[2 * h] / l_sc[2 * h]
            o2 = acc_sc[2 * h + 1] / l_sc[2 * h + 1]
            oh = o1 - lam * o2
            oh = oh * lax.rsqrt(jnp.mean(oh * oh, axis=-1, keepdims=True) + RMS_EPS)
            o_ref[h] = (oh * g * (1.0 - lam_init)).astype(o_ref.dtype)


def _attn_prompt(qk, vb, lam_params, sub_g, *, t, hd, lam_init, cast=()):
    _, n_heads, r, _ = qk.shape
    nq = r // t
    pairs = [(i, j) for i in range(nq) for j in range(i + 1)]
    qi = jnp.asarray([i for i, _ in pairs], jnp.int32)
    kj = jnp.asarray([j for _, j in pairs], jnp.int32)
    slopes = jnp.asarray([[[_head_slope(h, n_heads) * LOG2E] * LANES] for h in range(n_heads)], F32)
    vec = lambda n: pl.BlockSpec((1, n), lambda p, qi, kj: (0, 0))
    cast_specs = [_cast_specs(job, len(pairs), lambda p, qi, kj: p) for job in cast]
    grid_spec = pltpu.PrefetchScalarGridSpec(
        num_scalar_prefetch=2,
        grid=(len(pairs),),
        in_specs=[pl.BlockSpec((None, n_heads, t, 4 * hd), lambda p, qi, kj: (0, 0, qi[p], 0)),
                  pl.BlockSpec((None, n_heads, t, 4 * hd), lambda p, qi, kj: (1, 0, kj[p], 0)),
                  pl.BlockSpec((n_heads, t, 2 * hd), lambda p, qi, kj: (0, kj[p], 0)),
                  pl.BlockSpec((n_heads, 1, LANES), lambda p, qi, kj: (0, 0, 0)),
                  vec(hd), vec(hd), vec(hd), vec(hd), vec(2 * hd)] + [s[0] for s in cast_specs],
        out_specs=[pl.BlockSpec((n_heads, t, 2 * hd), lambda p, qi, kj: (0, qi[p], 0))]
                  + [s[1] for s in cast_specs],
        scratch_shapes=[pltpu.VMEM((2 * n_heads, t, 1), F32),
                        pltpu.VMEM((2 * n_heads, t, 1), F32),
                        pltpu.VMEM((2 * n_heads, t, 2 * hd), F32)],
    )
    kern = functools.partial(_attn_prompt_kernel, t=t, n_heads=n_heads, hd=hd, lam_init=lam_init,
                             n_cast=len(cast))
    return pl.pallas_call(
        kern, grid_spec=grid_spec,
        out_shape=[jax.ShapeDtypeStruct((n_heads, r, 2 * hd), BF16)] + [_cast_shape(job) for job in cast],
        compiler_params=_params(("arbitrary",)),
        name="attn_prompt",
    )(qi, kj, qk, qk, vb, slopes, *lam_params, sub_g, *[job[0] for job in cast])


def _attn_sample_kernel(pt_ref, q_ref, kn_ref, vn_ref, *rest, n_heads, hd, s_len, past_len, page,
                        gp, lam_init):
    (ck_hbm, cv_hbm, lq1_ref, lk1_ref, lq2_ref, lk2_ref, g_ref, o_ref,
     qb_sc, bias_sc, m_sc, l_sc, acc_sc, kbuf, vbuf, sem) = rest
    p = pl.program_id(1)
    n_p = pl.num_programs(1)
    step = pl.program_id(0) * n_p + p
    n_steps = pl.num_programs(0) * n_p

    def page_copies(st):
        slot = lax.rem(st, PAGE_RING)
        seq = lax.div(st, n_p)
        first = lax.rem(st, n_p) * gp
        out = []
        for gi in range(gp):
            pid = pt_ref[seq, first + gi]
            out.append(pltpu.make_async_copy(ck_hbm.at[pid], kbuf.at[slot, gi], sem.at[slot, 0]))
            out.append(pltpu.make_async_copy(cv_hbm.at[pid], vbuf.at[slot, gi], sem.at[slot, 1]))
        return out

    def start_step(st):
        for cp in page_copies(st):
            cp.start()

    @pl.when(step == 0)
    def _():
        for ahead in range(PAGE_RING - 1):
            @pl.when(ahead < n_steps)
            def _():
                start_step(step + ahead)

    @pl.when(step + PAGE_RING - 1 < n_steps)
    def _():
        start_step(step + PAGE_RING - 1)

    for cp in page_copies(step):
        cp.wait()
    slot = lax.rem(step, PAGE_RING)
    nr = 2 * n_heads * s_len
    kr = 2 * n_heads
    ncol = page * kr
    half = nr // 2
    scale = hd ** -0.5

    def row_info(shape):
        row = lax.broadcasted_iota(jnp.int32, shape, 0)
        tok = _mod_pow2(row, s_len)
        head = _mod_pow2(_div_pow2(row, s_len), n_heads)
        cmap = _div_pow2(row, s_len * n_heads)
        slope = jnp.zeros(shape, F32)
        for h in range(n_heads):
            slope = jnp.where(head == h, _head_slope(h, n_heads), slope)
        return tok, head, cmap, slope

    @pl.when(p == 0)
    def _():
        qb_sc[...] = (q_ref[0] * scale).astype(BF16)
        tok, head, cmap, slope = row_info((nr, ncol))
        col = lax.broadcasted_iota(jnp.int32, (nr, ncol), 1)
        key = _div_pow2(col, kr)
        own = _mod_pow2(col, kr) == cmap * n_heads + head
        bias_sc[...] = jnp.where(own, -slope * (past_len + tok - key).astype(F32), NEG)
        m_sc[...] = jnp.full_like(m_sc, NEG)
        l_sc[...] = jnp.zeros_like(l_sc)
        acc_sc[...] = jnp.zeros_like(acc_sc)

    tok1, _, _, slope1 = row_info((nr, 1))
    m = m_sc[...]
    l = l_sc[...]
    acc = acc_sc[...]
    scores, shifts = [], []
    m_new = m
    for gi in range(gp):
        kp = kbuf[slot, gi].astype(BF16)
        s = _dot_nt(qb_sc[...], kp) + bias_sc[...]
        shift = slope1 * ((p * gp + gi) * page).astype(F32)
        m_new = jnp.maximum(m_new, jnp.max(s, axis=1, keepdims=True) + shift)
        scores.append(s)
        shifts.append(shift)
    alpha = jnp.exp(m - m_new)
    l = alpha * l
    pv = jnp.zeros((2 * nr, hd), F32)
    for gi in range(gp):
        pr = jnp.exp(scores[gi] - (m_new - shifts[gi]))
        l = l + jnp.sum(pr, axis=1, keepdims=True)
        top, bot = pr[:half], pr[half:]
        chunks = range(ncol // LANES)
        top_r = jnp.concatenate(
            [pltpu.roll(top[:, LANES * cc:LANES * (cc + 1)], n_heads, 1) for cc in chunks], axis=1)
        bot_r = jnp.concatenate(
            [pltpu.roll(bot[:, LANES * cc:LANES * (cc + 1)], LANES - n_heads, 1) for cc in chunks], axis=1)
        lhs = jnp.concatenate([top, bot_r, top_r, bot], axis=0).astype(BF16)
        vp = vbuf[slot, gi].astype(BF16)
        pv = pv + _dot(lhs, vp)
    acc = jnp.concatenate([alpha, alpha], axis=0) * acc + pv
    m = m_new
    m_sc[...] = m
    l_sc[...] = l
    acc_sc[...] = acc

    @pl.when(p == pl.num_programs(1) - 1)
    def _():
        qf = q_ref[0] * scale
        s_new = []
        for t2 in range(s_len):
            st = jnp.sum(qf * kn_ref[0, t2], axis=1, keepdims=True)
            st = st - slope1 * (tok1 - t2).astype(F32)
            s_new.append(jnp.where(t2 <= tok1, st, NEG))
        m_fin = m
        for st in s_new:
            m_fin = jnp.maximum(m_fin, st)
        a_fin = jnp.exp(m - m_fin)
        l_fin = a_fin * l
        acc_f = jnp.concatenate([a_fin, a_fin], axis=0) * acc
        for t2 in range(s_len):
            pt = jnp.exp(s_new[t2] - m_fin)
            l_fin = l_fin + pt
            acc_f = acc_f + jnp.concatenate([pt, pt], axis=0) * vn_ref[0, t2]
        lam = _lambda_value(lq1_ref[...], lk1_ref[...], lq2_ref[...], lk2_ref[...], lam_init)
        halves = []
        for c2 in range(2):
            a = acc_f[c2 * nr:(c2 + 1) * nr] / l_fin
            halves.append(a[:half] - lam * a[half:])
        ms = (jnp.sum(halves[0] * halves[0], axis=1, keepdims=True)
              + jnp.sum(halves[1] * halves[1], axis=1, keepdims=True)) / (2.0 * hd)
        rs = lax.rsqrt(ms + RMS_EPS) * (1.0 - lam_init)
        for c2 in range(2):
            o_ref[0, :, c2 * hd:(c2 + 1) * hd] = halves[c2] * rs * g_ref[:, c2 * hd:(c2 + 1) * hd]


def _attn_sample(page_table, q, kn, vn, ck, cv, lam_params, sub_g, *, n_heads, hd, s_len, page,
                 gp, lam_init):
    bsz, nr, _ = q.shape
    n_pages = page_table.shape[1]
    assert n_pages % gp == 0 and hd == LANES and nr == 2 * n_heads * s_len
    ncol = ck.shape[1]
    vec = lambda n: pl.BlockSpec((1, n), lambda b, p, pt: (0, 0))
    in_hbm = pl.BlockSpec(memory_space=pl.ANY)
    grid_spec = pltpu.PrefetchScalarGridSpec(
        num_scalar_prefetch=1,
        grid=(bsz, n_pages // gp),
        in_specs=[pl.BlockSpec((1, nr, hd), lambda b, p, pt: (b, 0, 0)),
                  pl.BlockSpec((1, s_len, nr, hd), lambda b, p, pt: (b, 0, 0, 0)),
                  pl.BlockSpec((1, s_len, 2 * nr, hd), lambda b, p, pt: (b, 0, 0, 0)),
                  in_hbm, in_hbm,
                  vec(hd), vec(hd), vec(hd), vec(hd), vec(2 * hd)],
        out_specs=pl.BlockSpec((1, nr // 2, 2 * hd), lambda b, p, pt: (b, 0, 0)),
        scratch_shapes=[pltpu.VMEM((nr, hd), BF16),
                        pltpu.VMEM((nr, ncol), F32),
                        pltpu.VMEM((nr, 1), F32),
                        pltpu.VMEM((nr, 1), F32),
                        pltpu.VMEM((2 * nr, hd), F32),
                        pltpu.VMEM((PAGE_RING, gp, ncol, hd), F32),
                        pltpu.VMEM((PAGE_RING, gp, ncol, hd), F32),
                        pltpu.SemaphoreType.DMA((PAGE_RING, 2))],
    )
    kern = functools.partial(_attn_sample_kernel, n_heads=n_heads, hd=hd, s_len=s_len,
                             past_len=n_pages * page, page=page, gp=gp, lam_init=lam_init)
    return pl.pallas_call(
        kern, grid_spec=grid_spec,
        out_shape=jax.ShapeDtypeStruct((bsz, nr // 2, 2 * hd), F32),
        compiler_params=_params(("arbitrary", "arbitrary")),
        name="attn_sample",
    )(page_table, q, kn, vn, ck, cv, *lam_params, sub_g)


def _lru_gates(xc, gaw_ref, gab_ref, gxw_ref, gxb_ref, lam_ref):
    n_blk, bs, _ = gaw_ref.shape
    xcb = xc.astype(BF16)
    ra = jnp.concatenate([_dot(xcb[:, n * bs:(n + 1) * bs], gaw_ref[n]) for n in range(n_blk)], axis=1)
    rx = jnp.concatenate([_dot(xcb[:, n * bs:(n + 1) * bs], gxw_ref[n]) for n in range(n_blk)], axis=1)
    r = _sigmoid(ra + gab_ref[...])
    ig = _sigmoid(rx + gxb_ref[...])
    neg_lam = -lam_ref[...]
    softplus = jnp.maximum(neg_lam, 0.0) + jnp.log1p(jnp.exp(-jnp.abs(neg_lam)))
    log_a = -LRU_C * r * softplus
    a = jnp.exp(log_a)
    mult = jnp.sqrt(-jnp.tanh(log_a) * (a * a + 1.0))
    return a, ig, mult


def _lru_prompt_kernel(xb_ref, gb_ref, cw_ref, cb_ref, gaw_ref, gab_ref, gxw_ref, gxb_ref, lam_ref,
                       *rest, tm, conv_w, last_tile, last_row, n_cast):
    cast_src = rest[:n_cast]
    yb_ref, hl_ref = rest[n_cast:n_cast + 2]
    cast_dst = rest[n_cast + 2:2 * n_cast + 2]
    xbuf, hbuf, hcar = rest[2 * n_cast + 2:]
    _cast_blocks(cast_src, cast_dst)
    i = pl.program_id(0)
    w = xb_ref.shape[1]
    n_grp = tm // 8

    @pl.when(i == 0)
    def _():
        xbuf[0:8, :] = jnp.zeros((8, w), F32)
        hcar[...] = jnp.zeros_like(hcar)

    xbuf[8:8 + tm, :] = xb_ref[...]
    xc = cb_ref[...] + cw_ref[conv_w - 1:conv_w, :] * xb_ref[...]
    for jj in range(conv_w - 1):
        start = 8 - (conv_w - 1) + jj
        xc = xc + cw_ref[jj:jj + 1, :] * xbuf[start:start + tm, :]
    xbuf[0:8, :] = xbuf[tm:tm + 8, :]

    a, ig, mult = _lru_gates(xc, gaw_ref, gab_ref, gxw_ref, gxb_ref, lam_ref)
    grow = i * tm + lax.broadcasted_iota(jnp.int32, (tm, 1), 0)
    mult = jnp.where(grow == 0, 1.0, mult)
    u = xc * ig * mult

    a3 = a.reshape(n_grp, 8, w)
    u3 = u.reshape(n_grp, 8, w)
    sub = lax.broadcasted_iota(jnp.int32, (n_grp, 8, w), 1)
    for sh in (1, 2, 4):
        a_prev = pltpu.roll(a3, sh, axis=1)
        u_prev = pltpu.roll(u3, sh, axis=1)
        ok = sub >= sh
        u3 = jnp.where(ok, a3 * u_prev + u3, u3)
        a3 = jnp.where(ok, a3 * a_prev, a3)
    h_prev = hcar[...]
    for gi in range(n_grp):
        hg = a3[gi] * h_prev + u3[gi]
        hbuf[8 * gi:8 * (gi + 1), :] = hg
        h_prev = hg[7:8, :]
    hcar[...] = h_prev

    yb_ref[...] = (hbuf[...] * _gelu_tanh(gb_ref[...])).astype(yb_ref.dtype)

    @pl.when(i == last_tile)
    def _():
        hl_ref[...] = hbuf[last_row:last_row + 1, :]


def _lru_prompt(z, conv_w, conv_b, gaw, gab, gxw, gxb, lru_lambda, *, t_len, tm, cast=()):
    r = z.shape[0]
    w = conv_b.shape[1]
    cw = conv_w.shape[0]
    n_blk, bs, _ = gaw.shape
    n_steps = r // tm
    full = lambda shape: pl.BlockSpec(shape, lambda i: (0,) * len(shape))
    cast_specs = [_cast_specs(job, n_steps, lambda i: i) for job in cast]
    kern = functools.partial(_lru_prompt_kernel, tm=tm, conv_w=cw, n_cast=len(cast),
                             last_tile=(t_len - 1) // tm, last_row=(t_len - 1) % tm)
    return pl.pallas_call(
        kern,
        grid=(n_steps,),
        in_specs=[pl.BlockSpec((tm, w), lambda i: (i, 0)),
                  pl.BlockSpec((tm, w), lambda i: (i, 1)),
                  full((cw, w)), full((1, w)),
                  full((n_blk, bs, bs)), full((1, w)),
                  full((n_blk, bs, bs)), full((1, w)), full((1, w))] + [s[0] for s in cast_specs],
        out_specs=[pl.BlockSpec((tm, w), lambda i: (i, 0)), full((1, w))] + [s[1] for s in cast_specs],
        out_shape=[jax.ShapeDtypeStruct((r, w), BF16), jax.ShapeDtypeStruct((1, w), F32)]
                  + [_cast_shape(job) for job in cast],
        scratch_shapes=[pltpu.VMEM((tm + 8, w), F32), pltpu.VMEM((tm, w), F32),
                        pltpu.VMEM((1, w), F32)],
        compiler_params=_params(("arbitrary",)),
        name="lru_prompt",
    )(z, z, conv_w, conv_b, gaw, gab, gxw, gxb, lru_lambda, *[job[0] for job in cast])


def _lru_sample_kernel(xb_ref, gb_ref, cs_ref, h0_ref, cw_ref, cb_ref, gaw_ref, gab_ref, gxw_ref,
                       gxb_ref, lam_ref, yb_ref, hl_ref, *, conv_w):
    s_len, bsz, w = xb_ref.shape
    xp = [cs_ref[jj] for jj in range(conv_w - 1)] + [xb_ref[tt] for tt in range(s_len)]
    xcs = []
    for tt in range(s_len):
        xc = cb_ref[...] + cw_ref[0:1, :] * xp[tt]
        for jj in range(1, conv_w):
            xc = xc + cw_ref[jj:jj + 1, :] * xp[tt + jj]
        xcs.append(xc)
    xc = jnp.concatenate(xcs, axis=0)
    a, ig, mult = _lru_gates(xc, gaw_ref, gab_ref, gxw_ref, gxb_ref, lam_ref)
    u = xc * ig * mult
    h = h0_ref[...]
    for tt in range(s_len):
        rows = slice(tt * bsz, (tt + 1) * bsz)
        h = a[rows, :] * h + u[rows, :]
        yb_ref[tt] = (h * _gelu_tanh(gb_ref[tt])).astype(yb_ref.dtype)
    hl_ref[...] = h


def _lru_sample(xb, gb, conv_state, h0, conv_w, conv_b, gaw, gab, gxw, gxb, lru_lambda):
    s_len, bsz, w = xb.shape
    kern = functools.partial(_lru_sample_kernel, conv_w=conv_w.shape[0])
    return pl.pallas_call(
        kern,
        out_shape=[jax.ShapeDtypeStruct((s_len, bsz, w), BF16), jax.ShapeDtypeStruct((bsz, w), F32)],
        compiler_params=pltpu.CompilerParams(vmem_limit_bytes=VMEM_MB * 2**20),
        name="lru_sample",
    )(xb, gb, conv_state, h0, conv_w, conv_b, gaw, gab, gxw, gxb, lru_lambda)


def _out_proj_kernel(o_ref, yb_ref, x_ref, w_ref, g_ref, b_ref, out_ref, *, alpha):
    n_heads, _, hw = o_ref.shape
    aw = n_heads * hw
    mixed = _dot(yb_ref[...], w_ref[aw:, :])
    for h in range(n_heads):
        mixed = mixed + _dot(o_ref[h], w_ref[hw * h:hw * (h + 1), :])
    out_ref[...] = _layer_norm(alpha * x_ref[...] + mixed, g_ref[...], b_ref[...])


def _out_proj(o, yb, x, w, g, b, *, alpha, tm):
    r, d = x.shape
    n_heads, _, hw = o.shape
    bw = yb.shape[1]
    full = lambda shape: pl.BlockSpec(shape, lambda i: (0,) * len(shape))
    return pl.pallas_call(
        functools.partial(_out_proj_kernel, alpha=alpha),
        grid=(r // tm,),
        in_specs=[pl.BlockSpec((n_heads, tm, hw), lambda i: (0, i, 0)),
                  pl.BlockSpec((tm, bw), lambda i: (i, 0)),
                  pl.BlockSpec((tm, d), lambda i: (i, 0)),
                  full((n_heads * hw + bw, d)), full((1, d)), full((1, d))],
        out_specs=pl.BlockSpec((tm, d), lambda i: (i, 0)),
        out_shape=jax.ShapeDtypeStruct((r, d), F32),
        compiler_params=_params(("parallel",)),
        name="out_proj",
    )(o, yb, x, w, g, b)


def _mlp_kernel(x_ref, w1_ref, w2_ref, g_ref, b_ref, out_ref, xb_sc, acc_sc, *, alpha):
    f = pl.program_id(1)

    @pl.when(f == 0)
    def _():
        xb_sc[...] = x_ref[...].astype(BF16)
        acc_sc[...] = jnp.zeros_like(acc_sc)

    h = jnp.maximum(_dot(xb_sc[...], w1_ref[...]), 0.0)
    acc_sc[...] += _dot((h * h).astype(BF16), w2_ref[...])

    @pl.when(f == pl.num_programs(1) - 1)
    def _():
        out_ref[...] = _layer_norm(alpha * x_ref[...] + acc_sc[...], g_ref[...], b_ref[...])


def _mlp_final_kernel(x_ref, w1_ref, w2_ref, g_ref, b_ref, yp_ref, ys_ref, xb_sc, acc_sc, *, alpha,
                      n_meta, n_tail, s_start, n_s):
    i = pl.program_id(0)
    f = pl.program_id(1)
    last = pl.num_programs(0) - 1

    @pl.when(f == 0)
    def _():
        xb_sc[...] = x_ref[...].astype(BF16)
        acc_sc[...] = jnp.zeros_like(acc_sc)

    h = jnp.maximum(_dot(xb_sc[...], w1_ref[...]), 0.0)
    acc_sc[...] += _dot((h * h).astype(BF16), w2_ref[...])

    def result():
        return _layer_norm(alpha * x_ref[...] + acc_sc[...], g_ref[...], b_ref[...])

    @pl.when(jnp.logical_and(f == pl.num_programs(1) - 1, i < last))
    def _():
        yp_ref[...] = result()

    @pl.when(jnp.logical_and(f == pl.num_programs(1) - 1, i == last))
    def _():
        res = result()
        yp_ref[0:n_tail, :] = res[n_meta:n_meta + n_tail, :]
        ys_ref[...] = res[s_start:s_start + n_s, :]


def _mlp(x, w1, w2, g, b, *, alpha, tm, tf):
    r, d = x.shape
    ff = w1.shape[1]
    vec = pl.BlockSpec((1, d), lambda i, f: (0, 0))
    return pl.pallas_call(
        functools.partial(_mlp_kernel, alpha=alpha),
        grid=(r // tm, ff // tf),
        in_specs=[pl.BlockSpec((tm, d), lambda i, f: (i, 0)),
                  pl.BlockSpec((d, tf), lambda i, f: (0, f)),
                  pl.BlockSpec((tf, d), lambda i, f: (f, 0)),
                  vec, vec],
        out_specs=pl.BlockSpec((tm, d), lambda i, f: (i, 0)),
        out_shape=jax.ShapeDtypeStruct((r, d), F32),
        scratch_shapes=[pltpu.VMEM((tm, d), BF16), pltpu.VMEM((tm, d), F32)],
        compiler_params=_params(("parallel", "arbitrary")),
        name="mlp",
    )(x, w1, w2, g, b)


def _mlp_final(x, w1, w2, g, b, *, alpha, tm, tf, n_meta, seq, n_s):
    r, d = x.shape
    ff = w1.shape[1]
    n_i = r // tm
    last_row = (n_i - 1) * tm
    n_tail = seq - last_row
    s_start = n_meta + seq - last_row
    assert pl.cdiv(seq, tm) == n_i and 0 < n_tail <= tm - n_meta and s_start + n_s <= tm
    assert n_meta % 8 == 0
    vec = pl.BlockSpec((1, d), lambda i, f: (0, 0))
    x_rows = lambda i, f: (pl.multiple_of(jnp.where(i < n_i - 1, n_meta + tm * i, last_row), 8), 0)
    kern = functools.partial(_mlp_final_kernel, alpha=alpha, n_meta=n_meta, n_tail=n_tail,
                             s_start=s_start, n_s=n_s)
    return pl.pallas_call(
        kern,
        grid=(n_i, ff // tf),
        in_specs=[pl.BlockSpec((pl.Element(tm), pl.Element(d)), x_rows),
                  pl.BlockSpec((d, tf), lambda i, f: (0, f)),
                  pl.BlockSpec((tf, d), lambda i, f: (f, 0)),
                  vec, vec],
        out_specs=[pl.BlockSpec((tm, d), lambda i, f: (i, 0)),
                   pl.BlockSpec((n_s, d), lambda i, f: (0, 0))],
        out_shape=[jax.ShapeDtypeStruct((seq, d), F32), jax.ShapeDtypeStruct((n_s, d), F32)],
        scratch_shapes=[pltpu.VMEM((tm, d), BF16), pltpu.VMEM((tm, d), F32)],
        compiler_params=_params(("arbitrary", "arbitrary")),
        name="mlp_final",
    )(x, w1, w2, g, b)


def _pool_project(ms, x, w_ref, sc_ref, g_ref, b_ref, alpha):
    y = jnp.concatenate([_dot(ms[gi].astype(BF16), w_ref[gi]) for gi in range(len(ms))], axis=1)
    return _layer_norm(alpha * x + y * sc_ref[...], g_ref[...], b_ref[...])


def _pool_prompt_kernel(x_ref, w_ref, sc_ref, g_ref, b_ref, out_ref, *bufs, tm, alpha, pad):
    i = pl.program_id(0)
    n_g = len(POOL_WINDOWS)
    gw = x_ref.shape[1] // n_g

    @pl.when(i == 0)
    def _():
        for buf in bufs:
            buf[0:pad, :] = jnp.zeros((pad, buf.shape[1]), F32)

    x = x_ref[...]
    pos = i * tm + lax.broadcasted_iota(jnp.int32, (tm, 1), 0)
    ms = []
    cur = x
    for k, win in enumerate(POOL_WINDOWS):
        half = win // 2
        buf = bufs[k]
        buf[pad:pad + tm, :] = cur
        ws = cur + buf[pad - half:pad - half + tm, :]
        buf[0:pad, :] = buf[tm:tm + pad, :]
        cnt = jnp.minimum(win, pos + 1).astype(F32)
        ms.append(ws[:, :gw] / cnt - x[:, k * gw:(k + 1) * gw])
        if k + 1 < n_g:
            cur = ws[:, gw:]
    out_ref[...] = _pool_project(ms, x, w_ref, sc_ref, g_ref, b_ref, alpha)


def _pool_prompt(x, w, sc, g, b, *, alpha, tm):
    r, d = x.shape
    n_g, gw, _ = w.shape
    pad = 16
    assert POOL_WINDOWS == tuple(2 ** (k + 1) for k in range(n_g)) and max(POOL_WINDOWS) // 2 <= pad
    full = lambda shape: pl.BlockSpec(shape, lambda i: (0,) * len(shape))
    return pl.pallas_call(
        functools.partial(_pool_prompt_kernel, tm=tm, alpha=alpha, pad=pad),
        grid=(r // tm,),
        in_specs=[pl.BlockSpec((tm, d), lambda i: (i, 0)),
                  full((n_g, gw, gw)), full((1, d)), full((1, d)), full((1, d))],
        out_specs=pl.BlockSpec((tm, d), lambda i: (i, 0)),
        out_shape=jax.ShapeDtypeStruct((r, d), F32),
        scratch_shapes=[pltpu.VMEM((tm + pad, (n_g - k) * gw), F32) for k in range(n_g)],
        compiler_params=_params(("arbitrary",)),
        name="pool_prompt",
    )(x, w, sc, g, b)


def _pool_sample_kernel(x_ref, st_ref, w_ref, sc_ref, g_ref, b_ref, out_ref, *, alpha):
    s_len, bsz, d = x_ref.shape
    n_buf = st_ref.shape[0]
    gw = d // len(POOL_WINDOWS)
    ext = [st_ref[jj] for jj in range(n_buf)] + [x_ref[tt] for tt in range(s_len)]
    x = jnp.concatenate(ext[n_buf:], axis=0)
    ms = []
    for gi, win in enumerate(POOL_WINDOWS):
        cols = slice(gi * gw, (gi + 1) * gw)
        rows = []
        for tt in range(s_len):
            ws = ext[n_buf + tt][:, cols]
            for kk in range(1, win):
                ws = ws + ext[n_buf + tt - kk][:, cols]
            rows.append(ws / float(win) - ext[n_buf + tt][:, cols])
        ms.append(jnp.concatenate(rows, axis=0))
    y = _pool_project(ms, x, w_ref, sc_ref, g_ref, b_ref, alpha)
    for tt in range(s_len):
        out_ref[tt] = y[tt * bsz:(tt + 1) * bsz, :]


def _pool_sample(x, state, w, sc, g, b, *, alpha):
    return pl.pallas_call(
        functools.partial(_pool_sample_kernel, alpha=alpha),
        out_shape=jax.ShapeDtypeStruct(x.shape, F32),
        compiler_params=pltpu.CompilerParams(vmem_limit_bytes=VMEM_MB * 2**20),
        name="pool_sample",
    )(x, state, w, sc, g, b)


def kernel(x_prompt, x_sample, cache_k, cache_v, state_conv, state_lru, state_pool, page_table, meta_tokens, w_in, lam_q1, lam_k1, lam_q2, lam_k2, sub_norm_g, conv_w, conv_b, gate_a_w, gate_a_b, gate_x_w, gate_x_b, lru_lambda, w_out_ab, pool_w, pool_scale, mix_ln_g, mix_ln_b, w_ff1, w_ff2, ff_ln_g, ff_ln_b):
    n_prompt, seq, d = x_prompt.shape
    bsz, s_len, _ = x_sample.shape
    depth = w_ff1.shape[0]
    assert n_prompt == 1 and depth == 2
    n_meta = meta_tokens.shape[0]
    n_heads, hd2 = cache_k.shape[-2:]
    hd = hd2 // 2
    aw = n_heads * hd2
    bw = state_lru.shape[-1]
    assert aw == bw and w_in.shape[-1] == 3 * aw + 2 * bw
    page = cache_k.shape[2]
    n_buf = state_pool.shape[2]
    cw = conv_w.shape[1]
    assert s_len >= cw - 1
    alpha = (2.0 * depth) ** 0.25
    lam_init = 0.8 - 0.6 * math.exp(-0.3 * 0)

    t_len = n_meta + seq
    n_s = bsz * s_len
    rows = -(-(t_len + n_s) // ROW_ALIGN) * ROW_ALIGN
    xs_tm = x_sample.transpose(1, 0, 2).reshape(n_s, d)
    x0 = jnp.concatenate([meta_tokens, x_prompt[0], xs_tm,
                          jnp.zeros((rows - t_len - n_s, d), F32)], axis=0)

    row2 = lambda v: v.reshape(1, -1)
    lam_params = (row2(lam_q1[0]), row2(lam_k1[0]), row2(lam_q2[0]), row2(lam_k2[0]))
    sub_g = row2(sub_norm_g[0])

    assert rows - T_ATT < t_len
    zt, zb, qk, vb, k_rows, v_rows = _in_proj(x0, w_in[0].astype(BF16), t_len=t_len, tm=T_ATT,
                                              tn=TN_PROJ, n_heads=n_heads, hd=hd)
    s0 = t_len - (rows - T_ATT)
    assert s0 + n_s <= T_ATT
    zs = jnp.concatenate([zt[s0:s0 + n_s], zb[t_len:t_len + n_s]], axis=1)

    d_ff = w_ff1.shape[2]
    w1_all, w2_all = w_ff1.reshape(depth * d, d_ff), w_ff2.reshape(depth * d_ff, d)
    o, w1b0, w2b0, w_out_b = _attn_prompt(
        qk, vb, lam_params, sub_g, t=T_ATT, hd=hd, lam_init=lam_init,
        cast=((w1_all, 0, d), (w2_all, 0, d_ff), (w_out_ab[0], 0, d)))

    split = lambda c: zs[:, c * aw:(c + 1) * aw].reshape(s_len, bsz, n_heads, 2, hd)
    nr = 2 * n_heads * s_len
    q_s = split(0).transpose(1, 3, 2, 0, 4).reshape(bsz, nr, hd)
    kn = split(1).transpose(1, 0, 3, 2, 4)[:, :, :, :, None, :]
    kn = jnp.broadcast_to(kn, (bsz, s_len, 2, n_heads, s_len, hd)).reshape(bsz, s_len, nr, hd)
    vn = split(2).transpose(1, 0, 3, 2, 4)[:, :, :, None, :, None, :]
    vn = jnp.broadcast_to(vn, (bsz, s_len, 2, 2, n_heads, s_len, hd)).reshape(bsz, s_len, 2 * nr, hd)
    page_rows = lambda c: (c[0].reshape(-1, page, n_heads, 2, hd).transpose(0, 1, 3, 2, 4)
                           .reshape(-1, page * 2 * n_heads, hd))
    o_s = _attn_sample(page_table, q_s, kn, vn, page_rows(cache_k), page_rows(cache_v), lam_params,
                       sub_g, n_heads=n_heads, hd=hd, s_len=s_len, page=page,
                       gp=min(PAGES_PER_STEP, page_table.shape[1]), lam_init=lam_init)
    o_s = o_s.reshape(bsz, n_heads, s_len, hd2).transpose(1, 2, 0, 3).reshape(n_heads, n_s, hd2)
    o = lax.dynamic_update_slice(o, o_s.astype(BF16), (0, t_len, 0))

    gaw, gxw = gate_a_w[0].astype(BF16), gate_x_w[0].astype(BF16)
    lru_args = (conv_w[0], row2(conv_b[0]), gaw, row2(gate_a_b[0]), gxw, row2(gate_x_b[0]),
                row2(lru_lambda[0]))
    yb, h_last_p, w1b1, w2b1 = _lru_prompt(zb, *lru_args, t_len=t_len, tm=TM_LRU,
                                           cast=((w1_all, d, d), (w2_all, d_ff, d_ff)))
    xb_s = zs[:, 3 * aw:3 * aw + bw].reshape(s_len, bsz, bw)
    gb_s = zs[:, 3 * aw + bw:].reshape(s_len, bsz, bw)
    yb_s, h_last_s = _lru_sample(xb_s, gb_s, state_conv[0].transpose(1, 0, 2), state_lru[0], *lru_args)
    yb = lax.dynamic_update_slice(yb, yb_s.reshape(n_s, bw), (t_len, 0))

    x1 = _out_proj(o, yb, x0, w_out_b, row2(mix_ln_g[0]), row2(mix_ln_b[0]), alpha=alpha, tm=TM_OUT)
    x2 = _mlp(x1, w1b0, w2b0, row2(ff_ln_g[0]), row2(ff_ln_b[0]), alpha=alpha, tm=TM_MLP, tf=TF_MLP)

    pool_args = (pool_w[0].astype(BF16), row2(pool_scale[0]), row2(mix_ln_g[1]), row2(mix_ln_b[1]))
    x3 = _pool_prompt(x2, *pool_args, alpha=alpha, tm=TM_POOL)
    x2_s = x2[t_len:t_len + n_s].reshape(s_len, bsz, d)
    x3_s = _pool_sample(x2_s, state_pool[0].transpose(1, 0, 2), *pool_args, alpha=alpha)
    x3 = lax.dynamic_update_slice(x3, x3_s.reshape(n_s, d), (t_len, 0))
    y_p, y_s = _mlp_final(x3, w1b1, w2b1, row2(ff_ln_g[1]), row2(ff_ln_b[1]), alpha=alpha, tm=TM_MLP,
                          tf=TF_MLP, n_meta=n_meta, seq=seq, n_s=n_s)

    to_bt = lambda v: v.reshape(s_len, bsz, -1).transpose(1, 0, 2)
    kv_p = lambda v: (v.reshape(t_len, 2, n_heads, hd).transpose(0, 2, 1, 3)
                      .reshape(1, 1, t_len, n_heads, hd2))
    kv_s = lambda c: to_bt(zs[:, c * aw:(c + 1) * aw]).reshape(1, bsz, s_len, n_heads, hd2)
    y_prompt = y_p[None]
    y_sample = to_bt(y_s)
    new_conv_prompt = zb[t_len - (cw - 1):t_len, :bw][None, None]
    new_lru_prompt = h_last_p[None]
    new_pool_prompt = x2[t_len - n_buf:t_len][None, None]
    new_conv_sample = xb_s[s_len - (cw - 1):].transpose(1, 0, 2)[None]
    new_lru_sample = h_last_s[None]
    new_pool_sample = jnp.concatenate([state_pool[0], x2_s.transpose(1, 0, 2)], axis=1)[:, -n_buf:][None]
    return (y_prompt, y_sample, kv_p(k_rows), kv_p(v_rows), new_conv_prompt, new_lru_prompt, new_pool_prompt,
            kv_s(1), kv_s(2), new_conv_sample, new_lru_sample, new_pool_sample)
```

```python
import functools
import math

import jax
import jax.numpy as jnp
from jax import lax
from jax.experimental import pallas as pl
from jax.experimental.pallas import tpu as pltpu

F32 = jnp.float32
BF16 = jnp.bfloat16

LN_EPS = 1e-5
RMS_EPS = 1e-5
LRU_C = 8.0
POOL_WINDOWS = (2, 4, 8, 16)
NEG = -1e30
LOG2E = 1.4426950408889634
LANES = 128
N_AUG = 3

ROW_ALIGN = 768
T_ATT = 768
TN_PROJ = 1024
TM_LRU = 256
TM_OUT = 384
TM_MLP = 768
TF_MLP = 512
TM_POOL = 256
ATT_HEAD_UNROLL = 2
PAGES_PER_STEP = 8
PAGE_RING = 3
VMEM_MB = 56
VMEM_MB_IN_PROJ = 60


def _params(sem, mb=VMEM_MB):
    return pltpu.CompilerParams(dimension_semantics=sem, vmem_limit_bytes=mb * 2**20)


def _layer_norm(y, g, b):
    mu = jnp.mean(y, axis=-1, keepdims=True)
    d = y - mu
    var = jnp.mean(d * d, axis=-1, keepdims=True)
    return d * lax.rsqrt(var + LN_EPS) * g + b


def _dot(a, b):
    return jnp.dot(a, b, preferred_element_type=F32)


def _dot_nt(a, b):
    return lax.dot_general(a, b, (((1,), (1,)), ((), ())), preferred_element_type=F32)


def _sigmoid(x):
    return 1.0 / (1.0 + jnp.exp(-x))


def _gelu_tanh(x):
    c = math.sqrt(2.0 / math.pi)
    return 0.5 * x * (1.0 + jnp.tanh(c * (x + 0.044715 * (x * x * x))))


def _lambda_value(lq1, lk1, lq2, lk2, lam_init):
    s1 = jnp.sum(lq1 * lk1, axis=-1, keepdims=True)
    s2 = jnp.sum(lq2 * lk2, axis=-1, keepdims=True)
    return jnp.exp(s1) - jnp.exp(s2) + lam_init


def _div_pow2(x, n):
    assert n & (n - 1) == 0
    return lax.shift_right_logical(x, n.bit_length() - 1)


def _mod_pow2(x, n):
    assert n & (n - 1) == 0
    return lax.bitwise_and(x, n - 1)


BF16_ROWS = 16


def _cast_specs(job, n_steps, step_of):
    mat, row_start, rows = job
    cols = mat.shape[1]
    units = rows // BF16_ROWS
    assert units * BF16_ROWS == rows
    n_blocks = max(k for k in range(1, n_steps + 1) if units % k == 0)
    br = rows // n_blocks
    assert row_start % br == 0
    first = row_start // br
    block = lambda *idx: jnp.minimum(step_of(*idx), n_blocks - 1)
    return (pl.BlockSpec((br, cols), lambda *idx: (first + block(*idx), 0)),
            pl.BlockSpec((br, cols), lambda *idx: (block(*idx), 0)))


def _cast_shape(job):
    mat, _, rows = job
    return jax.ShapeDtypeStruct((rows, mat.shape[1]), BF16)


def _cast_blocks(srcs, dsts):
    for src, dst in zip(srcs, dsts):
        dst[...] = src[...].astype(BF16)


def _head_slope(h, n_heads):
    return 2.0 ** (-8.0 * (h + 1) / n_heads)


def _in_proj_kernel(x_ref, w_ref, zt_ref, zb_ref, qk_ref, vb_ref, kr_ref, vr_ref, *, n_heads, hd):
    i = pl.program_id(0)
    j = pl.program_id(1)
    z = _dot(x_ref[...].astype(BF16), w_ref[...])
    tm = z.shape[0]

    @pl.when(j >= 3)
    def _():
        zb_ref[...] = z

    @pl.when(jnp.logical_and(i == pl.num_programs(0) - 1, j < 3))
    def _():
        zt_ref[...] = z

    lane = lax.broadcasted_iota(jnp.int32, (tm, hd), 1)

    def cache_rows(dst_ref):
        for c in range(2):
            for h in range(n_heads):
                col = (2 * h + c) * hd
                dst_ref[pl.ds(c * n_heads + h, tm, stride=2 * n_heads), :] = z[:, col:col + hd]

    @pl.when(j == 0)
    def _():
        ones = jnp.where(lane < N_AUG, 1.0, 0.0).astype(BF16)
        q_scale = hd ** -0.5 * LOG2E
        for h in range(n_heads):
            for c in range(2):
                col = (2 * h + c) * hd
                qk_ref[0, h, :, 2 * hd * c:2 * hd * c + hd] = (z[:, col:col + hd] * q_scale).astype(BF16)
                qk_ref[0, h, :, 2 * hd * c + hd:2 * hd * (c + 1)] = ones

    @pl.when(j == 1)
    def _():
        row = lax.broadcasted_iota(jnp.int32, (tm, hd), 0).astype(F32)
        for h in range(n_heads):
            a = row * (_head_slope(h, n_heads) * LOG2E)
            hi = a.astype(BF16).astype(F32)
            mid = (a - hi).astype(BF16).astype(F32)
            lo = a - hi - mid
            aug = jnp.where(lane == 0, hi, jnp.where(lane == 1, mid, jnp.where(lane == 2, lo, 0.0)))
            aug = aug.astype(BF16)
            for c in range(2):
                col = (2 * h + c) * hd
                qk_ref[0, h, :, 2 * hd * c:2 * hd * c + hd] = z[:, col:col + hd].astype(BF16)
                qk_ref[0, h, :, 2 * hd * c + hd:2 * hd * (c + 1)] = aug
        cache_rows(kr_ref)

    @pl.when(j == 2)
    def _():
        for h in range(n_heads):
            vb_ref[h] = z[:, 2 * hd * h:2 * hd * (h + 1)].astype(BF16)
        cache_rows(vr_ref)


def _in_proj(x, w, *, t_len, tm, tn, n_heads, hd):
    r, d = x.shape
    n = w.shape[1]
    assert hd == LANES and tn == 2 * hd * n_heads and n == 5 * tn
    kr = 2 * n_heads
    n_i = r // tm
    cache_blk = pl.BlockSpec((tm * kr, hd), lambda i, j: (i, 0))
    cache_shape = jax.ShapeDtypeStruct((t_len * kr, hd), F32)
    return pl.pallas_call(
        functools.partial(_in_proj_kernel, n_heads=n_heads, hd=hd),
        grid=(n_i, n // tn),
        in_specs=[pl.BlockSpec((tm, d), lambda i, j: (i, 0)),
                  pl.BlockSpec((d, tn), lambda i, j: (0, j))],
        out_specs=[
                   pl.BlockSpec((tm, tn), lambda i, j: (0, jnp.where(i == n_i - 1, jnp.minimum(j, 2), 0))),
                   pl.BlockSpec((tm, tn), lambda i, j: (i, jnp.maximum(j - 3, 0))),
                   pl.BlockSpec((1, n_heads, tm, 4 * hd), lambda i, j: (jnp.minimum(j, 1), 0, i, 0)),
                   pl.BlockSpec((n_heads, tm, 2 * hd), lambda i, j: (0, i, 0)),
                   cache_blk, cache_blk],
        out_shape=[jax.ShapeDtypeStruct((tm, 3 * tn), F32),
                   jax.ShapeDtypeStruct((r, 2 * tn), F32),
                   jax.ShapeDtypeStruct((2, n_heads, r, 4 * hd), BF16),
                   jax.ShapeDtypeStruct((n_heads, r, 2 * hd), BF16),
                   cache_shape, cache_shape],
        compiler_params=_params(("arbitrary", "arbitrary"), mb=VMEM_MB_IN_PROJ),
        name="in_proj",
    )(x, w)


def _attn_prompt_kernel(qi_ref, kj_ref, qa_ref, ka_ref, vb_ref, sl_ref, lq1_ref, lk1_ref, lq2_ref,
                        lk2_ref, g_ref, *rest, t, n_heads, hd, lam_init, n_cast):
    cast_src = rest[:n_cast]
    o_ref = rest[n_cast]
    cast_dst = rest[n_cast + 1:2 * n_cast + 1]
    m_sc, l_sc, acc_sc = rest[2 * n_cast + 1:]
    _cast_blocks(cast_src, cast_dst)
    p = pl.program_id(0)
    i = qi_ref[p]
    j = kj_ref[p]

    @pl.when(j == 0)
    def _():
        m_sc[...] = jnp.full_like(m_sc, NEG)
        l_sc[...] = jnp.zeros_like(l_sc)
        acc_sc[...] = jnp.zeros_like(acc_sc)

    def tile(masked):
        if masked:
            keep = (lax.broadcasted_iota(jnp.int32, (t, t), 1)
                    <= lax.broadcasted_iota(jnp.int32, (t, t), 0))
        tile_dist = ((j - i) * t).astype(F32)

        def head(h, carry):
            shift = sl_ref[h][:, 0:1] * tile_dist
            for c in range(2):
                s = _dot_nt(qa_ref[h, :, 2 * hd * c:2 * hd * (c + 1)],
                            ka_ref[h, :, 2 * hd * c:2 * hd * (c + 1)])
                if masked:
                    s = jnp.where(keep, s, NEG)
                idx = 2 * h + c
                m_old = m_sc[idx]
                m_new = jnp.maximum(m_old, jnp.max(s, axis=1, keepdims=True) + shift)
                alpha = jnp.exp2(m_old - m_new)
                pr = jnp.exp2(s - (m_new - shift))
                l_sc[idx] = alpha * l_sc[idx] + jnp.sum(pr, axis=1, keepdims=True)
                acc_sc[idx] = alpha * acc_sc[idx] + _dot(pr.astype(BF16), vb_ref[h])
                m_sc[idx] = m_new
            return carry

        lax.fori_loop(0, n_heads, head, 0, unroll=ATT_HEAD_UNROLL)

    @pl.when(j < i)
    def _():
        tile(False)

    @pl.when(j == i)
    def _():
        tile(True)
        lam = _lambda_value(lq1_ref[...], lk1_ref[...], lq2_ref[...], lk2_ref[...], lam_init)
        g = g_ref[...]
        for h in range(n_heads):
            o1 = acc_sc[2 * h] / l_sc[2 * h]
            o2 = acc_sc[2 * h + 1] / l_sc[2 * h + 1]
            oh = o1 - lam * o2
            oh = oh * lax.rsqrt(jnp.mean(oh * oh, axis=-1, keepdims=True) + RMS_EPS)
            o_ref[h] = (oh * g * (1.0 - lam_init)).astype(o_ref.dtype)


def _attn_prompt(qk, vb, lam_params, sub_g, *, t, hd, lam_init, cast=()):
    _, n_heads, r, _ = qk.shape
    nq = r // t
    pairs = [(i, j) for i in range(nq) for j in range(i + 1)]
    qi = jnp.asarray([i for i, _ in pairs], jnp.int32)
    kj = jnp.asarray([j for _, j in pairs], jnp.int32)
    slopes = jnp.asarray([[[_head_slope(h, n_heads) * LOG2E] * LANES] for h in range(n_heads)], F32)
    vec = lambda n: pl.BlockSpec((1, n), lambda p, qi, kj: (0, 0))
    cast_specs = [_cast_specs(job, len(pairs), lambda p, qi, kj: p) for job in cast]
    grid_spec = pltpu.PrefetchScalarGridSpec(
        num_scalar_prefetch=2,
        grid=(len(pairs),),
        in_specs=[pl.BlockSpec((None, n_heads, t, 4 * hd), lambda p, qi, kj: (0, 0, qi[p], 0)),
                  pl.BlockSpec((None, n_heads, t, 4 * hd), lambda p, qi, kj: (1, 0, kj[p], 0)),
                  pl.BlockSpec((n_heads, t, 2 * hd), lambda p, qi, kj: (0, kj[p], 0)),
                  pl.BlockSpec((n_heads, 1, LANES), lambda p, qi, kj: (0, 0, 0)),
                  vec(hd), vec(hd), vec(hd), vec(hd), vec(2 * hd)] + [s[0] for s in cast_specs],
        out_specs=[pl.BlockSpec((n_heads, t, 2 * hd), lambda p, qi, kj: (0, qi[p], 0))]
                  + [s[1] for s in cast_specs],
        scratch_shapes=[pltpu.VMEM((2 * n_heads, t, 1), F32),
                        pltpu.VMEM((2 * n_heads, t, 1), F32),
                        pltpu.VMEM((2 * n_heads, t, 2 * hd), F32)],
    )
    kern = functools.partial(_attn_prompt_kernel, t=t, n_heads=n_heads, hd=hd, lam_init=lam_init,
                             n_cast=len(cast))
    return pl.pallas_call(
        kern, grid_spec=grid_spec,
        out_shape=[jax.ShapeDtypeStruct((n_heads, r, 2 * hd), BF16)] + [_cast_shape(job) for job in cast],
        compiler_params=_params(("arbitrary",)),
        name="attn_prompt",
    )(qi, kj, qk, qk, vb, slopes, *lam_params, sub_g, *[job[0] for job in cast])


def _attn_sample_kernel(pt_ref, q_ref, kn_ref, vn_ref, *rest, n_heads, hd, s_len, past_len, page,
                        gp, lam_init):
    (ck_hbm, cv_hbm, lq1_ref, lk1_ref, lq2_ref, lk2_ref, g_ref, o_ref,
     qb_sc, bias_sc, m_sc, l_sc, acc_sc, kbuf, vbuf, sem) = rest
    p = pl.program_id(1)
    n_p = pl.num_programs(1)
    step = pl.program_id(0) * n_p + p
    n_steps = pl.num_programs(0) * n_p

    def page_copies(st):
        slot = lax.rem(st, PAGE_RING)
        seq = lax.div(st, n_p)
        first = lax.rem(st, n_p) * gp
        out = []
        for gi in range(gp):
            pid = pt_ref[seq, first + gi]
            out.append(pltpu.make_async_copy(ck_hbm.at[pid], kbuf.at[slot, gi], sem.at[slot, 0]))
            out.append(pltpu.make_async_copy(cv_hbm.at[pid], vbuf.at[slot, gi], sem.at[slot, 1]))
        return out

    def start_step(st):
        for n, cp in enumerate(page_copies(st)):
            cp.start(priority=n % 2)

    @pl.when(step == 0)
    def _():
        for ahead in range(PAGE_RING - 1):
            @pl.when(ahead < n_steps)
            def _():
                start_step(step + ahead)

    @pl.when(step + PAGE_RING - 1 < n_steps)
    def _():
        start_step(step + PAGE_RING - 1)

    for cp in page_copies(step):
        cp.wait()
    slot = lax.rem(step, PAGE_RING)
    nr = 2 * n_heads * s_len
    kr = 2 * n_heads
    ncol = page * kr
    half = nr // 2
    scale = hd ** -0.5

    def row_info(shape):
        row = lax.broadcasted_iota(jnp.int32, shape, 0)
        tok = _mod_pow2(row, s_len)
        head = _mod_pow2(_div_pow2(row, s_len), n_heads)
        cmap = _div_pow2(row, s_len * n_heads)
        slope = jnp.zeros(shape, F32)
        for h in range(n_heads):
            slope = jnp.where(head == h, _head_slope(h, n_heads), slope)
        return tok, head, cmap, slope

    @pl.when(p == 0)
    def _():
        qb_sc[...] = (q_ref[0] * scale).astype(BF16)
        tok, head, cmap, slope = row_info((nr, ncol))
        col = lax.broadcasted_iota(jnp.int32, (nr, ncol), 1)
        key = _div_pow2(col, kr)
        own = _mod_pow2(col, kr) == cmap * n_heads + head
        bias_sc[...] = jnp.where(own, -slope * (past_len + tok - key).astype(F32), NEG)
        m_sc[...] = jnp.full_like(m_sc, NEG)
        l_sc[...] = jnp.zeros_like(l_sc)
        acc_sc[...] = jnp.zeros_like(acc_sc)

    tok1, _, _, slope1 = row_info((nr, 1))
    m = m_sc[...]
    l = l_sc[...]
    acc = acc_sc[...]
    scores, shifts = [], []
    m_new = m
    for gi in range(gp):
        kp = kbuf[slot, gi].astype(BF16)
        s = _dot_nt(qb_sc[...], kp) + bias_sc[...]
        shift = slope1 * ((p * gp + gi) * page).astype(F32)
        m_new = jnp.maximum(m_new, jnp.max(s, axis=1, keepdims=True) + shift)
        scores.append(s)
        shifts.append(shift)
    alpha = jnp.exp(m - m_new)
    l = alpha * l
    pv = jnp.zeros((2 * nr, hd), F32)
    for gi in range(gp):
        pr = jnp.exp(scores[gi] - (m_new - shifts[gi]))
        l = l + jnp.sum(pr, axis=1, keepdims=True)
        top, bot = pr[:half], pr[half:]
        chunks = range(ncol // LANES)
        top_r = jnp.concatenate(
            [pltpu.roll(top[:, LANES * cc:LANES * (cc + 1)], n_heads, 1) for cc in chunks], axis=1)
        bot_r = jnp.concatenate(
            [pltpu.roll(bot[:, LANES * cc:LANES * (cc + 1)], LANES - n_heads, 1) for cc in chunks], axis=1)
        lhs = jnp.concatenate([top, bot_r, top_r, bot], axis=0).astype(BF16)
        vp = vbuf[slot, gi].astype(BF16)
        pv = pv + _dot(lhs, vp)
    acc = jnp.concatenate([alpha, alpha], axis=0) * acc + pv
    m = m_new
    m_sc[...] = m
    l_sc[...] = l
    acc_sc[...] = acc

    @pl.when(p == pl.num_programs(1) - 1)
    def _():
        qf = q_ref[0] * scale
        s_new = []
        for t2 in range(s_len):
            st = jnp.sum(qf * kn_ref[0, t2], axis=1, keepdims=True)
            st = st - slope1 * (tok1 - t2).astype(F32)
            s_new.append(jnp.where(t2 <= tok1, st, NEG))
        m_fin = m
        for st in s_new:
            m_fin = jnp.maximum(m_fin, st)
        a_fin = jnp.exp(m - m_fin)
        l_fin = a_fin * l
        acc_f = jnp.concatenate([a_fin, a_fin], axis=0) * acc
        for t2 in range(s_len):
            pt = jnp.exp(s_new[t2] - m_fin)
            l_fin = l_fin + pt
            acc_f = acc_f + jnp.concatenate([pt, pt], axis=0) * vn_ref[0, t2]
        lam = _lambda_value(lq1_ref[...], lk1_ref[...], lq2_ref[...], lk2_ref[...], lam_init)
        halves = []
        for c2 in range(2):
            a = acc_f[c2 * nr:(c2 + 1) * nr] / l_fin
            halves.append(a[:half] - lam * a[half:])
        ms = (jnp.sum(halves[0] * halves[0], axis=1, keepdims=True)
              + jnp.sum(halves[1] * halves[1], axis=1, keepdims=True)) / (2.0 * hd)
        rs = lax.rsqrt(ms + RMS_EPS) * (1.0 - lam_init)
        for c2 in range(2):
            o_ref[0, :, c2 * hd:(c2 + 1) * hd] = halves[c2] * rs * g_ref[:, c2 * hd:(c2 + 1) * hd]


def _attn_sample(page_table, q, kn, vn, ck, cv, lam_params, sub_g, *, n_heads, hd, s_len, page,
                 gp, lam_init):
    bsz, nr, _ = q.shape
    n_pages = page_table.shape[1]
    assert n_pages % gp == 0 and hd == LANES and nr == 2 * n_heads * s_len
    ncol = ck.shape[1]
    vec = lambda n: pl.BlockSpec((1, n), lambda b, p, pt: (0, 0))
    in_hbm = pl.BlockSpec(memory_space=pl.ANY)
    grid_spec = pltpu.PrefetchScalarGridSpec(
        num_scalar_prefetch=1,
        grid=(bsz, n_pages // gp),
        in_specs=[pl.BlockSpec((1, nr, hd), lambda b, p, pt: (b, 0, 0)),
                  pl.BlockSpec((1, s_len, nr, hd), lambda b, p, pt: (b, 0, 0, 0)),
                  pl.BlockSpec((1, s_len, 2 * nr, hd), lambda b, p, pt: (b, 0, 0, 0)),
                  in_hbm, in_hbm,
                  vec(hd), vec(hd), vec(hd), vec(hd), vec(2 * hd)],
        out_specs=pl.BlockSpec((1, nr // 2, 2 * hd), lambda b, p, pt: (b, 0, 0)),
        scratch_shapes=[pltpu.VMEM((nr, hd), BF16),
                        pltpu.VMEM((nr, ncol), F32),
                        pltpu.VMEM((nr, 1), F32),
                        pltpu.VMEM((nr, 1), F32),
                        pltpu.VMEM((2 * nr, hd), F32),
                        pltpu.VMEM((PAGE_RING, gp, ncol, hd), F32),
                        pltpu.VMEM((PAGE_RING, gp, ncol, hd), F32),
                        pltpu.SemaphoreType.DMA((PAGE_RING, 2))],
    )
    kern = functools.partial(_attn_sample_kernel, n_heads=n_heads, hd=hd, s_len=s_len,
                             past_len=n_pages * page, page=page, gp=gp, lam_init=lam_init)
    return pl.pallas_call(
        kern, grid_spec=grid_spec,
        out_shape=jax.ShapeDtypeStruct((bsz, nr // 2, 2 * hd), F32),
        compiler_params=_params(("arbitrary", "arbitrary")),
        name="attn_sample",
    )(page_table, q, kn, vn, ck, cv, *lam_params, sub_g)


def _lru_gates(xc, gaw_ref, gab_ref, gxw_ref, gxb_ref, lam_ref):
    n_blk, bs, _ = gaw_ref.shape
    xcb = xc.astype(BF16)
    ra = jnp.concatenate([_dot(xcb[:, n * bs:(n + 1) * bs], gaw_ref[n]) for n in range(n_blk)], axis=1)
    rx = jnp.concatenate([_dot(xcb[:, n * bs:(n + 1) * bs], gxw_ref[n]) for n in range(n_blk)], axis=1)
    r = _sigmoid(ra + gab_ref[...])
    ig = _sigmoid(rx + gxb_ref[...])
    neg_lam = -lam_ref[...]
    softplus = jnp.maximum(neg_lam, 0.0) + jnp.log1p(jnp.exp(-jnp.abs(neg_lam)))
    log_a = -LRU_C * r * softplus
    a = jnp.exp(log_a)
    mult = jnp.sqrt(-jnp.tanh(log_a) * (a * a + 1.0))
    return a, ig, mult


def _lru_prompt_kernel(xb_ref, gb_ref, cw_ref, cb_ref, gaw_ref, gab_ref, gxw_ref, gxb_ref, lam_ref,
                       *rest, tm, conv_w, last_tile, last_row, n_cast):
    cast_src = rest[:n_cast]
    yb_ref, hl_ref = rest[n_cast:n_cast + 2]
    cast_dst = rest[n_cast + 2:2 * n_cast + 2]
    xbuf, hbuf, hcar = rest[2 * n_cast + 2:]
    _cast_blocks(cast_src, cast_dst)
    i = pl.program_id(0)
    w = xb_ref.shape[1]
    n_grp = tm // 8

    @pl.when(i == 0)
    def _():
        xbuf[0:8, :] = jnp.zeros((8, w), F32)
        hcar[...] = jnp.zeros_like(hcar)

    xbuf[8:8 + tm, :] = xb_ref[...]
    xc = cb_ref[...] + cw_ref[conv_w - 1:conv_w, :] * xb_ref[...]
    for jj in range(conv_w - 1):
        start = 8 - (conv_w - 1) + jj
        xc = xc + cw_ref[jj:jj + 1, :] * xbuf[start:start + tm, :]
    xbuf[0:8, :] = xbuf[tm:tm + 8, :]

    a, ig, mult = _lru_gates(xc, gaw_ref, gab_ref, gxw_ref, gxb_ref, lam_ref)
    grow = i * tm + lax.broadcasted_iota(jnp.int32, (tm, 1), 0)
    mult = jnp.where(grow == 0, 1.0, mult)
    u = xc * ig * mult

    a3 = a.reshape(n_grp, 8, w)
    u3 = u.reshape(n_grp, 8, w)
    sub = lax.broadcasted_iota(jnp.int32, (n_grp, 8, w), 1)
    for sh in (1, 2, 4):
        a_prev = pltpu.roll(a3, sh, axis=1)
        u_prev = pltpu.roll(u3, sh, axis=1)
        ok = sub >= sh
        u3 = jnp.where(ok, a3 * u_prev + u3, u3)
        a3 = jnp.where(ok, a3 * a_prev, a3)
    h_prev = hcar[...]
    for gi in range(n_grp):
        hg = a3[gi] * h_prev + u3[gi]
        hbuf[8 * gi:8 * (gi + 1), :] = hg
        h_prev = hg[7:8, :]
    hcar[...] = h_prev

    yb_ref[...] = (hbuf[...] * _gelu_tanh(gb_ref[...])).astype(yb_ref.dtype)

    @pl.when(i == last_tile)
    def _():
        hl_ref[...] = hbuf[last_row:last_row + 1, :]


def _lru_prompt(z, conv_w, conv_b, gaw, gab, gxw, gxb, lru_lambda, *, t_len, tm, cast=()):
    r = z.shape[0]
    w = conv_b.shape[1]
    cw = conv_w.shape[0]
    n_blk, bs, _ = gaw.shape
    n_steps = r // tm
    full = lambda shape: pl.BlockSpec(shape, lambda i: (0,) * len(shape))
    cast_specs = [_cast_specs(job, n_steps, lambda i: i) for job in cast]
    kern = functools.partial(_lru_prompt_kernel, tm=tm, conv_w=cw, n_cast=len(cast),
                             last_tile=(t_len - 1) // tm, last_row=(t_len - 1) % tm)
    return pl.pallas_call(
        kern,
        grid=(n_steps,),
        in_specs=[pl.BlockSpec((tm, w), lambda i: (i, 0)),
                  pl.BlockSpec((tm, w), lambda i: (i, 1)),
                  full((cw, w)), full((1, w)),
                  full((n_blk, bs, bs)), full((1, w)),
                  full((n_blk, bs, bs)), full((1, w)), full((1, w))] + [s[0] for s in cast_specs],
        out_specs=[pl.BlockSpec((tm, w), lambda i: (i, 0)), full((1, w))] + [s[1] for s in cast_specs],
        out_shape=[jax.ShapeDtypeStruct((r, w), BF16), jax.ShapeDtypeStruct((1, w), F32)]
                  + [_cast_shape(job) for job in cast],
        scratch_shapes=[pltpu.VMEM((tm + 8, w), F32), pltpu.VMEM((tm, w), F32),
                        pltpu.VMEM((1, w), F32)],
        compiler_params=_params(("arbitrary",)),
        name="lru_prompt",
    )(z, z, conv_w, conv_b, gaw, gab, gxw, gxb, lru_lambda, *[job[0] for job in cast])


def _lru_sample_kernel(xb_ref, gb_ref, cs_ref, h0_ref, cw_ref, cb_ref, gaw_ref, gab_ref, gxw_ref,
                       gxb_ref, lam_ref, yb_ref, hl_ref, *, conv_w):
    s_len, bsz, w = xb_ref.shape
    xp = [cs_ref[jj] for jj in range(conv_w - 1)] + [xb_ref[tt] for tt in range(s_len)]
    xcs = []
    for tt in range(s_len):
        xc = cb_ref[...] + cw_ref[0:1, :] * xp[tt]
        for jj in range(1, conv_w):
            xc = xc + cw_ref[jj:jj + 1, :] * xp[tt + jj]
        xcs.append(xc)
    xc = jnp.concatenate(xcs, axis=0)
    a, ig, mult = _lru_gates(xc, gaw_ref, gab_ref, gxw_ref, gxb_ref, lam_ref)
    u = xc * ig * mult
    h = h0_ref[...]
    for tt in range(s_len):
        rows = slice(tt * bsz, (tt + 1) * bsz)
        h = a[rows, :] * h + u[rows, :]
        yb_ref[tt] = (h * _gelu_tanh(gb_ref[tt])).astype(yb_ref.dtype)
    hl_ref[...] = h


def _lru_sample(xb, gb, conv_state, h0, conv_w, conv_b, gaw, gab, gxw, gxb, lru_lambda):
    s_len, bsz, w = xb.shape
    kern = functools.partial(_lru_sample_kernel, conv_w=conv_w.shape[0])
    return pl.pallas_call(
        kern,
        out_shape=[jax.ShapeDtypeStruct((s_len, bsz, w), BF16), jax.ShapeDtypeStruct((bsz, w), F32)],
        compiler_params=pltpu.CompilerParams(vmem_limit_bytes=VMEM_MB * 2**20),
        name="lru_sample",
    )(xb, gb, conv_state, h0, conv_w, conv_b, gaw, gab, gxw, gxb, lru_lambda)


def _out_proj_kernel(o_ref, yb_ref, x_ref, w_ref, g_ref, b_ref, out_ref, *, alpha):
    n_heads, _, hw = o_ref.shape
    aw = n_heads * hw
    mixed = _dot(yb_ref[...], w_ref[aw:, :])
    for h in range(n_heads):
        mixed = mixed + _dot(o_ref[h], w_ref[hw * h:hw * (h + 1), :])
    out_ref[...] = _layer_norm(alpha * x_ref[...] + mixed, g_ref[...], b_ref[...])


def _out_proj(o, yb, x, w, g, b, *, alpha, tm):
    r, d = x.shape
    n_heads, _, hw = o.shape
    bw = yb.shape[1]
    full = lambda shape: pl.BlockSpec(shape, lambda i: (0,) * len(shape))
    return pl.pallas_call(
        functools.partial(_out_proj_kernel, alpha=alpha),
        grid=(r // tm,),
        in_specs=[pl.BlockSpec((n_heads, tm, hw), lambda i: (0, i, 0)),
                  pl.BlockSpec((tm, bw), lambda i: (i, 0)),
                  pl.BlockSpec((tm, d), lambda i: (i, 0)),
                  full((n_heads * hw + bw, d)), full((1, d)), full((1, d))],
        out_specs=pl.BlockSpec((tm, d), lambda i: (i, 0)),
        out_shape=jax.ShapeDtypeStruct((r, d), F32),
        compiler_params=_params(("parallel",)),
        name="out_proj",
    )(o, yb, x, w, g, b)


def _mlp_kernel(x_ref, w1_ref, w2_ref, g_ref, b_ref, out_ref, xb_sc, acc_sc, *, alpha):
    f = pl.program_id(1)

    @pl.when(f == 0)
    def _():
        xb_sc[...] = x_ref[...].astype(BF16)
        acc_sc[...] = jnp.zeros_like(acc_sc)

    h = jnp.maximum(_dot(xb_sc[...], w1_ref[...]), 0.0)
    acc_sc[...] += _dot((h * h).astype(BF16), w2_ref[...])

    @pl.when(f == pl.num_programs(1) - 1)
    def _():
        out_ref[...] = _layer_norm(alpha * x_ref[...] + acc_sc[...], g_ref[...], b_ref[...])


def _mlp_final_kernel(x_ref, w1_ref, w2_ref, g_ref, b_ref, yp_ref, ys_ref, xb_sc, acc_sc, *, alpha,
                      n_meta, n_tail, s_start, n_s):
    i = pl.program_id(0)
    f = pl.program_id(1)
    last = pl.num_programs(0) - 1

    @pl.when(f == 0)
    def _():
        xb_sc[...] = x_ref[...].astype(BF16)
        acc_sc[...] = jnp.zeros_like(acc_sc)

    h = jnp.maximum(_dot(xb_sc[...], w1_ref[...]), 0.0)
    acc_sc[...] += _dot((h * h).astype(BF16), w2_ref[...])

    def result():
        return _layer_norm(alpha * x_ref[...] + acc_sc[...], g_ref[...], b_ref[...])

    @pl.when(jnp.logical_and(f == pl.num_programs(1) - 1, i < last))
    def _():
        yp_ref[...] = result()

    @pl.when(jnp.logical_and(f == pl.num_programs(1) - 1, i == last))
    def _():
        res = result()
        yp_ref[0:n_tail, :] = res[n_meta:n_meta + n_tail, :]
        ys_ref[...] = res[s_start:s_start + n_s, :]


def _mlp(x, w1, w2, g, b, *, alpha, tm, tf):
    r, d = x.shape
    ff = w1.shape[1]
    vec = pl.BlockSpec((1, d), lambda i, f: (0, 0))
    return pl.pallas_call(
        functools.partial(_mlp_kernel, alpha=alpha),
        grid=(r // tm, ff // tf),
        in_specs=[pl.BlockSpec((tm, d), lambda i, f: (i, 0)),
                  pl.BlockSpec((d, tf), lambda i, f: (0, f)),
                  pl.BlockSpec((tf, d), lambda i, f: (f, 0)),
                  vec, vec],
        out_specs=pl.BlockSpec((tm, d), lambda i, f: (i, 0)),
        out_shape=jax.ShapeDtypeStruct((r, d), F32),
        scratch_shapes=[pltpu.VMEM((tm, d), BF16), pltpu.VMEM((tm, d), F32)],
        compiler_params=_params(("parallel", "arbitrary")),
        name="mlp",
    )(x, w1, w2, g, b)


def _mlp_final(x, w1, w2, g, b, *, alpha, tm, tf, n_meta, seq, n_s):
    r, d = x.shape
    ff = w1.shape[1]
    n_i = r // tm
    last_row = (n_i - 1) * tm
    n_tail = seq - last_row
    s_start = n_meta + seq - last_row
    assert pl.cdiv(seq, tm) == n_i and 0 < n_tail <= tm - n_meta and s_start + n_s <= tm
    assert n_meta % 8 == 0
    vec = pl.BlockSpec((1, d), lambda i, f: (0, 0))
    x_rows = lambda i, f: (pl.multiple_of(jnp.where(i < n_i - 1, n_meta + tm * i, last_row), 8), 0)
    kern = functools.partial(_mlp_final_kernel, alpha=alpha, n_meta=n_meta, n_tail=n_tail,
                             s_start=s_start, n_s=n_s)
    return pl.pallas_call(
        kern,
        grid=(n_i, ff // tf),
        in_specs=[pl.BlockSpec((pl.Element(tm), pl.Element(d)), x_rows),
                  pl.BlockSpec((d, tf), lambda i, f: (0, f)),
                  pl.BlockSpec((tf, d), lambda i, f: (f, 0)),
                  vec, vec],
        out_specs=[pl.BlockSpec((tm, d), lambda i, f: (i, 0)),
                   pl.BlockSpec((n_s, d), lambda i, f: (0, 0))],
        out_shape=[jax.ShapeDtypeStruct((seq, d), F32), jax.ShapeDtypeStruct((n_s, d), F32)],
        scratch_shapes=[pltpu.VMEM((tm, d), BF16), pltpu.VMEM((tm, d), F32)],
        compiler_params=_params(("arbitrary", "arbitrary")),
        name="mlp_final",
    )(x, w1, w2, g, b)


def _pool_project(ms, x, w_ref, sc_ref, g_ref, b_ref, alpha):
    y = jnp.concatenate([_dot(ms[gi].astype(BF16), w_ref[gi]) for gi in range(len(ms))], axis=1)
    return _layer_norm(alpha * x + y * sc_ref[...], g_ref[...], b_ref[...])


def _pool_prompt_kernel(x_ref, w_ref, sc_ref, g_ref, b_ref, out_ref, *bufs, tm, alpha, pad):
    i = pl.program_id(0)
    n_g = len(POOL_WINDOWS)
    gw = x_ref.shape[1] // n_g

    @pl.when(i == 0)
    def _():
        for buf in bufs:
            buf[0:pad, :] = jnp.zeros((pad, buf.shape[1]), F32)

    x = x_ref[...]
    pos = i * tm + lax.broadcasted_iota(jnp.int32, (tm, 1), 0)
    ms = []
    cur = x
    for k, win in enumerate(POOL_WINDOWS):
        half = win // 2
        buf = bufs[k]
        buf[pad:pad + tm, :] = cur
        ws = cur + buf[pad - half:pad - half + tm, :]
        buf[0:pad, :] = buf[tm:tm + pad, :]
        cnt = jnp.minimum(win, pos + 1).astype(F32)
        ms.append(ws[:, :gw] / cnt - x[:, k * gw:(k + 1) * gw])
        if k + 1 < n_g:
            cur = ws[:, gw:]
    out_ref[...] = _pool_project(ms, x, w_ref, sc_ref, g_ref, b_ref, alpha)


def _pool_prompt(x, w, sc, g, b, *, alpha, tm):
    r, d = x.shape
    n_g, gw, _ = w.shape
    pad = 16
    assert POOL_WINDOWS == tuple(2 ** (k + 1) for k in range(n_g)) and max(POOL_WINDOWS) // 2 <= pad
    full = lambda shape: pl.BlockSpec(shape, lambda i: (0,) * len(shape))
    return pl.pallas_call(
        functools.partial(_pool_prompt_kernel, tm=tm, alpha=alpha, pad=pad),
        grid=(r // tm,),
        in_specs=[pl.BlockSpec((tm, d), lambda i: (i, 0)),
                  full((n_g, gw, gw)), full((1, d)), full((1, d)), full((1, d))],
        out_specs=pl.BlockSpec((tm, d), lambda i: (i, 0)),
        out_shape=jax.ShapeDtypeStruct((r, d), F32),
        scratch_shapes=[pltpu.VMEM((tm + pad, (n_g - k) * gw), F32) for k in range(n_g)],
        compiler_params=_params(("arbitrary",)),
        name="pool_prompt",
    )(x, w, sc, g, b)


def _pool_sample_kernel(x_ref, st_ref, w_ref, sc_ref, g_ref, b_ref, out_ref, *, alpha):
    s_len, bsz, d = x_ref.shape
    n_buf = st_ref.shape[0]
    gw = d // len(POOL_WINDOWS)
    ext = [st_ref[jj] for jj in range(n_buf)] + [x_ref[tt] for tt in range(s_len)]
    x = jnp.concatenate(ext[n_buf:], axis=0)
    ms = []
    for gi, win in enumerate(POOL_WINDOWS):
        cols = slice(gi * gw, (gi + 1) * gw)
        rows = []
        for tt in range(s_len):
            ws = ext[n_buf + tt][:, cols]
            for kk in range(1, win):
                ws = ws + ext[n_buf + tt - kk][:, cols]
            rows.append(ws / float(win) - ext[n_buf + tt][:, cols])
        ms.append(jnp.concatenate(rows, axis=0))
    y = _pool_project(ms, x, w_ref, sc_ref, g_ref, b_ref, alpha)
    for tt in range(s_len):
        out_ref[tt] = y[tt * bsz:(tt + 1) * bsz, :]


def _pool_sample(x, state, w, sc, g, b, *, alpha):
    return pl.pallas_call(
        functools.partial(_pool_sample_kernel, alpha=alpha),
        out_shape=jax.ShapeDtypeStruct(x.shape, F32),
        compiler_params=pltpu.CompilerParams(vmem_limit_bytes=VMEM_MB * 2**20),
        name="pool_sample",
    )(x, state, w, sc, g, b)


def kernel(x_prompt, x_sample, cache_k, cache_v, state_conv, state_lru, state_pool, page_table, meta_tokens, w_in, lam_q1, lam_k1, lam_q2, lam_k2, sub_norm_g, conv_w, conv_b, gate_a_w, gate_a_b, gate_x_w, gate_x_b, lru_lambda, w_out_ab, pool_w, pool_scale, mix_ln_g, mix_ln_b, w_ff1, w_ff2, ff_ln_g, ff_ln_b):
    n_prompt, seq, d = x_prompt.shape
    bsz, s_len, _ = x_sample.shape
    depth = w_ff1.shape[0]
    assert n_prompt == 1 and depth == 2
    n_meta = meta_tokens.shape[0]
    n_heads, hd2 = cache_k.shape[-2:]
    hd = hd2 // 2
    aw = n_heads * hd2
    bw = state_lru.shape[-1]
    assert aw == bw and w_in.shape[-1] == 3 * aw + 2 * bw
    page = cache_k.shape[2]
    n_buf = state_pool.shape[2]
    cw = conv_w.shape[1]
    assert s_len >= cw - 1
    alpha = (2.0 * depth) ** 0.25
    lam_init = 0.8 - 0.6 * math.exp(-0.3 * 0)

    t_len = n_meta + seq
    n_s = bsz * s_len
    rows = -(-(t_len + n_s) // ROW_ALIGN) * ROW_ALIGN
    xs_tm = x_sample.transpose(1, 0, 2).reshape(n_s, d)
    x0 = jnp.concatenate([meta_tokens, x_prompt[0], xs_tm,
                          jnp.zeros((rows - t_len - n_s, d), F32)], axis=0)

    row2 = lambda v: v.reshape(1, -1)
    lam_params = (row2(lam_q1[0]), row2(lam_k1[0]), row2(lam_q2[0]), row2(lam_k2[0]))
    sub_g = row2(sub_norm_g[0])

    assert rows - T_ATT < t_len
    zt, zb, qk, vb, k_rows, v_rows = _in_proj(x0, w_in[0].astype(BF16), t_len=t_len, tm=T_ATT,
                                              tn=TN_PROJ, n_heads=n_heads, hd=hd)
    s0 = t_len - (rows - T_ATT)
    assert s0 + n_s <= T_ATT
    zs = jnp.concatenate([zt[s0:s0 + n_s], zb[t_len:t_len + n_s]], axis=1)

    d_ff = w_ff1.shape[2]
    w1_all, w2_all = w_ff1.reshape(depth * d, d_ff), w_ff2.reshape(depth * d_ff, d)
    o, w1b0, w2b0, w_out_b = _attn_prompt(
        qk, vb, lam_params, sub_g, t=T_ATT, hd=hd, lam_init=lam_init,
        cast=((w1_all, 0, d), (w2_all, 0, d_ff), (w_out_ab[0], 0, d)))

    split = lambda c: zs[:, c * aw:(c + 1) * aw].reshape(s_len, bsz, n_heads, 2, hd)
    nr = 2 * n_heads * s_len
    q_s = split(0).transpose(1, 3, 2, 0, 4).reshape(bsz, nr, hd)
    kn = split(1).transpose(1, 0, 3, 2, 4)[:, :, :, :, None, :]
    kn = jnp.broadcast_to(kn, (bsz, s_len, 2, n_heads, s_len, hd)).reshape(bsz, s_len, nr, hd)
    vn = split(2).transpose(1, 0, 3, 2, 4)[:, :, :, None, :, None, :]
    vn = jnp.broadcast_to(vn, (bsz, s_len, 2, 2, n_heads, s_len, hd)).reshape(bsz, s_len, 2 * nr, hd)
    page_rows = lambda c: (c[0].reshape(-1, page, n_heads, 2, hd).transpose(0, 1, 3, 2, 4)
                           .reshape(-1, page * 2 * n_heads, hd))
    o_s = _attn_sample(page_table, q_s, kn, vn, page_rows(cache_k), page_rows(cache_v), lam_params,
                       sub_g, n_heads=n_heads, hd=hd, s_len=s_len, page=page,
                       gp=min(PAGES_PER_STEP, page_table.shape[1]), lam_init=lam_init)
    o_s = o_s.reshape(bsz, n_heads, s_len, hd2).transpose(1, 2, 0, 3).reshape(n_heads, n_s, hd2)
    o = lax.dynamic_update_slice(o, o_s.astype(BF16), (0, t_len, 0))

    gaw, gxw = gate_a_w[0].astype(BF16), gate_x_w[0].astype(BF16)
    lru_args = (conv_w[0], row2(conv_b[0]), gaw, row2(gate_a_b[0]), gxw, row2(gate_x_b[0]),
                row2(lru_lambda[0]))
    yb, h_last_p, w1b1, w2b1 = _lru_prompt(zb, *lru_args, t_len=t_len, tm=TM_LRU,
                                           cast=((w1_all, d, d), (w2_all, d_ff, d_ff)))
    xb_s = zs[:, 3 * aw:3 * aw + bw].reshape(s_len, bsz, bw)
    gb_s = zs[:, 3 * aw + bw:].reshape(s_len, bsz, bw)
    yb_s, h_last_s = _lru_sample(xb_s, gb_s, state_conv[0].transpose(1, 0, 2), state_lru[0], *lru_args)
    yb = lax.dynamic_update_slice(yb, yb_s.reshape(n_s, bw), (t_len, 0))

    x1 = _out_proj(o, yb, x0, w_out_b, row2(mix_ln_g[0]), row2(mix_ln_b[0]), alpha=alpha, tm=TM_OUT)
    x2 = _mlp(x1, w1b0, w2b0, row2(ff_ln_g[0]), row2(ff_ln_b[0]), alpha=alpha, tm=TM_MLP, tf=TF_MLP)

    pool_args = (pool_w[0].astype(BF16), row2(pool_scale[0]), row2(mix_ln_g[1]), row2(mix_ln_b[1]))
    x3 = _pool_prompt(x2, *pool_args, alpha=alpha, tm=TM_POOL)
    x2_s = x2[t_len:t_len + n_s].reshape(s_len, bsz, d)
    x3_s = _pool_sample(x2_s, state_pool[0].transpose(1, 0, 2), *pool_args, alpha=alpha)
    x3 = lax.dynamic_update_slice(x3, x3_s.reshape(n_s, d), (t_len, 0))
    y_p, y_s = _mlp_final(x3, w1b1, w2b1, row2(ff_ln_g[1]), row2(ff_ln_b[1]), alpha=alpha, tm=TM_MLP,
                          tf=TF_MLP, n_meta=n_meta, seq=seq, n_s=n_s)

    to_bt = lambda v: v.reshape(s_len, bsz, -1).transpose(1, 0, 2)
    kv_p = lambda v: (v.reshape(t_len, 2, n_heads, hd).transpose(0, 2, 1, 3)
                      .reshape(1, 1, t_len, n_heads, hd2))
    kv_s = lambda c: to_bt(zs[:, c * aw:(c + 1) * aw]).reshape(1, bsz, s_len, n_heads, hd2)
    y_prompt = y_p[None]
    y_sample = to_bt(y_s)
    new_conv_prompt = zb[t_len - (cw - 1):t_len, :bw][None, None]
    new_lru_prompt = h_last_p[None]
    new_pool_prompt = x2[t_len - n_buf:t_len][None, None]
    new_conv_sample = xb_s[s_len - (cw - 1):].transpose(1, 0, 2)[None]
    new_lru_sample = h_last_s[None]
    new_pool_sample = jnp.concatenate([state_pool[0], x2_s.transpose(1, 0, 2)], axis=1)[:, -n_buf:][None]
    return (y_prompt, y_sample, kv_p(k_rows), kv_p(v_rows), new_conv_prompt, new_lru_prompt, new_pool_prompt,
            kv_s(1), kv_s(2), new_conv_sample, new_lru_sample, new_pool_sample)
```

```python
import functools
import math

import jax
import jax.numpy as jnp
from jax import lax
from jax.experimental import pallas as pl
from jax.experimental.pallas import tpu as pltpu

F32 = jnp.float32
BF16 = jnp.bfloat16

LN_EPS = 1e-5
RMS_EPS = 1e-5
LRU_C = 8.0
POOL_WINDOWS = (2, 4, 8, 16)
NEG = -1e30
LOG2E = 1.4426950408889634
LANES = 128
N_AUG = 3

ROW_ALIGN = 768
T_ATT = 768
TN_PROJ = 1024
TM_LRU = 256
TM_OUT = 384
TM_MLP = 768
TF_MLP = 512
TM_POOL = 256
ATT_HEAD_UNROLL = 2
PAGES_PER_STEP = 8
PAGE_RING = 3
VMEM_MB = 56
VMEM_MB_IN_PROJ = 60


def _params(sem, mb=VMEM_MB):
    return pltpu.CompilerParams(dimension_semantics=sem, vmem_limit_bytes=mb * 2**20)


def _layer_norm(y, g, b):
    mu = jnp.mean(y, axis=-1, keepdims=True)
    d = y - mu
    var = jnp.mean(d * d, axis=-1, keepdims=True)
    return d * lax.rsqrt(var + LN_EPS) * g + b


def _dot(a, b):
    return jnp.dot(a, b, preferred_element_type=F32)


def _dot_nt(a, b):
    return lax.dot_general(a, b, (((1,), (1,)), ((), ())), preferred_element_type=F32)


def _sigmoid(x):
    return 1.0 / (1.0 + jnp.exp(-x))


def _gelu_tanh(x):
    c = math.sqrt(2.0 / math.pi)
    return 0.5 * x * (1.0 + jnp.tanh(c * (x + 0.044715 * (x * x * x))))


def _lambda_value(lq1, lk1, lq2, lk2, lam_init):
    s1 = jnp.sum(lq1 * lk1, axis=-1, keepdims=True)
    s2 = jnp.sum(lq2 * lk2, axis=-1, keepdims=True)
    return jnp.exp(s1) - jnp.exp(s2) + lam_init


def _div_pow2(x, n):
    assert n & (n - 1) == 0
    return lax.shift_right_logical(x, n.bit_length() - 1)


def _mod_pow2(x, n):
    assert n & (n - 1) == 0
    return lax.bitwise_and(x, n - 1)


BF16_ROWS = 16


def _cast_specs(job, n_steps, step_of):
    mat, row_start, rows = job
    cols = mat.shape[1]
    units = rows // BF16_ROWS
    assert units * BF16_ROWS == rows
    n_blocks = max(k for k in range(1, n_steps + 1) if units % k == 0)
    br = rows // n_blocks
    assert row_start % br == 0
    first = row_start // br
    block = lambda *idx: jnp.minimum(step_of(*idx), n_blocks - 1)
    return (pl.BlockSpec((br, cols), lambda *idx: (first + block(*idx), 0)),
            pl.BlockSpec((br, cols), lambda *idx: (block(*idx), 0)))


def _cast_shape(job):
    mat, _, rows = job
    return jax.ShapeDtypeStruct((rows, mat.shape[1]), BF16)


def _cast_blocks(srcs, dsts):
    for src, dst in zip(srcs, dsts):
        dst[...] = src[...].astype(BF16)


def _head_slope(h, n_heads):
    return 2.0 ** (-8.0 * (h + 1) / n_heads)


def _in_proj_kernel(x_ref, w_ref, zt_ref, zb_ref, qk_ref, vb_ref, kr_ref, vr_ref, *, n_heads, hd):
    i = pl.program_id(0)
    j = pl.program_id(1)
    z = _dot(x_ref[...].astype(BF16), w_ref[...])
    tm = z.shape[0]

    @pl.when(j >= 3)
    def _():
        zb_ref[...] = z

    @pl.when(jnp.logical_and(i == pl.num_programs(0) - 1, j < 3))
    def _():
        zt_ref[...] = z

    lane = lax.broadcasted_iota(jnp.int32, (tm, hd), 1)

    def cache_rows(dst_ref):
        for c in range(2):
            for h in range(n_heads):
                col = (2 * h + c) * hd
                dst_ref[pl.ds(c * n_heads + h, tm, stride=2 * n_heads), :] = z[:, col:col + hd]

    @pl.when(j == 0)
    def _():
        ones = jnp.where(lane < N_AUG, 1.0, 0.0).astype(BF16)
        q_scale = hd ** -0.5 * LOG2E
        for h in range(n_heads):
            for c in range(2):
                col = (2 * h + c) * hd
                qk_ref[0, h, :, 2 * hd * c:2 * hd * c + hd] = (z[:, col:col + hd] * q_scale).astype(BF16)
                qk_ref[0, h, :, 2 * hd * c + hd:2 * hd * (c + 1)] = ones

    @pl.when(j == 1)
    def _():
        row = lax.broadcasted_iota(jnp.int32, (tm, hd), 0).astype(F32)
        for h in range(n_heads):
            a = row * (_head_slope(h, n_heads) * LOG2E)
            hi = a.astype(BF16).astype(F32)
            mid = (a - hi).astype(BF16).astype(F32)
            lo = a - hi - mid
            aug = jnp.where(lane == 0, hi, jnp.where(lane == 1, mid, jnp.where(lane == 2, lo, 0.0)))
            aug = aug.astype(BF16)
            for c in range(2):
                col = (2 * h + c) * hd
                qk_ref[0, h, :, 2 * hd * c:2 * hd * c + hd] = z[:, col:col + hd].astype(BF16)
                qk_ref[0, h, :, 2 * hd * c + hd:2 * hd * (c + 1)] = aug
        cache_rows(kr_ref)

    @pl.when(j == 2)
    def _():
        for h in range(n_heads):
            vb_ref[h] = z[:, 2 * hd * h:2 * hd * (h + 1)].astype(BF16)
        cache_rows(vr_ref)


def _in_proj(x, w, *, t_len, tm, tn, n_heads, hd):
    r, d = x.shape
    n = w.shape[1]
    assert hd == LANES and tn == 2 * hd * n_heads and n == 5 * tn
    kr = 2 * n_heads
    n_i = r // tm
    cache_blk = pl.BlockSpec((tm * kr, hd), lambda i, j: (i, 0))
    cache_shape = jax.ShapeDtypeStruct((t_len * kr, hd), F32)
    return pl.pallas_call(
        functools.partial(_in_proj_kernel, n_heads=n_heads, hd=hd),
        grid=(n_i, n // tn),
        in_specs=[pl.BlockSpec((tm, d), lambda i, j: (i, 0)),
                  pl.BlockSpec((d, tn), lambda i, j: (0, j))],
        out_specs=[
                   pl.BlockSpec((tm, tn), lambda i, j: (0, jnp.where(i == n_i - 1, jnp.minimum(j, 2), 0))),
                   pl.BlockSpec((tm, tn), lambda i, j: (i, jnp.maximum(j - 3, 0))),
                   pl.BlockSpec((1, n_heads, tm, 4 * hd), lambda i, j: (jnp.minimum(j, 1), 0, i, 0)),
                   pl.BlockSpec((n_heads, tm, 2 * hd), lambda i, j: (0, i, 0)),
                   cache_blk, cache_blk],
        out_shape=[jax.ShapeDtypeStruct((tm, 3 * tn), F32),
                   jax.ShapeDtypeStruct((r, 2 * tn), F32),
                   jax.ShapeDtypeStruct((2, n_heads, r, 4 * hd), BF16),
                   jax.ShapeDtypeStruct((n_heads, r, 2 * hd), BF16),
                   cache_shape, cache_shape],
        compiler_params=_params(("arbitrary", "arbitrary"), mb=VMEM_MB_IN_PROJ),
        name="in_proj",
    )(x, w)


def _attn_prompt_kernel(qi_ref, kj_ref, qa_ref, ka_ref, vb_ref, sl_ref, lq1_ref, lk1_ref, lq2_ref,
                        lk2_ref, g_ref, *rest, t, n_heads, hd, lam_init, n_cast):
    cast_src = rest[:n_cast]
    o_ref = rest[n_cast]
    cast_dst = rest[n_cast + 1:2 * n_cast + 1]
    m_sc, l_sc, acc_sc = rest[2 * n_cast + 1:]
    _cast_blocks(cast_src, cast_dst)
    p = pl.program_id(1)
    i = qi_ref[p]
    j = kj_ref[p]

    @pl.when(j == 0)
    def _():
        m_sc[...] = jnp.full_like(m_sc, NEG)
        l_sc[...] = jnp.zeros_like(l_sc)
        acc_sc[...] = jnp.zeros_like(acc_sc)

    def tile(masked):
        if masked:
            keep = (lax.broadcasted_iota(jnp.int32, (t, t), 1)
                    <= lax.broadcasted_iota(jnp.int32, (t, t), 0))
        tile_dist = ((j - i) * t).astype(F32)

        def head(h, carry):
            shift = sl_ref[h][:, 0:1] * tile_dist
            for c in range(2):
                s = _dot_nt(qa_ref[h, :, 2 * hd * c:2 * hd * (c + 1)],
                            ka_ref[h, :, 2 * hd * c:2 * hd * (c + 1)])
                if masked:
                    s = jnp.where(keep, s, NEG)
                idx = 2 * h + c
                m_old = m_sc[idx]
                m_new = jnp.maximum(m_old, jnp.max(s, axis=1, keepdims=True) + shift)
                alpha = jnp.exp2(m_old - m_new)
                pr = jnp.exp2(s - (m_new - shift))
                l_sc[idx] = alpha * l_sc[idx] + jnp.sum(pr, axis=1, keepdims=True)
                acc_sc[idx] = alpha * acc_sc[idx] + _dot(pr.astype(BF16), vb_ref[h])
                m_sc[idx] = m_new
            return carry

        for h in range(n_heads):
            head(h, 0)

    @pl.when(j < i)
    def _():
        tile(False)

    @pl.when(j == i)
    def _():
        tile(True)
        lam = _lambda_value(lq1_ref[...], lk1_ref[...], lq2_ref[...], lk2_ref[...], lam_init)
        g = g_ref[...]
        for h in range(n_heads):
            o1 = acc_sc[2 * h] / l_sc[2 * h]
            o2 = acc_sc[2 * h + 1] / l_sc[2 * h + 1]
            oh = o1 - lam * o2
            oh = oh * lax.rsqrt(jnp.mean(oh * oh, axis=-1, keepdims=True) + RMS_EPS)
            o_ref[h] = (oh * g * (1.0 - lam_init)).astype(o_ref.dtype)


def _attn_prompt(qk, vb, lam_params, sub_g, *, t, hd, lam_init, cast=()):
    _, n_heads, r, _ = qk.shape
    nq = r // t
    pairs = [(i, j) for i in range(nq) for j in range(i + 1)]
    qi = jnp.asarray([i for i, _ in pairs], jnp.int32)
    kj = jnp.asarray([j for _, j in pairs], jnp.int32)
    slopes = jnp.asarray([[[_head_slope(h, n_heads) * LOG2E] * LANES] for h in range(n_heads)], F32)
    hg = ATT_HEAD_UNROLL
    n_pairs = len(pairs)
    vec = lambda n: pl.BlockSpec((1, n), lambda g, p, qi, kj: (0, 0))
    cast_specs = [_cast_specs(job, (n_heads // hg) * n_pairs, lambda g, p, qi, kj: g * n_pairs + p)
                  for job in cast]
    grid_spec = pltpu.PrefetchScalarGridSpec(
        num_scalar_prefetch=2,
        grid=(n_heads // hg, n_pairs),
        in_specs=[pl.BlockSpec((None, hg, t, 4 * hd), lambda g, p, qi, kj: (0, g, qi[p], 0)),
                  pl.BlockSpec((None, hg, t, 4 * hd), lambda g, p, qi, kj: (1, g, kj[p], 0)),
                  pl.BlockSpec((hg, t, 2 * hd), lambda g, p, qi, kj: (g, kj[p], 0)),
                  pl.BlockSpec((hg, 1, LANES), lambda g, p, qi, kj: (g, 0, 0)),
                  vec(hd), vec(hd), vec(hd), vec(hd), vec(2 * hd)] + [s[0] for s in cast_specs],
        out_specs=[pl.BlockSpec((hg, t, 2 * hd), lambda g, p, qi, kj: (g, qi[p], 0))]
                  + [s[1] for s in cast_specs],
        scratch_shapes=[pltpu.VMEM((2 * hg, t, 1), F32),
                        pltpu.VMEM((2 * hg, t, 1), F32),
                        pltpu.VMEM((2 * hg, t, 2 * hd), F32)],
    )
    kern = functools.partial(_attn_prompt_kernel, t=t, n_heads=hg, hd=hd, lam_init=lam_init,
                             n_cast=len(cast))
    return pl.pallas_call(
        kern, grid_spec=grid_spec,
        out_shape=[jax.ShapeDtypeStruct((n_heads, r, 2 * hd), BF16)] + [_cast_shape(job) for job in cast],
        compiler_params=_params(("arbitrary", "arbitrary")),
        name="attn_prompt",
    )(qi, kj, qk, qk, vb, slopes, *lam_params, sub_g, *[job[0] for job in cast])


def _attn_sample_kernel(pt_ref, q_ref, kn_ref, vn_ref, *rest, n_heads, hd, s_len, past_len, page,
                        gp, lam_init):
    (ck_hbm, cv_hbm, lq1_ref, lk1_ref, lq2_ref, lk2_ref, g_ref, o_ref,
     qb_sc, bias_sc, m_sc, l_sc, acc_sc, kbuf, vbuf, sem) = rest
    p = pl.program_id(1)
    n_p = pl.num_programs(1)
    step = pl.program_id(0) * n_p + p
    n_steps = pl.num_programs(0) * n_p

    def page_copies(st):
        slot = lax.rem(st, PAGE_RING)
        seq = lax.div(st, n_p)
        first = lax.rem(st, n_p) * gp
        out = []
        for gi in range(gp):
            pid = pt_ref[seq, first + gi]
            out.append(pltpu.make_async_copy(ck_hbm.at[pid], kbuf.at[slot, gi], sem.at[slot, 0]))
            out.append(pltpu.make_async_copy(cv_hbm.at[pid], vbuf.at[slot, gi], sem.at[slot, 1]))
        return out

    def start_step(st):
        for cp in page_copies(st):
            cp.start()

    @pl.when(step == 0)
    def _():
        for ahead in range(PAGE_RING - 1):
            @pl.when(ahead < n_steps)
            def _():
                start_step(step + ahead)

    @pl.when(step + PAGE_RING - 1 < n_steps)
    def _():
        start_step(step + PAGE_RING - 1)

    for cp in page_copies(step):
        cp.wait()
    slot = lax.rem(step, PAGE_RING)
    nr = 2 * n_heads * s_len
    kr = 2 * n_heads
    ncol = page * kr
    half = nr // 2
    scale = hd ** -0.5

    def row_info(shape):
        row = lax.broadcasted_iota(jnp.int32, shape, 0)
        tok = _mod_pow2(row, s_len)
        head = _mod_pow2(_div_pow2(row, s_len), n_heads)
        cmap = _div_pow2(row, s_len * n_heads)
        slope = jnp.zeros(shape, F32)
        for h in range(n_heads):
            slope = jnp.where(head == h, _head_slope(h, n_heads), slope)
        return tok, head, cmap, slope

    @pl.when(p == 0)
    def _():
        qb_sc[...] = (q_ref[0] * scale).astype(BF16)
        tok, head, cmap, slope = row_info((nr, ncol))
        col = lax.broadcasted_iota(jnp.int32, (nr, ncol), 1)
        key = _div_pow2(col, kr)
        own = _mod_pow2(col, kr) == cmap * n_heads + head
        bias_sc[...] = jnp.where(own, -slope * (past_len + tok - key).astype(F32), NEG)
        m_sc[...] = jnp.full_like(m_sc, NEG)
        l_sc[...] = jnp.zeros_like(l_sc)
        acc_sc[...] = jnp.zeros_like(acc_sc)

    tok1, _, _, slope1 = row_info((nr, 1))
    m = m_sc[...]
    l = l_sc[...]
    acc = acc_sc[...]
    scores, shifts = [], []
    m_new = m
    for gi in range(gp):
        kp = kbuf[slot, gi].astype(BF16)
        s = _dot_nt(qb_sc[...], kp) + bias_sc[...]
        shift = slope1 * ((p * gp + gi) * page).astype(F32)
        m_new = jnp.maximum(m_new, jnp.max(s, axis=1, keepdims=True) + shift)
        scores.append(s)
        shifts.append(shift)
    alpha = jnp.exp(m - m_new)
    l = alpha * l
    pv = jnp.zeros((2 * nr, hd), F32)
    for gi in range(gp):
        pr = jnp.exp(scores[gi] - (m_new - shifts[gi]))
        l = l + jnp.sum(pr, axis=1, keepdims=True)
        top, bot = pr[:half], pr[half:]
        chunks = range(ncol // LANES)
        top_r = jnp.concatenate(
            [pltpu.roll(top[:, LANES * cc:LANES * (cc + 1)], n_heads, 1) for cc in chunks], axis=1)
        bot_r = jnp.concatenate(
            [pltpu.roll(bot[:, LANES * cc:LANES * (cc + 1)], LANES - n_heads, 1) for cc in chunks], axis=1)
        lhs = jnp.concatenate([top, bot_r, top_r, bot], axis=0).astype(BF16)
        vp = vbuf[slot, gi].astype(BF16)
        pv = pv + _dot(lhs, vp)
    acc = jnp.concatenate([alpha, alpha], axis=0) * acc + pv
    m = m_new
    m_sc[...] = m
    l_sc[...] = l
    acc_sc[...] = acc

    @pl.when(p == pl.num_programs(1) - 1)
    def _():
        qf = q_ref[0] * scale
        s_new = []
        for t2 in range(s_len):
            st = jnp.sum(qf * kn_ref[0, t2], axis=1, keepdims=True)
            st = st - slope1 * (tok1 - t2).astype(F32)
            s_new.append(jnp.where(t2 <= tok1, st, NEG))
        m_fin = m
        for st in s_new:
            m_fin = jnp.maximum(m_fin, st)
        a_fin = jnp.exp(m - m_fin)
        l_fin = a_fin * l
        acc_f = jnp.concatenate([a_fin, a_fin], axis=0) * acc
        for t2 in range(s_len):
            pt = jnp.exp(s_new[t2] - m_fin)
            l_fin = l_fin + pt
            acc_f = acc_f + jnp.concatenate([pt, pt], axis=0) * vn_ref[0, t2]
        lam = _lambda_value(lq1_ref[...], lk1_ref[...], lq2_ref[...], lk2_ref[...], lam_init)
        halves = []
        for c2 in range(2):
            a = acc_f[c2 * nr:(c2 + 1) * nr] / l_fin
            halves.append(a[:half] - lam * a[half:])
        ms = (jnp.sum(halves[0] * halves[0], axis=1, keepdims=True)
              + jnp.sum(halves[1] * halves[1], axis=1, keepdims=True)) / (2.0 * hd)
        rs = lax.rsqrt(ms + RMS_EPS) * (1.0 - lam_init)
        for c2 in range(2):
            o_ref[0, :, c2 * hd:(c2 + 1) * hd] = halves[c2] * rs * g_ref[:, c2 * hd:(c2 + 1) * hd]


def _attn_sample(page_table, q, kn, vn, ck, cv, lam_params, sub_g, *, n_heads, hd, s_len, page,
                 gp, lam_init):
    bsz, nr, _ = q.shape
    n_pages = page_table.shape[1]
    assert n_pages % gp == 0 and hd == LANES and nr == 2 * n_heads * s_len
    ncol = ck.shape[1]
    vec = lambda n: pl.BlockSpec((1, n), lambda b, p, pt: (0, 0))
    in_hbm = pl.BlockSpec(memory_space=pl.ANY)
    grid_spec = pltpu.PrefetchScalarGridSpec(
        num_scalar_prefetch=1,
        grid=(bsz, n_pages // gp),
        in_specs=[pl.BlockSpec((1, nr, hd), lambda b, p, pt: (b, 0, 0)),
                  pl.BlockSpec((1, s_len, nr, hd), lambda b, p, pt: (b, 0, 0, 0)),
                  pl.BlockSpec((1, s_len, 2 * nr, hd), lambda b, p, pt: (b, 0, 0, 0)),
                  in_hbm, in_hbm,
                  vec(hd), vec(hd), vec(hd), vec(hd), vec(2 * hd)],
        out_specs=pl.BlockSpec((1, nr // 2, 2 * hd), lambda b, p, pt: (b, 0, 0)),
        scratch_shapes=[pltpu.VMEM((nr, hd), BF16),
                        pltpu.VMEM((nr, ncol), F32),
                        pltpu.VMEM((nr, 1), F32),
                        pltpu.VMEM((nr, 1), F32),
                        pltpu.VMEM((2 * nr, hd), F32),
                        pltpu.VMEM((PAGE_RING, gp, ncol, hd), F32),
                        pltpu.VMEM((PAGE_RING, gp, ncol, hd), F32),
                        pltpu.SemaphoreType.DMA((PAGE_RING, 2))],
    )
    kern = functools.partial(_attn_sample_kernel, n_heads=n_heads, hd=hd, s_len=s_len,
                             past_len=n_pages * page, page=page, gp=gp, lam_init=lam_init)
    return pl.pallas_call(
        kern, grid_spec=grid_spec,
        out_shape=jax.ShapeDtypeStruct((bsz, nr // 2, 2 * hd), F32),
        compiler_params=_params(("arbitrary", "arbitrary")),
        name="attn_sample",
    )(page_table, q, kn, vn, ck, cv, *lam_params, sub_g)


def _lru_gates(xc, gaw_ref, gab_ref, gxw_ref, gxb_ref, lam_ref):
    n_blk, bs, _ = gaw_ref.shape
    xcb = xc.astype(BF16)
    ra = jnp.concatenate([_dot(xcb[:, n * bs:(n + 1) * bs], gaw_ref[n]) for n in range(n_blk)], axis=1)
    rx = jnp.concatenate([_dot(xcb[:, n * bs:(n + 1) * bs], gxw_ref[n]) for n in range(n_blk)], axis=1)
    r = _sigmoid(ra + gab_ref[...])
    ig = _sigmoid(rx + gxb_ref[...])
    neg_lam = -lam_ref[...]
    softplus = jnp.maximum(neg_lam, 0.0) + jnp.log1p(jnp.exp(-jnp.abs(neg_lam)))
    log_a = -LRU_C * r * softplus
    a = jnp.exp(log_a)
    mult = jnp.sqrt(-jnp.tanh(log_a) * (a * a + 1.0))
    return a, ig, mult


def _lru_prompt_kernel(xb_ref, gb_ref, cw_ref, cb_ref, gaw_ref, gab_ref, gxw_ref, gxb_ref, lam_ref,
                       *rest, tm, conv_w, last_tile, last_row, n_cast):
    cast_src = rest[:n_cast]
    yb_ref, hl_ref = rest[n_cast:n_cast + 2]
    cast_dst = rest[n_cast + 2:2 * n_cast + 2]
    xbuf, hbuf, hcar = rest[2 * n_cast + 2:]
    _cast_blocks(cast_src, cast_dst)
    i = pl.program_id(0)
    w = xb_ref.shape[1]
    n_grp = tm // 8

    @pl.when(i == 0)
    def _():
        xbuf[0:8, :] = jnp.zeros((8, w), F32)
        hcar[...] = jnp.zeros_like(hcar)

    xbuf[8:8 + tm, :] = xb_ref[...]
    xc = cb_ref[...] + cw_ref[conv_w - 1:conv_w, :] * xb_ref[...]
    for jj in range(conv_w - 1):
        start = 8 - (conv_w - 1) + jj
        xc = xc + cw_ref[jj:jj + 1, :] * xbuf[start:start + tm, :]
    xbuf[0:8, :] = xbuf[tm:tm + 8, :]

    a, ig, mult = _lru_gates(xc, gaw_ref, gab_ref, gxw_ref, gxb_ref, lam_ref)
    grow = i * tm + lax.broadcasted_iota(jnp.int32, (tm, 1), 0)
    mult = jnp.where(grow == 0, 1.0, mult)
    u = xc * ig * mult

    a3 = a.reshape(n_grp, 8, w)
    u3 = u.reshape(n_grp, 8, w)
    sub = lax.broadcasted_iota(jnp.int32, (n_grp, 8, w), 1)
    for sh in (1, 2, 4):
        a_prev = pltpu.roll(a3, sh, axis=1)
        u_prev = pltpu.roll(u3, sh, axis=1)
        ok = sub >= sh
        u3 = jnp.where(ok, a3 * u_prev + u3, u3)
        a3 = jnp.where(ok, a3 * a_prev, a3)
    h_prev = hcar[...]
    for gi in range(n_grp):
        hg = a3[gi] * h_prev + u3[gi]
        hbuf[8 * gi:8 * (gi + 1), :] = hg
        h_prev = hg[7:8, :]
    hcar[...] = h_prev

    yb_ref[...] = (hbuf[...] * _gelu_tanh(gb_ref[...])).astype(yb_ref.dtype)

    @pl.when(i == last_tile)
    def _():
        hl_ref[...] = hbuf[last_row:last_row + 1, :]


def _lru_prompt(z, conv_w, conv_b, gaw, gab, gxw, gxb, lru_lambda, *, t_len, tm, cast=()):
    r = z.shape[0]
    w = conv_b.shape[1]
    cw = conv_w.shape[0]
    n_blk, bs, _ = gaw.shape
    n_steps = r // tm
    full = lambda shape: pl.BlockSpec(shape, lambda i: (0,) * len(shape))
    cast_specs = [_cast_specs(job, n_steps, lambda i: i) for job in cast]
    kern = functools.partial(_lru_prompt_kernel, tm=tm, conv_w=cw, n_cast=len(cast),
                             last_tile=(t_len - 1) // tm, last_row=(t_len - 1) % tm)
    return pl.pallas_call(
        kern,
        grid=(n_steps,),
        in_specs=[pl.BlockSpec((tm, w), lambda i: (i, 0)),
                  pl.BlockSpec((tm, w), lambda i: (i, 1)),
                  full((cw, w)), full((1, w)),
                  full((n_blk, bs, bs)), full((1, w)),
                  full((n_blk, bs, bs)), full((1, w)), full((1, w))] + [s[0] for s in cast_specs],
        out_specs=[pl.BlockSpec((tm, w), lambda i: (i, 0)), full((1, w))] + [s[1] for s in cast_specs],
        out_shape=[jax.ShapeDtypeStruct((r, w), BF16), jax.ShapeDtypeStruct((1, w), F32)]
                  + [_cast_shape(job) for job in cast],
        scratch_shapes=[pltpu.VMEM((tm + 8, w), F32), pltpu.VMEM((tm, w), F32),
                        pltpu.VMEM((1, w), F32)],
        compiler_params=_params(("arbitrary",)),
        name="lru_prompt",
    )(z, z, conv_w, conv_b, gaw, gab, gxw, gxb, lru_lambda, *[job[0] for job in cast])


def _lru_sample_kernel(xb_ref, gb_ref, cs_ref, h0_ref, cw_ref, cb_ref, gaw_ref, gab_ref, gxw_ref,
                       gxb_ref, lam_ref, yb_ref, hl_ref, *, conv_w):
    s_len, bsz, w = xb_ref.shape
    xp = [cs_ref[jj] for jj in range(conv_w - 1)] + [xb_ref[tt] for tt in range(s_len)]
    xcs = []
    for tt in range(s_len):
        xc = cb_ref[...] + cw_ref[0:1, :] * xp[tt]
        for jj in range(1, conv_w):
            xc = xc + cw_ref[jj:jj + 1, :] * xp[tt + jj]
        xcs.append(xc)
    xc = jnp.concatenate(xcs, axis=0)
    a, ig, mult = _lru_gates(xc, gaw_ref, gab_ref, gxw_ref, gxb_ref, lam_ref)
    u = xc * ig * mult
    h = h0_ref[...]
    for tt in range(s_len):
        rows = slice(tt * bsz, (tt + 1) * bsz)
        h = a[rows, :] * h + u[rows, :]
        yb_ref[tt] = (h * _gelu_tanh(gb_ref[tt])).astype(yb_ref.dtype)
    hl_ref[...] = h


def _lru_sample(xb, gb, conv_state, h0, conv_w, conv_b, gaw, gab, gxw, gxb, lru_lambda):
    s_len, bsz, w = xb.shape
    kern = functools.partial(_lru_sample_kernel, conv_w=conv_w.shape[0])
    return pl.pallas_call(
        kern,
        out_shape=[jax.ShapeDtypeStruct((s_len, bsz, w), BF16), jax.ShapeDtypeStruct((bsz, w), F32)],
        compiler_params=pltpu.CompilerParams(vmem_limit_bytes=VMEM_MB * 2**20),
        name="lru_sample",
    )(xb, gb, conv_state, h0, conv_w, conv_b, gaw, gab, gxw, gxb, lru_lambda)


def _out_proj_kernel(o_ref, yb_ref, x_ref, w_ref, g_ref, b_ref, out_ref, *, alpha):
    n_heads, _, hw = o_ref.shape
    aw = n_heads * hw
    mixed = _dot(yb_ref[...], w_ref[aw:, :])
    for h in range(n_heads):
        mixed = mixed + _dot(o_ref[h], w_ref[hw * h:hw * (h + 1), :])
    out_ref[...] = _layer_norm(alpha * x_ref[...] + mixed, g_ref[...], b_ref[...])


def _out_proj(o, yb, x, w, g, b, *, alpha, tm):
    r, d = x.shape
    n_heads, _, hw = o.shape
    bw = yb.shape[1]
    full = lambda shape: pl.BlockSpec(shape, lambda i: (0,) * len(shape))
    return pl.pallas_call(
        functools.partial(_out_proj_kernel, alpha=alpha),
        grid=(r // tm,),
        in_specs=[pl.BlockSpec((n_heads, tm, hw), lambda i: (0, i, 0)),
                  pl.BlockSpec((tm, bw), lambda i: (i, 0)),
                  pl.BlockSpec((tm, d), lambda i: (i, 0)),
                  full((n_heads * hw + bw, d)), full((1, d)), full((1, d))],
        out_specs=pl.BlockSpec((tm, d), lambda i: (i, 0)),
        out_shape=jax.ShapeDtypeStruct((r, d), F32),
        compiler_params=_params(("parallel",)),
        name="out_proj",
    )(o, yb, x, w, g, b)


def _mlp_kernel(x_ref, w1_ref, w2_ref, g_ref, b_ref, out_ref, xb_sc, acc_sc, *, alpha):
    f = pl.program_id(1)

    @pl.when(f == 0)
    def _():
        xb_sc[...] = x_ref[...].astype(BF16)
        acc_sc[...] = jnp.zeros_like(acc_sc)

    h = jnp.maximum(_dot(xb_sc[...], w1_ref[...]), 0.0)
    acc_sc[...] += _dot((h * h).astype(BF16), w2_ref[...])

    @pl.when(f == pl.num_programs(1) - 1)
    def _():
        out_ref[...] = _layer_norm(alpha * x_ref[...] + acc_sc[...], g_ref[...], b_ref[...])


def _mlp_final_kernel(x_ref, w1_ref, w2_ref, g_ref, b_ref, yp_ref, ys_ref, xb_sc, acc_sc, *, alpha,
                      n_meta, n_tail, s_start, n_s):
    i = pl.program_id(0)
    f = pl.program_id(1)
    last = pl.num_programs(0) - 1

    @pl.when(f == 0)
    def _():
        xb_sc[...] = x_ref[...].astype(BF16)
        acc_sc[...] = jnp.zeros_like(acc_sc)

    h = jnp.maximum(_dot(xb_sc[...], w1_ref[...]), 0.0)
    acc_sc[...] += _dot((h * h).astype(BF16), w2_ref[...])

    def result():
        return _layer_norm(alpha * x_ref[...] + acc_sc[...], g_ref[...], b_ref[...])

    @pl.when(jnp.logical_and(f == pl.num_programs(1) - 1, i < last))
    def _():
        yp_ref[...] = result()

    @pl.when(jnp.logical_and(f == pl.num_programs(1) - 1, i == last))
    def _():
        res = result()
        yp_ref[0:n_tail, :] = res[n_meta:n_meta + n_tail, :]
        ys_ref[...] = res[s_start:s_start + n_s, :]


def _mlp(x, w1, w2, g, b, *, alpha, tm, tf):
    r, d = x.shape
    ff = w1.shape[1]
    vec = pl.BlockSpec((1, d), lambda i, f: (0, 0))
    return pl.pallas_call(
        functools.partial(_mlp_kernel, alpha=alpha),
        grid=(r // tm, ff // tf),
        in_specs=[pl.BlockSpec((tm, d), lambda i, f: (i, 0)),
                  pl.BlockSpec((d, tf), lambda i, f: (0, f)),
                  pl.BlockSpec((tf, d), lambda i, f: (f, 0)),
                  vec, vec],
        out_specs=pl.BlockSpec((tm, d), lambda i, f: (i, 0)),
        out_shape=jax.ShapeDtypeStruct((r, d), F32),
        scratch_shapes=[pltpu.VMEM((tm, d), BF16), pltpu.VMEM((tm, d), F32)],
        compiler_params=_params(("parallel", "arbitrary")),
        name="mlp",
    )(x, w1, w2, g, b)


def _mlp_final(x, w1, w2, g, b, *, alpha, tm, tf, n_meta, seq, n_s):
    r, d = x.shape
    ff = w1.shape[1]
    n_i = r // tm
    last_row = (n_i - 1) * tm
    n_tail = seq - last_row
    s_start = n_meta + seq - last_row
    assert pl.cdiv(seq, tm) == n_i and 0 < n_tail <= tm - n_meta and s_start + n_s <= tm
    assert n_meta % 8 == 0
    vec = pl.BlockSpec((1, d), lambda i, f: (0, 0))
    x_rows = lambda i, f: (pl.multiple_of(jnp.where(i < n_i - 1, n_meta + tm * i, last_row), 8), 0)
    kern = functools.partial(_mlp_final_kernel, alpha=alpha, n_meta=n_meta, n_tail=n_tail,
                             s_start=s_start, n_s=n_s)
    return pl.pallas_call(
        kern,
        grid=(n_i, ff // tf),
        in_specs=[pl.BlockSpec((pl.Element(tm), pl.Element(d)), x_rows),
                  pl.BlockSpec((d, tf), lambda i, f: (0, f)),
                  pl.BlockSpec((tf, d), lambda i, f: (f, 0)),
                  vec, vec],
        out_specs=[pl.BlockSpec((tm, d), lambda i, f: (i, 0)),
                   pl.BlockSpec((n_s, d), lambda i, f: (0, 0))],
        out_shape=[jax.ShapeDtypeStruct((seq, d), F32), jax.ShapeDtypeStruct((n_s, d), F32)],
        scratch_shapes=[pltpu.VMEM((tm, d), BF16), pltpu.VMEM((tm, d), F32)],
        compiler_params=_params(("arbitrary", "arbitrary")),
        name="mlp_final",
    )(x, w1, w2, g, b)


def _pool_project(ms, x, w_ref, sc_ref, g_ref, b_ref, alpha):
    y = jnp.concatenate([_dot(ms[gi].astype(BF16), w_ref[gi]) for gi in range(len(ms))], axis=1)
    return _layer_norm(alpha * x + y * sc_ref[...], g_ref[...], b_ref[...])


def _pool_prompt_kernel(x_ref, w_ref, sc_ref, g_ref, b_ref, out_ref, *bufs, tm, alpha, pad):
    i = pl.program_id(0)
    n_g = len(POOL_WINDOWS)
    gw = x_ref.shape[1] // n_g

    @pl.when(i == 0)
    def _():
        for buf in bufs:
            buf[0:pad, :] = jnp.zeros((pad, buf.shape[1]), F32)

    x = x_ref[...]
    pos = i * tm + lax.broadcasted_iota(jnp.int32, (tm, 1), 0)
    ms = []
    cur = x
    for k, win in enumerate(POOL_WINDOWS):
        half = win // 2
        buf = bufs[k]
        buf[pad:pad + tm, :] = cur
        ws = cur + buf[pad - half:pad - half + tm, :]
        buf[0:pad, :] = buf[tm:tm + pad, :]
        cnt = jnp.minimum(win, pos + 1).astype(F32)
        ms.append(ws[:, :gw] / cnt - x[:, k * gw:(k + 1) * gw])
        if k + 1 < n_g:
            cur = ws[:, gw:]
    out_ref[...] = _pool_project(ms, x, w_ref, sc_ref, g_ref, b_ref, alpha)


def _pool_prompt(x, w, sc, g, b, *, alpha, tm):
    r, d = x.shape
    n_g, gw, _ = w.shape
    pad = 16
    assert POOL_WINDOWS == tuple(2 ** (k + 1) for k in range(n_g)) and max(POOL_WINDOWS) // 2 <= pad
    full = lambda shape: pl.BlockSpec(shape, lambda i: (0,) * len(shape))
    return pl.pallas_call(
        functools.partial(_pool_prompt_kernel, tm=tm, alpha=alpha, pad=pad),
        grid=(r // tm,),
        in_specs=[pl.BlockSpec((tm, d), lambda i: (i, 0)),
                  full((n_g, gw, gw)), full((1, d)), full((1, d)), full((1, d))],
        out_specs=pl.BlockSpec((tm, d), lambda i: (i, 0)),
        out_shape=jax.ShapeDtypeStruct((r, d), F32),
        scratch_shapes=[pltpu.VMEM((tm + pad, (n_g - k) * gw), F32) for k in range(n_g)],
        compiler_params=_params(("arbitrary",)),
        name="pool_prompt",
    )(x, w, sc, g, b)


def _pool_sample_kernel(x_ref, st_ref, w_ref, sc_ref, g_ref, b_ref, out_ref, *, alpha):
    s_len, bsz, d = x_ref.shape
    n_buf = st_ref.shape[0]
    gw = d // len(POOL_WINDOWS)
    ext = [st_ref[jj] for jj in range(n_buf)] + [x_ref[tt] for tt in range(s_len)]
    x = jnp.concatenate(ext[n_buf:], axis=0)
    ms = []
    for gi, win in enumerate(POOL_WINDOWS):
        cols = slice(gi * gw, (gi + 1) * gw)
        rows = []
        for tt in range(s_len):
            ws = ext[n_buf + tt][:, cols]
            for kk in range(1, win):
                ws = ws + ext[n_buf + tt - kk][:, cols]
            rows.append(ws / float(win) - ext[n_buf + tt][:, cols])
        ms.append(jnp.concatenate(rows, axis=0))
    y = _pool_project(ms, x, w_ref, sc_ref, g_ref, b_ref, alpha)
    for tt in range(s_len):
        out_ref[tt] = y[tt * bsz:(tt + 1) * bsz, :]


def _pool_sample(x, state, w, sc, g, b, *, alpha):
    return pl.pallas_call(
        functools.partial(_pool_sample_kernel, alpha=alpha),
        out_shape=jax.ShapeDtypeStruct(x.shape, F32),
        compiler_params=pltpu.CompilerParams(vmem_limit_bytes=VMEM_MB * 2**20),
        name="pool_sample",
    )(x, state, w, sc, g, b)


def kernel(x_prompt, x_sample, cache_k, cache_v, state_conv, state_lru, state_pool, page_table, meta_tokens, w_in, lam_q1, lam_k1, lam_q2, lam_k2, sub_norm_g, conv_w, conv_b, gate_a_w, gate_a_b, gate_x_w, gate_x_b, lru_lambda, w_out_ab, pool_w, pool_scale, mix_ln_g, mix_ln_b, w_ff1, w_ff2, ff_ln_g, ff_ln_b):
    n_prompt, seq, d = x_prompt.shape
    bsz, s_len, _ = x_sample.shape
    depth = w_ff1.shape[0]
    assert n_prompt == 1 and depth == 2
    n_meta = meta_tokens.shape[0]
    n_heads, hd2 = cache_k.shape[-2:]
    hd = hd2 // 2
    aw = n_heads * hd2
    bw = state_lru.shape[-1]
    assert aw == bw and w_in.shape[-1] == 3 * aw + 2 * bw
    page = cache_k.shape[2]
    n_buf = state_pool.shape[2]
    cw = conv_w.shape[1]
    assert s_len >= cw - 1
    alpha = (2.0 * depth) ** 0.25
    lam_init = 0.8 - 0.6 * math.exp(-0.3 * 0)

    t_len = n_meta + seq
    n_s = bsz * s_len
    rows = -(-(t_len + n_s) // ROW_ALIGN) * ROW_ALIGN
    xs_tm = x_sample.transpose(1, 0, 2).reshape(n_s, d)
    x0 = jnp.concatenate([meta_tokens, x_prompt[0], xs_tm,
                          jnp.zeros((rows - t_len - n_s, d), F32)], axis=0)

    row2 = lambda v: v.reshape(1, -1)
    lam_params = (row2(lam_q1[0]), row2(lam_k1[0]), row2(lam_q2[0]), row2(lam_k2[0]))
    sub_g = row2(sub_norm_g[0])

    assert rows - T_ATT < t_len
    zt, zb, qk, vb, k_rows, v_rows = _in_proj(x0, w_in[0].astype(BF16), t_len=t_len, tm=T_ATT,
                                              tn=TN_PROJ, n_heads=n_heads, hd=hd)
    s0 = t_len - (rows - T_ATT)
    assert s0 + n_s <= T_ATT
    zs = jnp.concatenate([zt[s0:s0 + n_s], zb[t_len:t_len + n_s]], axis=1)

    d_ff = w_ff1.shape[2]
    w1_all, w2_all = w_ff1.reshape(depth * d, d_ff), w_ff2.reshape(depth * d_ff, d)
    o, w1b0, w2b0, w_out_b = _attn_prompt(
        qk, vb, lam_params, sub_g, t=T_ATT, hd=hd, lam_init=lam_init,
        cast=((w1_all, 0, d), (w2_all, 0, d_ff), (w_out_ab[0], 0, d)))

    split = lambda c: zs[:, c * aw:(c + 1) * aw].reshape(s_len, bsz, n_heads, 2, hd)
    nr = 2 * n_heads * s_len
    q_s = split(0).transpose(1, 3, 2, 0, 4).reshape(bsz, nr, hd)
    kn = split(1).transpose(1, 0, 3, 2, 4)[:, :, :, :, None, :]
    kn = jnp.broadcast_to(kn, (bsz, s_len, 2, n_heads, s_len, hd)).reshape(bsz, s_len, nr, hd)
    vn = split(2).transpose(1, 0, 3, 2, 4)[:, :, :, None, :, None, :]
    vn = jnp.broadcast_to(vn, (bsz, s_len, 2, 2, n_heads, s_len, hd)).reshape(bsz, s_len, 2 * nr, hd)
    page_rows = lambda c: (c[0].reshape(-1, page, n_heads, 2, hd).transpose(0, 1, 3, 2, 4)
                           .reshape(-1, page * 2 * n_heads, hd))
    o_s = _attn_sample(page_table, q_s, kn, vn, page_rows(cache_k), page_rows(cache_v), lam_params,
                       sub_g, n_heads=n_heads, hd=hd, s_len=s_len, page=page,
                       gp=min(PAGES_PER_STEP, page_table.shape[1]), lam_init=lam_init)
    o_s = o_s.reshape(bsz, n_heads, s_len, hd2).transpose(1, 2, 0, 3).reshape(n_heads, n_s, hd2)
    o = lax.dynamic_update_slice(o, o_s.astype(BF16), (0, t_len, 0))

    gaw, gxw = gate_a_w[0].astype(BF16), gate_x_w[0].astype(BF16)
    lru_args = (conv_w[0], row2(conv_b[0]), gaw, row2(gate_a_b[0]), gxw, row2(gate_x_b[0]),
                row2(lru_lambda[0]))
    yb, h_last_p, w1b1, w2b1 = _lru_prompt(zb, *lru_args, t_len=t_len, tm=TM_LRU,
                                           cast=((w1_all, d, d), (w2_all, d_ff, d_ff)))
    xb_s = zs[:, 3 * aw:3 * aw + bw].reshape(s_len, bsz, bw)
    gb_s = zs[:, 3 * aw + bw:].reshape(s_len, bsz, bw)
    yb_s, h_last_s = _lru_sample(xb_s, gb_s, state_conv[0].transpose(1, 0, 2), state_lru[0], *lru_args)
    yb = lax.dynamic_update_slice(yb, yb_s.reshape(n_s, bw), (t_len, 0))

    x1 = _out_proj(o, yb, x0, w_out_b, row2(mix_ln_g[0]), row2(mix_ln_b[0]), alpha=alpha, tm=TM_OUT)
    x2 = _mlp(x1, w1b0, w2b0, row2(ff_ln_g[0]), row2(ff_ln_b[0]), alpha=alpha, tm=TM_MLP, tf=TF_MLP)

    pool_args = (pool_w[0].astype(BF16), row2(pool_scale[0]), row2(mix_ln_g[1]), row2(mix_ln_b[1]))
    x3 = _pool_prompt(x2, *pool_args, alpha=alpha, tm=TM_POOL)
    x2_s = x2[t_len:t_len + n_s].reshape(s_len, bsz, d)
    x3_s = _pool_sample(x2_s, state_pool[0].transpose(1, 0, 2), *pool_args, alpha=alpha)
    x3 = lax.dynamic_update_slice(x3, x3_s.reshape(n_s, d), (t_len, 0))
    y_p, y_s = _mlp_final(x3, w1b1, w2b1, row2(ff_ln_g[1]), row2(ff_ln_b[1]), alpha=alpha, tm=TM_MLP,
                          tf=TF_MLP, n_meta=n_meta, seq=seq, n_s=n_s)

    to_bt = lambda v: v.reshape(s_len, bsz, -1).transpose(1, 0, 2)
    kv_p = lambda v: (v.reshape(t_len, 2, n_heads, hd).transpose(0, 2, 1, 3)
                      .reshape(1, 1, t_len, n_heads, hd2))
    kv_s = lambda c: to_bt(zs[:, c * aw:(c + 1) * aw]).reshape(1, bsz, s_len, n_heads, hd2)
    y_prompt = y_p[None]
    y_sample = to_bt(y_s)
    new_conv_prompt = zb[t_len - (cw - 1):t_len, :bw][None, None]
    new_lru_prompt = h_last_p[None]
    new_pool_prompt = x2[t_len - n_buf:t_len][None, None]
    new_conv_sample = xb_s[s_len - (cw - 1):].transpose(1, 0, 2)[None]
    new_lru_sample = h_last_s[None]
    new_pool_sample = jnp.concatenate([state_pool[0], x2_s.transpose(1, 0, 2)], axis=1)[:, -n_buf:][None]
    return (y_prompt, y_sample, kv_p(k_rows), kv_p(v_rows), new_conv_prompt, new_lru_prompt, new_pool_prompt,
            kv_s(1), kv_s(2), new_conv_sample, new_lru_sample, new_pool_sample)
```
